```python
import math
import jax, jax.numpy as jnp
from jax import lax
import numpy as np

D_MODEL = 1024
BATCH = 1
SEQ = 16384
DEPTH = 1
DEC_BATCH = 128
DEC_SEQ = 4
PAST_LEN = 8192
PAGE_SIZE = 128

POOL_WIDTH = D_MODEL // 2
POOL_WINDOWS = (2, 4, 8, 16)
POOL_GROUPS = len(POOL_WINDOWS)
POOL_GROUP_DIM = POOL_WIDTH // POOL_GROUPS
POOL_STATE = max(POOL_WINDOWS) - 1
HEAD_DIM = 64
N_HEADS = (D_MODEL // 2) // HEAD_DIM
DILATION_PAIRS = ((128, 1), (512, 4), (2048, 16))
N_DIL = len(DILATION_PAIRS)
W_MAX = max(w for w, _ in DILATION_PAIRS)
ATTN_WIDTH = N_HEADS * HEAD_DIM
QKV_WIDTH = N_DIL * N_HEADS * HEAD_DIM
ROT_DIM = HEAD_DIM // 4
ROPE_THETA = 500000.0
Q_BLOCK = 128
N_BRANCH = 2
IN_COLS = POOL_WIDTH + 3 * QKV_WIDTH + N_BRANCH * D_MODEL
N_EXPERT_GROUPS = 4
EXPERTS_PER_GROUP = 8
N_EXPERTS = N_EXPERT_GROUPS * EXPERTS_PER_GROUP
TOP_K = 2
D_EXPERT = D_MODEL // 2
MOE_BLOCK = 128
EPS = 1e-6

kernel_name = "pool_dilated_attn_hier_moe_step"


def rms_norm(x, w):
    x32 = x.astype(jnp.float32)
    y = x32 * lax.rsqrt(jnp.mean(x32 * x32, axis=-1, keepdims=True) + EPS)
    return (y * w.astype(jnp.float32)).astype(x.dtype)


def partial_rope(x, pos):
    half = ROT_DIM // 2
    inv = ROPE_THETA ** (-(jnp.arange(half, dtype=jnp.float32) * 2.0 / ROT_DIM))
    ang = pos.astype(jnp.float32)[:, None] * inv[None, :]
    cos = jnp.cos(ang)[None, :, None, None, :]
    sin = jnp.sin(ang)[None, :, None, None, :]
    x32 = x.astype(jnp.float32)
    x1, x2 = x32[..., :half], x32[..., half:ROT_DIM]
    out = jnp.concatenate([x1 * cos - x2 * sin, x1 * sin + x2 * cos, x32[..., ROT_DIM:]], axis=-1)
    return out.astype(x.dtype)


def project(xn, w_in, q_norm_w, k_norm_w, pos):
    b, t, _ = xn.shape
    proj = jnp.einsum('btd,dc->btc', xn, w_in)
    o1 = POOL_WIDTH
    o2 = o1 + QKV_WIDTH
    o3 = o2 + QKV_WIDTH
    o4 = o3 + QKV_WIDTH
    hs = (b, t, N_DIL, N_HEADS, HEAD_DIM)
    pool_in = proj[..., :o1]
    q = proj[..., o1:o2].reshape(hs)
    k = proj[..., o2:o3].reshape(hs)
    v = proj[..., o3:o4].reshape(hs)
    gates = jax.nn.sigmoid(proj[..., o4:].astype(jnp.float32)).astype(xn.dtype).reshape(b, t, N_BRANCH, D_MODEL)
    q = partial_rope(rms_norm(q, q_norm_w[:, None, :]), pos)
    k = partial_rope(rms_norm(k, k_norm_w[:, None, :]), pos)
    return pool_in, q, k, v, gates


def pool_mix(u_ext, pos, pool_w, pool_scale):
    b, L, _ = u_ext.shape
    t = L - POOL_STATE
    u32 = u_ext.astype(jnp.float32)
    cs = jnp.concatenate([jnp.zeros((b, 1, POOL_WIDTH), jnp.float32), jnp.cumsum(u32, axis=1)], axis=1)
    hi = cs[:, POOL_STATE + 1:]
    cur = u32[:, POOL_STATE:]
    outs = []
    for gi, w in enumerate(POOL_WINDOWS):
        sl = slice(gi * POOL_GROUP_DIM, (gi + 1) * POOL_GROUP_DIM)
        lo = cs[:, POOL_STATE + 1 - w:POOL_STATE + 1 - w + t, sl]
        cnt = jnp.minimum(pos + 1, w).astype(jnp.float32)[None, :, None]
        outs.append((hi[..., sl] - lo) / cnt - cur[..., sl])
    pooled = jnp.stack(outs, axis=2)
    mixed = jnp.einsum('btgc,gce->btge', pooled, pool_w.astype(jnp.float32))
    return (mixed.reshape(b, t, POOL_WIDTH) * pool_scale.astype(jnp.float32)).astype(u_ext.dtype)


def gather_attend(q, k_ctx, v_ctx, q_index, valid_start, dil, n_keys):
    offs = jnp.arange(n_keys, dtype=jnp.int32) * dil
    idx = q_index[:, None] - offs[None, :]
    valid = idx >= valid_start
    idx = jnp.maximum(idx, 0)
    kk = k_ctx[:, idx].astype(jnp.float32)
    vv = v_ctx[:, idx].astype(jnp.float32)
    s = jnp.einsum('bqhd,bqnhd->bqhn', q.astype(jnp.float32), kk) * (HEAD_DIM ** -0.5)
    s = jnp.where(valid[None, :, None, :], s, -jnp.inf)
    m = jnp.max(s, axis=-1, keepdims=True)
    p = jnp.exp(s - m)
    l = jnp.sum(p, axis=-1, keepdims=True)
    o = jnp.einsum('bqhn,bqnhd->bqhd', p, vv) / l
    lse = (m + jnp.log(l))[..., 0]
    return o, lse


def combine_groups(outs, lses):
    w = jax.nn.softmax(jnp.stack(lses, axis=0), axis=0)
    return jnp.sum(w[..., None] * jnp.stack(outs, axis=0), axis=0)


def dilated_attention_prompt(q, k, v):
    b, s = q.shape[:2]
    pad = ((0, 0), (W_MAX, 0), (0, 0), (0, 0))
    kps = [jnp.pad(k[:, :, g], pad) for g in range(N_DIL)]
    vps = [jnp.pad(v[:, :, g], pad) for g in range(N_DIL)]
    qs = [q[:, :, g] for g in range(N_DIL)]

    def block(i):
        t0 = i * Q_BLOCK
        outs, lses = [], []
        for g, (win, dil) in enumerate(DILATION_PAIRS):
            start = t0 + W_MAX - win
            kc = lax.dynamic_slice_in_dim(kps[g], start, Q_BLOCK + win, axis=1)
            vc = lax.dynamic_slice_in_dim(vps[g], start, Q_BLOCK + win, axis=1)
            qb = lax.dynamic_slice_in_dim(qs[g], t0, Q_BLOCK, axis=1)
            o, l = gather_attend(qb, kc, vc, win + jnp.arange(Q_BLOCK, dtype=jnp.int32), win - t0, dil, win // dil + 1)
            outs.append(o)
            lses.append(l)
        return combine_groups(outs, lses)

    ob = lax.map(block, jnp.arange(s // Q_BLOCK, dtype=jnp.int32))
    return jnp.moveaxis(ob, 0, 1).reshape(b, s, ATTN_WIDTH).astype(q.dtype)


def dilated_attention_sample(q, k, v, caches):
    b, t = q.shape[:2]
    outs, lses, new_caches = [], [], []
    for g, (win, dil) in enumerate(DILATION_PAIRS):
        cache = caches[g].astype(k.dtype)
        L = cache.shape[1]
        kc = jnp.concatenate([cache[:, :, 0], k[:, :, g]], axis=1)
        vc = jnp.concatenate([cache[:, :, 1], v[:, :, g]], axis=1)
        o, l = gather_attend(q[:, :, g], kc, vc, L + jnp.arange(t, dtype=jnp.int32), 0, dil, win // dil + 1)
        outs.append(o)
        lses.append(l)
        new_caches.append(jnp.stack([kc[:, t:], vc[:, t:]], axis=2))
    out = combine_groups(outs, lses).reshape(b, t, ATTN_WIDTH).astype(q.dtype)
    return out, new_caches


def merge_branches(ya, yb, gates, w_bp, w_ba, w_o):
    pa = jnp.einsum('btc,cd->btd', ya, w_bp)
    pb = jnp.einsum('btc,cd->btd', yb, w_ba)
    merged = gates[:, :, 0] * pa + gates[:, :, 1] * pb
    return jnp.einsum('btd,de->bte', merged, w_o)


def expert_dispatch(h, e_idx, e_w, w_g, w_u, w_d):
    T, D = h.shape
    n_assign = T * TOP_K
    flat_e = e_idx.reshape(-1)
    flat_w = e_w.reshape(-1)
    flat_tok = jnp.repeat(jnp.arange(T, dtype=jnp.int32), TOP_K)
    order = jnp.argsort(flat_e)
    sorted_e = flat_e[order]
    counts = jnp.bincount(flat_e, length=N_EXPERTS)
    padded = (counts + MOE_BLOCK - 1) // MOE_BLOCK * MOE_BLOCK
    padded_end = jnp.cumsum(padded)
    padded_start = padded_end - padded
    start = jnp.cumsum(counts) - counts
    dest = padded_start[sorted_e] + jnp.arange(n_assign, dtype=jnp.int32) - start[sorted_e]
    n_blocks = -(-(n_assign + N_EXPERTS * (MOE_BLOCK - 1)) // MOE_BLOCK)
    n_rows = n_blocks * MOE_BLOCK
    row_tok = jnp.full((n_rows,), T, jnp.int32).at[dest].set(flat_tok[order])
    row_w = jnp.zeros((n_rows,), jnp.float32).at[dest].set(flat_w[order])
    block_e = jnp.minimum(jnp.searchsorted(padded_end, jnp.arange(n_blocks) * MOE_BLOCK, side='right'), N_EXPERTS - 1)
    h_pad = jnp.concatenate([h, jnp.zeros((1, D), h.dtype)], axis=0)
    xb = h_pad[row_tok].reshape(n_blocks, MOE_BLOCK, D)

    def run(args):
        xblk, e = args
        return (jax.nn.silu(xblk @ w_g[e]) * (xblk @ w_u[e])) @ w_d[e]

    yb = lax.map(run, (xb, block_e)).reshape(n_rows, D)
    y = yb.astype(jnp.float32) * row_w[:, None]
    return jnp.zeros((T + 1, D), jnp.float32).at[row_tok].add(y)[:T].astype(h.dtype)


def hier_moe(h, w_rg, b_rg, w_re, b_re, w_g, w_u, w_d):
    T = h.shape[0]
    h32 = h.astype(jnp.float32)
    p_group = jax.nn.softmax(h32 @ w_rg.astype(jnp.float32) + b_rg.astype(jnp.float32), axis=-1)
    g_sel = jnp.argmax(p_group, axis=-1)
    g_w = jnp.take_along_axis(p_group, g_sel[:, None], axis=-1)
    e_logits = (h32 @ w_re.astype(jnp.float32) + b_re.astype(jnp.float32)).reshape(T, N_EXPERT_GROUPS, EXPERTS_PER_GROUP)
    e_logits = jnp.take_along_axis(e_logits, g_sel[:, None, None], axis=1)[:, 0]
    top_v, top_i = lax.top_k(e_logits, TOP_K)
    e_w = jax.nn.softmax(top_v, axis=-1) * g_w
    e_idx = (g_sel[:, None] * EXPERTS_PER_GROUP + top_i).astype(jnp.int32)
    return expert_dispatch(h, e_idx, e_w, w_g, w_u, w_d)


def setup_inputs(seed: int = 0) -> dict:
    key = jax.random.key(seed)
    ks = jax.random.split(key, 24)
    f32 = jnp.float32

    def nrm(k, shape, scale):
        return jax.random.normal(k, shape, f32) * scale

    cl = [min(win, PAST_LEN) for win, _ in DILATION_PAIRS]
    return {
        'x_prompt': nrm(ks[0], (BATCH, SEQ, D_MODEL), 1.0),
        'x_sample': nrm(ks[1], (DEC_BATCH, DEC_SEQ, D_MODEL), 1.0),
        'cache_kv_w128': nrm(ks[2], (DEPTH, DEC_BATCH, cl[0], 2, N_HEADS, HEAD_DIM), 1.0),
        'cache_kv_w512': nrm(ks[3], (DEPTH, DEC_BATCH, cl[1], 2, N_HEADS, HEAD_DIM), 1.0),
        'cache_kv_w2048': nrm(ks[4], (DEPTH, DEC_BATCH, cl[2], 2, N_HEADS, HEAD_DIM), 1.0),
        'state_pool': nrm(ks[5], (DEPTH, DEC_BATCH, POOL_STATE, POOL_WIDTH), 1.0),
        'norm1_w': 1.0 + nrm(ks[6], (DEPTH, D_MODEL), 0.05),
        'w_in': nrm(ks[7], (DEPTH, D_MODEL, IN_COLS), D_MODEL ** -0.5),
        'q_norm_w': 1.0 + nrm(ks[8], (DEPTH, N_DIL, HEAD_DIM), 0.05),
        'k_norm_w': 1.0 + nrm(ks[9], (DEPTH, N_DIL, HEAD_DIM), 0.05),
        'pool_w': nrm(ks[10], (DEPTH, POOL_GROUPS, POOL_GROUP_DIM, POOL_GROUP_DIM), POOL_GROUP_DIM ** -0.5),
        'pool_scale': 1.0 + nrm(ks[11], (DEPTH, POOL_WIDTH), 0.1),
        'w_branch_pool': nrm(ks[12], (DEPTH, POOL_WIDTH, D_MODEL), POOL_WIDTH ** -0.5),
        'w_branch_attn': nrm(ks[13], (DEPTH, ATTN_WIDTH, D_MODEL), ATTN_WIDTH ** -0.5),
        'w_out': nrm(ks[14], (DEPTH, D_MODEL, D_MODEL), D_MODEL ** -0.5),
        'norm2_w': 1.0 + nrm(ks[15], (DEPTH, D_MODEL), 0.05),
        'w_router_group': nrm(ks[16], (DEPTH, D_MODEL, N_EXPERT_GROUPS), D_MODEL ** -0.5),
        'b_router_group': nrm(ks[17], (DEPTH, N_EXPERT_GROUPS), 0.01),
        'w_router_expert': nrm(ks[18], (DEPTH, D_MODEL, N_EXPERTS), D_MODEL ** -0.5),
        'b_router_expert': nrm(ks[19], (DEPTH, N_EXPERTS), 0.01),
        'w_expert_gate': nrm(ks[20], (DEPTH, N_EXPERTS, D_MODEL, D_EXPERT), D_MODEL ** -0.5),
        'w_expert_up': nrm(ks[21], (DEPTH, N_EXPERTS, D_MODEL, D_EXPERT), D_MODEL ** -0.5),
        'w_expert_down': nrm(ks[22], (DEPTH, N_EXPERTS, D_EXPERT, D_MODEL), D_EXPERT ** -0.5),
    }


def reference(x_prompt, x_sample, cache_kv_w128, cache_kv_w512, cache_kv_w2048, state_pool,
              norm1_w, w_in, q_norm_w, k_norm_w, pool_w, pool_scale, w_branch_pool, w_branch_attn,
              w_out, norm2_w, w_router_group, b_router_group, w_router_expert, b_router_expert,
              w_expert_gate, w_expert_up, w_expert_down):
    xp, xs = x_prompt, x_sample
    bp, sp, _ = xp.shape
    bs, ss, _ = xs.shape
    pos_p = jnp.arange(sp, dtype=jnp.int32)
    pos_s = PAST_LEN + jnp.arange(ss, dtype=jnp.int32)
    kv_caches = (cache_kv_w128, cache_kv_w512, cache_kv_w2048)
    pool_p, pool_s = [], []
    kv_p = [[] for _ in DILATION_PAIRS]
    kv_s = [[] for _ in DILATION_PAIRS]
    for layer in range(DEPTH):
        xn = rms_norm(xp, norm1_w[layer])
        u, q, k, v, gates = project(xn, w_in[layer], q_norm_w[layer], k_norm_w[layer], pos_p)
        ya = pool_mix(jnp.pad(u, ((0, 0), (POOL_STATE, 0), (0, 0))), pos_p, pool_w[layer], pool_scale[layer])
        yb = dilated_attention_prompt(q, k, v)
        xp = xp + merge_branches(ya, yb, gates, w_branch_pool[layer], w_branch_attn[layer], w_out[layer])
        pool_p.append(u[:, sp - POOL_STATE:])
        for g, (win, _) in enumerate(DILATION_PAIRS):
            keep = min(win, sp)
            kv_p[g].append(jnp.stack([k[:, sp - keep:, g], v[:, sp - keep:, g]], axis=2))
        xn = rms_norm(xs, norm1_w[layer])
        u, q, k, v, gates = project(xn, w_in[layer], q_norm_w[layer], k_norm_w[layer], pos_s)
        u_ext = jnp.concatenate([state_pool[layer].astype(u.dtype), u], axis=1)
        ya = pool_mix(u_ext, pos_s, pool_w[layer], pool_scale[layer])
        yb, new_kv = dilated_attention_sample(q, k, v, [c[layer] for c in kv_caches])
        xs = xs + merge_branches(ya, yb, gates, w_branch_pool[layer], w_branch_attn[layer], w_out[layer])
        pool_s.append(u_ext[:, u_ext.shape[1] - POOL_STATE:])
        for g in range(N_DIL):
            kv_s[g].append(new_kv[g])
        h = jnp.concatenate([rms_norm(xp, norm2_w[layer]).reshape(bp * sp, D_MODEL),
                             rms_norm(xs, norm2_w[layer]).reshape(bs * ss, D_MODEL)], axis=0)
        m = hier_moe(h, w_router_group[layer], b_router_group[layer], w_router_expert[layer],
                     b_router_expert[layer], w_expert_gate[layer], w_expert_up[layer], w_expert_down[layer])
        xp = xp + m[:bp * sp].reshape(xp.shape)
        xs = xs + m[bp * sp:].reshape(xs.shape)
    return (xp, xs, jnp.stack(pool_p), jnp.stack(pool_s),
            jnp.stack(kv_p[0]), jnp.stack(kv_s[0]),
            jnp.stack(kv_p[1]), jnp.stack(kv_s[1]),
            jnp.stack(kv_p[2]), jnp.stack(kv_s[2]))
```

```python
import functools

import numpy as np
import jax
import jax.numpy as jnp
from jax import lax
from jax.experimental import pallas as pl
from jax.experimental.pallas import tpu as pltpu

F32 = jnp.float32
BF16 = jnp.bfloat16

D_MODEL = 1024
PAST_LEN = 8192
POOL_WIDTH = 512
POOL_WINDOWS = (2, 4, 8, 16)
POOL_GROUP_DIM = 128
POOL_STATE = 15
HEAD_DIM = 64
N_HEADS = 8
DILATION_PAIRS = ((128, 1), (512, 4), (2048, 16))
N_DIL = 3
ATTN_WIDTH = 512
QKV_WIDTH = 1536
ROT_DIM = 16
ROPE_THETA = 500000.0
N_BRANCH = 2
IN_COLS = POOL_WIDTH + 3 * QKV_WIDTH + N_BRANCH * D_MODEL
N_EXPERT_GROUPS = 4
EXPERTS_PER_GROUP = 8
N_EXPERTS = 32
D_EXPERT = 512
EPS = 1e-6

LANES = 128
VMEM_LIMIT_BYTES = 56 * 1024 * 1024

C_U = 0
C_Q = POOL_WIDTH
C_K = C_Q + QKV_WIDTH
C_V = C_K + QKV_WIDTH
C_G = C_V + QKV_WIDTH

N_CLASS = 16
TI = 256
ROWS_PER_CLASS = TI // N_CLASS
NEG = -1e30


def _class_order():
    return [(c // 4) + 4 * (c % 4) for c in range(N_CLASS)]


def _const_spec(shape, single_buffer=False):
    nd = len(shape)
    kw = {}
    if single_buffer:
        kw["pipeline_mode"] = pl.Buffered(1)
    return pl.BlockSpec(shape, lambda *_: (0,) * nd, **kw)


def _rms(x, w):
    ms = jnp.mean(x * x, axis=-1, keepdims=True)
    return x * lax.rsqrt(ms + EPS) * w


def _head_norm_rope(y, wrow, seg, segt, cos, sina, sinb):
    sq = (y * y).astype(BF16)
    ssum = jnp.dot(sq, seg, preferred_element_type=F32)
    r = lax.rsqrt(ssum * (1.0 / HEAD_DIM) + EPS)
    r_hi = r.astype(BF16)
    r_lo = (r - r_hi.astype(F32)).astype(BF16)
    rexp = (jnp.dot(r_hi, segt, preferred_element_type=F32)
            + jnp.dot(r_lo, segt, preferred_element_type=F32))
    yn = y * rexp * wrow
    outs = []
    for j in range(QKV_WIDTH // LANES):
        c = yn[:, j * LANES:(j + 1) * LANES]
        outs.append(c * cos + pltpu.roll(c, LANES - ROT_DIM // 2, 1) * sina
                    + pltpu.roll(c, ROT_DIM // 2, 1) * sinb)
    return jnp.concatenate(outs, axis=1)


def _pool_mix_out(pooled, poolw_ref, pool_scale):
    outs = []
    for g in range(len(POOL_WINDOWS)):
        outs.append(jnp.dot(pooled[g].astype(BF16), poolw_ref[g], preferred_element_type=F32))
    return jnp.concatenate(outs, axis=1) * pool_scale


def _proj_prompt_kernel(x_ref, n1w_ref, w_ref, qw_ref, kw_ref, cos_ref, sina_ref, sinb_ref,
                        perm_ref, seg_ref, segt_ref, poolw_ref, pools_ref,
                        ya_ref, gates_ref, q_ref, k_ref, v_ref, kvt_ref, ut_ref,
                        uext_ref):
    n = pl.program_id(0)
    xn = _rms(x_ref[...], n1w_ref[...]).astype(BF16)

    u = jnp.dot(xn, w_ref[:, C_U:C_U + POOL_WIDTH], preferred_element_type=F32)

    @pl.when(n == 0)
    def _():
        uext_ref[0:16, :] = jnp.zeros((16, POOL_WIDTH), F32)

    uext_ref[16:16 + TI, :] = u
    pos = n * TI + lax.broadcasted_iota(jnp.int32, (TI, 1), 0)
    pooled = []
    for g, w in enumerate(POOL_WINDOWS):
        lanes = slice(g * POOL_GROUP_DIM, (g + 1) * POOL_GROUP_DIM)
        s = uext_ref[16:16 + TI, lanes]
        for m in range(1, w):
            s = s + uext_ref[16 - m:16 - m + TI, lanes]
        cnt = jnp.minimum(pos + 1, w).astype(F32)
        pooled.append(s / cnt - u[:, lanes])
    ya_ref[...] = _pool_mix_out(pooled, poolw_ref, pools_ref[...]).astype(BF16)
    ut_ref[...] = u[TI - 16:, :]
    uext_ref[0:16, :] = uext_ref[TI:TI + 16, :]

    gl = jnp.dot(xn, w_ref[:, C_G:C_G + N_BRANCH * D_MODEL], preferred_element_type=F32)
    gates_ref[...] = jax.nn.sigmoid(gl).astype(BF16)

    xp = jnp.dot(perm_ref[...], xn, preferred_element_type=F32).astype(BF16)
    cos, sina, sinb = cos_ref[...], sina_ref[...], sinb_ref[...]
    seg, segt = seg_ref[...], segt_ref[...]
    shp = (N_CLASS, ROWS_PER_CLASS, QKV_WIDTH)
    yq = jnp.dot(xp, w_ref[:, C_Q:C_Q + QKV_WIDTH], preferred_element_type=F32)
    q_ref[...] = _head_norm_rope(yq, qw_ref[...], seg, segt, cos, sina, sinb).astype(BF16).reshape(shp)
    yk = jnp.dot(xp, w_ref[:, C_K:C_K + QKV_WIDTH], preferred_element_type=F32)
    kr = _head_norm_rope(yk, kw_ref[...], seg, segt, cos, sina, sinb)
    k_ref[...] = kr.astype(BF16).reshape(shp)
    kvt_ref[:, :, 0:QKV_WIDTH] = kr.reshape(shp)
    yv = jnp.dot(xp, w_ref[:, C_V:C_V + QKV_WIDTH], preferred_element_type=F32)
    v_ref[...] = yv.astype(BF16).reshape(shp)
    kvt_ref[:, :, QKV_WIDTH:2 * QKV_WIDTH] = yv.reshape(shp)


def _rope_tables(pos):
    half = ROT_DIM // 2
    inv = ROPE_THETA ** (-(jnp.arange(half, dtype=F32) * 2.0 / ROT_DIM))
    ang = pos.astype(F32)[:, None] * inv[None, :]
    c, s = jnp.cos(ang), jnp.sin(ang)
    n = pos.shape[0]
    ones = jnp.ones((n, HEAD_DIM - ROT_DIM), F32)
    zeros = jnp.zeros((n, HEAD_DIM - ROT_DIM), F32)
    z8 = jnp.zeros((n, half), F32)
    cos64 = jnp.concatenate([c, c, ones], axis=1)
    sina64 = jnp.concatenate([-s, z8, zeros], axis=1)
    sinb64 = jnp.concatenate([z8, s, zeros], axis=1)
    rep = lambda t: jnp.concatenate([t, t], axis=1)
    return rep(cos64), rep(sina64), rep(sinb64)


def _seg_mats():
    h = np.arange(QKV_WIDTH) // HEAD_DIM
    seg = (h[:, None] == np.arange(LANES)[None, :]).astype(np.float32)
    return jnp.asarray(seg, BF16), jnp.asarray(seg.T, BF16)


def _perm_mat():
    cls = _class_order()
    p = np.zeros((TI, TI), np.float32)
    for c in range(N_CLASS):
        for j in range(ROWS_PER_CLASS):
            p[c * ROWS_PER_CLASS + j, N_CLASS * j + cls[c]] = 1.0
    return p


def _proj_prompt(x, n1w, w_bf, qw_row, kw_row, poolw_bf, pool_scale):
    s = x.shape[0]
    nt = s // TI
    cls = np.asarray(_class_order())
    perm = _perm_mat()
    j = np.arange(ROWS_PER_CLASS)
    inner = (N_CLASS * j[None, :] + cls[:, None]).reshape(-1)
    pos_perm = (np.arange(nt)[:, None] * TI + inner[None, :]).reshape(-1)
    cos, sina, sinb = _rope_tables(jnp.asarray(pos_perm, jnp.int32))
    seg, segt = _seg_mats()
    n_i = s // N_CLASS
    tail_i = DILATION_PAIRS[-1][0] // N_CLASS
    tail_blocks = tail_i // ROWS_PER_CLASS
    row = lambda a: a.reshape(1, -1)
    tile = lambda n: (n, 0)
    rblk = lambda n: (0, n, 0)
    tblk = lambda n: (0, jnp.maximum(n - (nt - tail_blocks), 0), 0)
    outs = pl.pallas_call(
        _proj_prompt_kernel,
        grid=(nt,),
        in_specs=[
            pl.BlockSpec((TI, D_MODEL), tile),
            _const_spec((1, D_MODEL)),
            _const_spec((D_MODEL, IN_COLS), single_buffer=True),
            _const_spec((1, QKV_WIDTH)),
            _const_spec((1, QKV_WIDTH)),
            pl.BlockSpec((TI, LANES), tile),
            pl.BlockSpec((TI, LANES), tile),
            pl.BlockSpec((TI, LANES), tile),
            _const_spec((TI, TI)),
            _const_spec((QKV_WIDTH, LANES)),
            _const_spec((LANES, QKV_WIDTH)),
            _const_spec((len(POOL_WINDOWS), POOL_GROUP_DIM, POOL_GROUP_DIM)),
            _const_spec((1, POOL_WIDTH)),
        ],
        out_specs=[
            pl.BlockSpec((TI, POOL_WIDTH), tile),
            pl.BlockSpec((TI, N_BRANCH * D_MODEL), tile),
            pl.BlockSpec((N_CLASS, ROWS_PER_CLASS, QKV_WIDTH), rblk),
            pl.BlockSpec((N_CLASS, ROWS_PER_CLASS, QKV_WIDTH), rblk),
            pl.BlockSpec((N_CLASS, ROWS_PER_CLASS, QKV_WIDTH), rblk),
            pl.BlockSpec((N_CLASS, ROWS_PER_CLASS, 2 * QKV_WIDTH), tblk),
            _const_spec((16, POOL_WIDTH)),
        ],
        out_shape=[
            jax.ShapeDtypeStruct((s, POOL_WIDTH), BF16),
            jax.ShapeDtypeStruct((s, N_BRANCH * D_MODEL), BF16),
            jax.ShapeDtypeStruct((N_CLASS, n_i, QKV_WIDTH), BF16),
            jax.ShapeDtypeStruct((N_CLASS, n_i, QKV_WIDTH), BF16),
            jax.ShapeDtypeStruct((N_CLASS, n_i, QKV_WIDTH), BF16),
            jax.ShapeDtypeStruct((N_CLASS, tail_i, 2 * QKV_WIDTH), F32),
            jax.ShapeDtypeStruct((16, POOL_WIDTH), F32),
        ],
        scratch_shapes=[pltpu.VMEM((16 + TI, POOL_WIDTH), F32)],
        compiler_params=pltpu.CompilerParams(
            dimension_semantics=("arbitrary",), vmem_limit_bytes=VMEM_LIMIT_BYTES),
        name="proj_prompt",
    )(x, row(n1w), w_bf, row(qw_row), row(kw_row), cos, sina, sinb,
      jnp.asarray(perm, BF16), seg, segt, poolw_bf, row(pool_scale))
    return outs


AT_I = 128
N_KEYS_BACK = 128


def _attn_update(q, k, v, bias, m_old, l_old, acc_old):
    nq = q.shape[0]
    lane = lax.broadcasted_iota(jnp.int32, (nq, LANES), 1)
    first = lane < HEAD_DIM
    m_cur, l_cur, pv = [], [], []
    for hp in range(N_HEADS // 2):
        sl = slice(hp * LANES, (hp + 1) * LANES)
        qp, kp, vp = q[:, sl], k[:, sl], v[:, sl]
        m_pair, l_pair, pv_pair = [], [], []
        for e in range(2):
            h = 2 * hp + e
            qm = jnp.where(first if e == 0 else jnp.logical_not(first), qp, jnp.zeros_like(qp))
            s = lax.dot_general(qm, kp, (((1,), (1,)), ((), ())), preferred_element_type=F32) + bias
            mc = jnp.max(s, axis=1, keepdims=True)
            mn = jnp.maximum(m_old[:, h * HEAD_DIM:h * HEAD_DIM + 1], mc)
            p = jnp.exp(s - mn)
            l_pair.append(jnp.sum(p, axis=1, keepdims=True))
            m_pair.append(mc)
            pv_pair.append(jnp.dot(p.astype(BF16), vp, preferred_element_type=F32))
        m_cur.append(jnp.where(first, m_pair[0], m_pair[1]))
        l_cur.append(jnp.where(first, l_pair[0], l_pair[1]))
        pv.append(jnp.where(first, pv_pair[0], pv_pair[1]))
    m_cur = jnp.concatenate(m_cur, axis=1)
    l_cur = jnp.concatenate(l_cur, axis=1)
    pv = jnp.concatenate(pv, axis=1)
    m_new = jnp.maximum(m_old, m_cur)
    alpha = jnp.exp(m_old - m_new)
    return m_new, alpha * l_old + l_cur, alpha * acc_old + pv


def _attn_prompt_kernel(q_ref, kc_ref, kp_ref, vc_ref, vp_ref, b0_ref, b1_ref, b2_ref,
                        r0_ref, r1_ref, r2_ref, o_ref,
                        kcat, vcat, acc, m_s, l_s, ost):
    n = pl.program_id(0)
    g = pl.program_id(1)
    i0 = n * AT_I
    kcat[:, 0:AT_I, :] = kp_ref[...]
    kcat[:, AT_I:2 * AT_I, :] = kc_ref[...]
    vcat[:, 0:AT_I, :] = vp_ref[...]
    vcat[:, AT_I:2 * AT_I, :] = vc_ref[...]

    def key_bias(b_ref, r_ref, base):
        return b_ref[...] + jnp.where(base + r_ref[...] >= 0, 0.0, NEG)

    @pl.when(g == 0)
    def _():
        m_s[...] = jnp.full(m_s.shape, NEG, F32)
        l_s[...] = jnp.zeros(l_s.shape, F32)
        acc[...] = jnp.zeros(acc.shape, F32)

        def body(s, carry):
            qo = pl.multiple_of(16 * s, 16)
            ko = pl.multiple_of(AT_I - 16 + 16 * s, 16)
            cat = lambda f: jnp.concatenate([f(c) for c in range(N_CLASS)], axis=0)
            q = cat(lambda c: q_ref[c, pl.ds(qo, 16), :])
            k = cat(lambda c: kcat[c, pl.ds(ko, 32), :])
            v = cat(lambda c: vcat[c, pl.ds(ko, 32), :])
            mo = cat(lambda c: m_s[c, pl.ds(qo, 16), :])
            lo = cat(lambda c: l_s[c, pl.ds(qo, 16), :])
            ao = cat(lambda c: acc[c, pl.ds(qo, 16), :])
            bias = key_bias(b0_ref, r0_ref, i0 + 16 * s)
            mn, ln, an = _attn_update(q, k, v, bias, mo, lo, ao)
            for c in range(N_CLASS):
                m_s[c, pl.ds(qo, 16), :] = mn[16 * c:16 * c + 16]
                l_s[c, pl.ds(qo, 16), :] = ln[16 * c:16 * c + 16]
                acc[c, pl.ds(qo, 16), :] = an[16 * c:16 * c + 16]
            return carry

        lax.fori_loop(0, AT_I // 16, body, 0)

    @pl.when(g == 1)
    def _():
        def body(idx, carry):
            r4 = idx // 4
            s = idx % 4
            qo = pl.multiple_of(32 * s, 32)
            ko = pl.multiple_of(AT_I - 32 + 32 * s, 32)
            cat = lambda f: jnp.concatenate([f(4 * r4 + a) for a in range(4)], axis=0)
            q = cat(lambda c: q_ref[c, pl.ds(qo, 32), :])
            k = cat(lambda c: kcat[c, pl.ds(ko, 64), :])
            v = cat(lambda c: vcat[c, pl.ds(ko, 64), :])
            mo = cat(lambda c: m_s[c, pl.ds(qo, 32), :])
            lo = cat(lambda c: l_s[c, pl.ds(qo, 32), :])
            ao = cat(lambda c: acc[c, pl.ds(qo, 32), :])
            bias = key_bias(b1_ref, r1_ref, i0 + 32 * s)
            mn, ln, an = _attn_update(q, k, v, bias, mo, lo, ao)
            for a in range(4):
                c = 4 * r4 + a
                m_s[c, pl.ds(qo, 32), :] = mn[32 * a:32 * a + 32]
                l_s[c, pl.ds(qo, 32), :] = ln[32 * a:32 * a + 32]
                acc[c, pl.ds(qo, 32), :] = an[32 * a:32 * a + 32]
            return carry

        lax.fori_loop(0, 16, body, 0)

    @pl.when(g == 2)
    def _():
        bias = key_bias(b2_ref, r2_ref, i0)

        def body(c, carry):
            mn, ln, an = _attn_update(q_ref[c], kcat[c], vcat[c], bias, m_s[c], l_s[c], acc[c])
            m_s[c] = mn
            l_s[c] = ln
            acc[c] = an
            return carry

        lax.fori_loop(0, N_CLASS, body, 0)

        cls = _class_order()
        for c in range(N_CLASS):
            o = acc[c] / l_s[c]
            for jc in range(ATTN_WIDTH // LANES):
                ost[jc, pl.ds(cls[c], AT_I, stride=N_CLASS), :] = o[:, jc * LANES:(jc + 1) * LANES]
        o_ref[...] = jnp.concatenate([ost[jc] for jc in range(ATTN_WIDTH // LANES)], axis=1).astype(BF16)


def _attn_bias_tables():
    cls = np.asarray(_class_order())

    def bias(diff):
        return np.where((diff >= 0) & (diff <= N_KEYS_BACK), 0.0, NEG).astype(np.float32)

    c, r = np.divmod(np.arange(256), 16)
    c2, r2 = np.divmod(np.arange(512), 32)
    d0 = 16 * (r[:, None] - r2[None, :] + 16) + cls[c][:, None] - cls[c2][None, :]
    rel0 = (r2 - 16).astype(np.int32)[None, :]
    a, r = np.divmod(np.arange(128), 32)
    a2, r2 = np.divmod(np.arange(256), 64)
    d1 = 4 * (r[:, None] - r2[None, :] + 32) + a[:, None] - a2[None, :]
    rel1 = (r2 - 32).astype(np.int32)[None, :]
    r = np.arange(128)
    r2 = np.arange(256)
    d2 = r[:, None] - r2[None, :] + 128
    rel2 = (r2 - 128).astype(np.int32)[None, :]
    return (jnp.asarray(bias(d0)), jnp.asarray(bias(d1)), jnp.asarray(bias(d2)),
            jnp.asarray(rel0), jnp.asarray(rel1), jnp.asarray(rel2))


def _attn_prompt(q_r, k_r, v_r):
    n_i = q_r.shape[1]
    s = n_i * N_CLASS
    nsteps = n_i // AT_I
    b0, b1, b2, r0, r1, r2 = _attn_bias_tables()
    blk = (N_CLASS, AT_I, ATTN_WIDTH)
    cur = lambda n, g: (0, n, g)
    prev = lambda n, g: (0, jnp.maximum(n - 1, 0), g)
    return pl.pallas_call(
        _attn_prompt_kernel,
        grid=(nsteps, N_DIL),
        in_specs=[
            pl.BlockSpec(blk, cur), pl.BlockSpec(blk, cur), pl.BlockSpec(blk, prev),
            pl.BlockSpec(blk, cur), pl.BlockSpec(blk, prev),
            _const_spec(b0.shape), _const_spec(b1.shape), _const_spec(b2.shape),
            _const_spec(r0.shape), _const_spec(r1.shape), _const_spec(r2.shape),
        ],
        out_specs=pl.BlockSpec((AT_I * N_CLASS, ATTN_WIDTH), lambda n, g: (n, 0)),
        out_shape=jax.ShapeDtypeStruct((s, ATTN_WIDTH), BF16),
        scratch_shapes=[
            pltpu.VMEM((N_CLASS, 2 * AT_I, ATTN_WIDTH), BF16),
            pltpu.VMEM((N_CLASS, 2 * AT_I, ATTN_WIDTH), BF16),
            pltpu.VMEM((N_CLASS, AT_I, ATTN_WIDTH), F32),
            pltpu.VMEM((N_CLASS, AT_I, ATTN_WIDTH), F32),
            pltpu.VMEM((N_CLASS, AT_I, ATTN_WIDTH), F32),
            pltpu.VMEM((ATTN_WIDTH // LANES, AT_I * N_CLASS, LANES), F32),
        ],
        compiler_params=pltpu.CompilerParams(
            dimension_semantics=("arbitrary", "arbitrary"), vmem_limit_bytes=VMEM_LIMIT_BYTES),
        name="attn_prompt",
    )(q_r, k_r, k_r, v_r, v_r, b0, b1, b2, r0, r1, r2)


def _proj_sample_kernel(n_b, n_t, x_ref, n1w_ref, w_ref, qw_ref, kw_ref, cos_ref, sina_ref, sinb_ref,
                        seg_ref, segt_ref, poolw_ref, pools_ref, state_ref,
                        ya_ref, gates_ref, q_ref, k_ref, v_ref, u_ref):
    xn = _rms(x_ref[...], n1w_ref[...]).astype(BF16)
    u = jnp.dot(xn, w_ref[:, C_U:C_U + POOL_WIDTH], preferred_element_type=F32)
    u_ref[...] = u

    def ext(j, lanes):
        if j < POOL_STATE:
            return state_ref[j, :, lanes]
        return u[(j - POOL_STATE) * n_b:(j - POOL_STATE + 1) * n_b, lanes]

    pooled = []
    for g, w in enumerate(POOL_WINDOWS):
        lanes = slice(g * POOL_GROUP_DIM, (g + 1) * POOL_GROUP_DIM)
        rows = []
        for i in range(n_t):
            s = ext(POOL_STATE + i, lanes)
            for m in range(1, w):
                s = s + ext(POOL_STATE + i - m, lanes)
            cnt = float(min(PAST_LEN + i + 1, w))
            rows.append(s / cnt - ext(POOL_STATE + i, lanes))
        pooled.append(jnp.concatenate(rows, axis=0))
    ya_ref[...] = _pool_mix_out(pooled, poolw_ref, pools_ref[...]).astype(BF16)

    gl = jnp.dot(xn, w_ref[:, C_G:C_G + N_BRANCH * D_MODEL], preferred_element_type=F32)
    gates_ref[...] = jax.nn.sigmoid(gl).astype(BF16)

    cos, sina, sinb = cos_ref[...], sina_ref[...], sinb_ref[...]
    seg, segt = seg_ref[...], segt_ref[...]
    yq = jnp.dot(xn, w_ref[:, C_Q:C_Q + QKV_WIDTH], preferred_element_type=F32)
    q_ref[...] = _head_norm_rope(yq, qw_ref[...], seg, segt, cos, sina, sinb)
    yk = jnp.dot(xn, w_ref[:, C_K:C_K + QKV_WIDTH], preferred_element_type=F32)
    k_ref[...] = _head_norm_rope(yk, kw_ref[...], seg, segt, cos, sina, sinb)
    v_ref[...] = jnp.dot(xn, w_ref[:, C_V:C_V + QKV_WIDTH], preferred_element_type=F32)


def _proj_sample(x, n1w, w_bf, qw_row, kw_row, poolw_bf, pool_scale, state_t, n_b, n_t):
    rows = n_b * n_t
    pos = PAST_LEN + jnp.arange(rows, dtype=jnp.int32) // n_b
    cos, sina, sinb = _rope_tables(pos)
    seg, segt = _seg_mats()
    row = lambda a: a.reshape(1, -1)
    args = (x, row(n1w), w_bf, row(qw_row), row(kw_row), cos, sina, sinb, seg, segt,
            poolw_bf, row(pool_scale), state_t)
    f32_qkv = jax.ShapeDtypeStruct((rows, QKV_WIDTH), F32)
    out_shape = [
        jax.ShapeDtypeStruct((rows, POOL_WIDTH), BF16),
        jax.ShapeDtypeStruct((rows, N_BRANCH * D_MODEL), BF16),
        f32_qkv, f32_qkv, f32_qkv,
        jax.ShapeDtypeStruct((rows, POOL_WIDTH), F32),
    ]
    return pl.pallas_call(
        functools.partial(_proj_sample_kernel, n_b, n_t),
        grid=(1,),
        in_specs=[_const_spec(a.shape) for a in args],
        out_specs=[_const_spec(o.shape) for o in out_shape],
        out_shape=out_shape,
        compiler_params=pltpu.CompilerParams(
            dimension_semantics=("arbitrary",), vmem_limit_bytes=VMEM_LIMIT_BYTES),
        name="proj_sample",
    )(*args)


def _attn_sample_kernel(q_ref, kn_ref, vn_ref, c0_ref, c1_ref, c2_ref, b0_ref, b1_ref, b2_ref, o_ref):
    caches = (c0_ref, c1_ref, c2_ref)
    biases = (b0_ref, b1_ref, b2_ref)
    s_all, v_all = [], []
    for g in range(N_DIL):
        c = caches[g]
        if g == 2:
            kc = c[:, :, 0].reshape(-1, HEAD_DIM)
            vc = c[:, :, 1].reshape(-1, HEAD_DIM)
        else:
            kc = c[:, 0].reshape(-1, HEAD_DIM)
            vc = c[:, 1].reshape(-1, HEAD_DIM)
        k = jnp.concatenate([kc, kn_ref[g]], axis=0).astype(BF16)
        v = jnp.concatenate([vc, vn_ref[g]], axis=0).astype(BF16)
        q = q_ref[g].astype(BF16)
        s = lax.dot_general(q, k, (((1,), (1,)), ((), ())), preferred_element_type=F32)
        s_all.append(s + biases[g][...])
        v_all.append(v)
    m = functools.reduce(jnp.maximum, [jnp.max(s, axis=1, keepdims=True) for s in s_all])
    l = jnp.zeros_like(m)
    o = jnp.zeros((q_ref.shape[1], HEAD_DIM), F32)
    for s, v in zip(s_all, v_all):
        p = jnp.exp(s - m)
        l = l + jnp.sum(p, axis=1, keepdims=True)
        o = o + jnp.dot(p.astype(BF16), v, preferred_element_type=F32)
    o_ref[...] = o / l


def _attn_sample_bias(n_t):
    nq = n_t * N_HEADS
    qi, qh = np.divmod(np.arange(nq), N_HEADS)

    def fin(valid):
        return jnp.asarray(np.where(valid, 0.0, NEG).astype(np.float32))

    out = []
    for g, (win, dil) in enumerate(DILATION_PAIRS):
        length = min(win, PAST_LEN)
        if g == 2:
            cols = (length // N_CLASS) * n_t * N_HEADS
            rest, kh = np.divmod(np.arange(cols), N_HEADS)
            m_idx, cls = np.divmod(rest, n_t)
            row = m_idx * N_CLASS + cls
        else:
            cols = length * N_HEADS
            row, kh = np.divmod(np.arange(cols), N_HEADS)
        delta = length + qi[:, None] - row[None, :]
        valid_c = (qh[:, None] == kh[None, :]) & (delta % dil == 0) & (delta // dil >= 1) & (delta // dil <= win // dil)
        nj, nh = np.divmod(np.arange(nq), N_HEADS)
        dn = qi[:, None] - nj[None, :]
        valid_n = (qh[:, None] == nh[None, :]) & (dn >= 0) & (dn % dil == 0) & (dn // dil <= win // dil)
        out.append(fin(np.concatenate([valid_c, valid_n], axis=1)))
    return out


def _attn_sample(q_b, kn_b, vn_b, c128, c512, c2048):
    n_b, _, nq, _ = q_b.shape
    n_t = nq // N_HEADS
    b0, b1, b2 = _attn_sample_bias(n_t)
    c2v = c2048.reshape(n_b, c2048.shape[1] // N_CLASS, N_CLASS, 2, N_HEADS, HEAD_DIM)
    per_b = lambda shape: pl.BlockSpec((None,) + shape, lambda b: (b,) + (0,) * len(shape))
    return pl.pallas_call(
        _attn_sample_kernel,
        grid=(n_b,),
        in_specs=[
            per_b((N_DIL, nq, HEAD_DIM)), per_b((N_DIL, nq, HEAD_DIM)), per_b((N_DIL, nq, HEAD_DIM)),
            per_b(c128.shape[1:]), per_b(c512.shape[1:]),
            per_b((c2v.shape[1], n_t, 2, N_HEADS, HEAD_DIM)),
            _const_spec(b0.shape), _const_spec(b1.shape), _const_spec(b2.shape),
        ],
        out_specs=per_b((nq, HEAD_DIM)),
        out_shape=jax.ShapeDtypeStruct((n_b, nq, HEAD_DIM), F32),
        compiler_params=pltpu.CompilerParams(
            dimension_semantics=("arbitrary",), vmem_limit_bytes=VMEM_LIMIT_BYTES),
        name="attn_sample",
    )(q_b, kn_b, vn_b, c128, c512, c2v, b0, b1, b2)


CACHE_DMA_CHUNKS = 4


def _cache_roll_kernel(n_t, c0, c1, c2, n0, n1, n2, o0, o1, o2, sem):
    copies = []
    for c, nw, o in ((c0, n0, o0), (c1, n1, o1), (c2, n2, o2)):
        n_b, length = c.shape[0], c.shape[1]
        step = n_b // CACHE_DMA_CHUNKS
        for ch in range(CACHE_DMA_CHUNKS):
            bs = pl.ds(ch * step, step)
            copies.append(pltpu.make_async_copy(
                c.at[bs, pl.ds(n_t, length - n_t)], o.at[bs, pl.ds(0, length - n_t)], sem))
            copies.append(pltpu.make_async_copy(nw.at[bs], o.at[bs, pl.ds(length - n_t, n_t)], sem))
    for cp in copies:
        cp.start()
    for cp in copies:
        cp.wait()


def _cache_roll(caches, news, n_t):
    anyspec = pl.BlockSpec(memory_space=pl.ANY)
    return pl.pallas_call(
        functools.partial(_cache_roll_kernel, n_t),
        in_specs=[anyspec] * 6,
        out_specs=[anyspec] * 3,
        out_shape=[jax.ShapeDtypeStruct(c.shape, c.dtype) for c in caches],
        scratch_shapes=[pltpu.SemaphoreType.DMA],
        name="cache_roll",
    )(*caches, *news)


TM = 512
MOE_ROWS = 256
L_E0, L_E1, L_W0, L_W1, L_R0, L_R1 = 0, 1, 2, 3, 4, 5
L_GROUP = N_EXPERTS


def _lane_min_index(mask, lane):
    return jnp.min(jnp.where(mask, lane, float(LANES)), axis=1, keepdims=True)


def _merge_route_kernel(n_first, xa_ref, xb_ref, yaa_ref, yab_ref, yba_ref, ybb_ref, ga_ref, gb_ref,
                        wbp_ref, wba_ref, wo_ref, n2w_ref, wrh_ref, wrl_ref, br_ref, ltri_ref,
                        x1_ref, h_ref, route_ref, cnt_ref):
    n = pl.program_id(0)
    first = n < n_first
    pick = lambda a, b: jnp.where(first, a[...], b[...])
    pa = jnp.dot(pick(yaa_ref, yab_ref), wbp_ref[...], preferred_element_type=F32)
    pb = jnp.dot(pick(yba_ref, ybb_ref), wba_ref[...], preferred_element_type=F32)
    gts = pick(ga_ref, gb_ref)
    merged = gts[:, :D_MODEL].astype(F32) * pa + gts[:, D_MODEL:].astype(F32) * pb
    x1 = pick(xa_ref, xb_ref) + jnp.dot(merged.astype(BF16), wo_ref[...], preferred_element_type=F32)
    x1_ref[...] = x1
    h = _rms(x1, n2w_ref[...])
    h_ref[...] = h

    h_hi = h.astype(BF16)
    h_lo = (h - h_hi.astype(F32)).astype(BF16)
    logits = (jnp.dot(h_hi, wrh_ref[...], preferred_element_type=F32)
              + jnp.dot(h_hi, wrl_ref[...], preferred_element_type=F32)
              + jnp.dot(h_lo, wrh_ref[...], preferred_element_type=F32)) + br_ref[...]
    rows = logits.shape[0]
    lane = lax.broadcasted_iota(jnp.int32, (rows, LANES), 1).astype(F32)
    is_group = (lane >= L_GROUP) & (lane < L_GROUP + N_EXPERT_GROUPS)
    gl = jnp.where(is_group, logits, NEG)
    gmax = jnp.max(gl, axis=1, keepdims=True)
    g_w = 1.0 / jnp.sum(jnp.exp(gl - gmax), axis=1, keepdims=True)
    g_sel = _lane_min_index(gl == gmax, lane) - L_GROUP
    in_group = (lane >= g_sel * EXPERTS_PER_GROUP) & (lane < (g_sel + 1.0) * EXPERTS_PER_GROUP)
    el = jnp.where(in_group, logits, NEG)
    v0 = jnp.max(el, axis=1, keepdims=True)
    e0 = _lane_min_index(el == v0, lane)
    el2 = jnp.where(lane == e0, NEG, el)
    v1 = jnp.max(el2, axis=1, keepdims=True)
    e1 = _lane_min_index(el2 == v1, lane)
    t = jnp.exp(v1 - v0)
    w0 = g_w / (1.0 + t)
    w1 = g_w * t / (1.0 + t)

    hot0 = lane == e0
    hot1 = lane == e1
    onehot = jnp.where(hot0 | hot1, 1.0, 0.0)

    @pl.when(n == 0)
    def _():
        cnt_ref[...] = jnp.zeros(cnt_ref.shape, F32)

    before = jnp.dot(ltri_ref[...], onehot.astype(BF16), preferred_element_type=F32) + cnt_ref[...]
    r0 = jnp.sum(jnp.where(hot0, before, 0.0), axis=1, keepdims=True)
    r1 = jnp.sum(jnp.where(hot1, before, 0.0), axis=1, keepdims=True)
    cnt_ref[...] = cnt_ref[...] + jnp.sum(onehot, axis=0, keepdims=True)

    rec = jnp.zeros((rows, LANES), F32)
    for ln, val in ((L_E0, e0), (L_E1, e1), (L_W0, w0), (L_W1, w1), (L_R0, r0), (L_R1, r1)):
        rec = jnp.where(lane == ln, val, rec)
    route_ref[...] = rec


def _router_weights(w_rg, b_rg, w_re, b_re):
    w = jnp.zeros((D_MODEL, LANES), F32)
    w = w.at[:, :N_EXPERTS].set(w_re).at[:, L_GROUP:L_GROUP + N_EXPERT_GROUPS].set(w_rg)
    b = jnp.zeros((1, LANES), F32)
    b = b.at[0, :N_EXPERTS].set(b_re).at[0, L_GROUP:L_GROUP + N_EXPERT_GROUPS].set(b_rg)
    w_hi = w.astype(BF16)
    w_lo = (w - w_hi.astype(F32)).astype(BF16)
    return w_hi, w_lo, b


def _merge_route(xa, xb, yaa, yab, yba, ybb, ga, gb, wbp, wba, wo, n2w, wr_hi, wr_lo, br):
    n_first = xa.shape[0] // TM
    t_all = xa.shape[0] + xb.shape[0]
    ltri = jnp.asarray(np.tril(np.ones((TM, TM), np.float32), -1), BF16)
    ta = lambda w: pl.BlockSpec((TM, w), lambda n: (jnp.minimum(n, n_first - 1), 0))
    tb = lambda w: pl.BlockSpec((TM, w), lambda n: (jnp.maximum(n - n_first, 0), 0))
    tile = lambda w: pl.BlockSpec((TM, w), lambda n: (n, 0))
    return pl.pallas_call(
        functools.partial(_merge_route_kernel, n_first),
        grid=(t_all // TM,),
        in_specs=[
            ta(D_MODEL), tb(D_MODEL), ta(POOL_WIDTH), tb(POOL_WIDTH), ta(ATTN_WIDTH), tb(ATTN_WIDTH),
            ta(N_BRANCH * D_MODEL), tb(N_BRANCH * D_MODEL),
            _const_spec(wbp.shape), _const_spec(wba.shape), _const_spec(wo.shape),
            _const_spec((1, D_MODEL)),
            _const_spec(wr_hi.shape), _const_spec(wr_lo.shape), _const_spec(br.shape),
            _const_spec(ltri.shape),
        ],
        out_specs=[tile(D_MODEL), tile(D_MODEL), tile(LANES), _const_spec((1, LANES))],
        out_shape=[
            jax.ShapeDtypeStruct((t_all, D_MODEL), F32),
            jax.ShapeDtypeStruct((t_all, D_MODEL), F32),
            jax.ShapeDtypeStruct((t_all, LANES), F32),
            jax.ShapeDtypeStruct((1, LANES), F32),
        ],
        compiler_params=pltpu.CompilerParams(
            dimension_semantics=("arbitrary",), vmem_limit_bytes=VMEM_LIMIT_BYTES),
        name="merge_route",
    )(xa, xb, yaa, yab, yba, ybb, ga, gb, wbp, wba, wo, n2w.reshape(1, -1), wr_hi, wr_lo, br, ltri)


def _moe_blocks(n_tokens):
    return -(-(2 * n_tokens + N_EXPERTS * (MOE_ROWS - 1)) // MOE_ROWS)


def _plan_kernel(nb_pad, cnt_ref, route_ref, upper_ref, dest_ref, blk_ref, xs_ref, zbuf, sem):
    n = pl.program_id(0)
    cnt = cnt_ref[...]
    nblk = jnp.floor((cnt + (MOE_ROWS - 1)) * (1.0 / MOE_ROWS))
    nb8 = jnp.broadcast_to(nblk, (8, LANES)).astype(BF16)
    bstart = jnp.dot(nb8, upper_ref[...], preferred_element_type=F32)[0:1]
    bend = bstart + nblk
    pstart = bstart * MOE_ROWS

    rec = route_ref[...]
    rows = rec.shape[0]
    lane = lax.broadcasted_iota(jnp.int32, (rows, LANES), 1).astype(F32)
    col = lambda ln: jnp.sum(jnp.where(lane == ln, rec, 0.0), axis=1, keepdims=True)
    look = lambda e: jnp.sum(jnp.where(lane == e, pstart, 0.0), axis=1, keepdims=True)
    d0 = look(col(L_E0)) + col(L_R0)
    d1 = look(col(L_E1)) + col(L_R1)
    dest_ref[...] = jnp.where(lane == 0, d0, jnp.where(lane == 1, d1, 0.0)).astype(jnp.int32)

    @pl.when(n == 0)
    def _():
        b = lax.broadcasted_iota(jnp.int32, (nb_pad, LANES), 0).astype(F32)
        lane_b = lax.broadcasted_iota(jnp.int32, (nb_pad, LANES), 1)
        done = jnp.where((bend <= b) & (lane_b < N_EXPERTS), 1.0, 0.0)
        e_of_b = jnp.minimum(jnp.sum(done, axis=1, keepdims=True), float(N_EXPERTS - 1))
        total = jnp.max(jnp.where(lane_b < N_EXPERTS, bend, 0.0), axis=1, keepdims=True)
        blk_ref[...] = jnp.where(lane_b == 0, e_of_b, jnp.where(lane_b == 1, total, 0.0)).astype(jnp.int32)
        zbuf[...] = jnp.zeros(zbuf.shape, F32)

    slab = zbuf.shape[0]
    start = jnp.minimum(n * slab, xs_ref.shape[0] - slab)
    cp = pltpu.make_async_copy(zbuf, xs_ref.at[pl.ds(pl.multiple_of(start, MOE_ROWS), slab)], sem)
    cp.start()
    cp.wait()


def _plan(cnt, route, n_rows):
    t = route.shape[0]
    nb = n_rows // MOE_ROWS
    nb_pad = -(-nb // 8) * 8
    slab_blocks = -(-nb // (t // TM))
    upper = jnp.asarray(np.triu(np.ones((LANES, LANES), np.float32), 1), BF16)
    return pl.pallas_call(
        functools.partial(_plan_kernel, nb_pad),
        grid=(t // TM,),
        in_specs=[_const_spec((1, LANES)), pl.BlockSpec((TM, LANES), lambda n: (n, 0)),
                  _const_spec((LANES, LANES))],
        out_specs=[pl.BlockSpec((TM, LANES), lambda n: (n, 0)), _const_spec((nb_pad, LANES)),
                   pl.BlockSpec(memory_space=pl.ANY)],
        out_shape=[jax.ShapeDtypeStruct((t, LANES), jnp.int32),
                   jax.ShapeDtypeStruct((nb_pad, LANES), jnp.int32),
                   jax.ShapeDtypeStruct((n_rows, D_MODEL), F32)],
        scratch_shapes=[pltpu.VMEM((slab_blocks * MOE_ROWS, D_MODEL), F32), pltpu.SemaphoreType.DMA],
        compiler_params=pltpu.CompilerParams(
            dimension_semantics=("arbitrary",), vmem_limit_bytes=VMEM_LIMIT_BYTES),
        name="moe_plan",
    )(cnt, route, upper)


def _dispatch_kernel(dest_ref, h_ref, xs_in, xs_out, sem):
    del xs_in
    rows = h_ref.shape[0]

    def issue(t, carry):
        for k in range(2):
            pltpu.make_async_copy(h_ref.at[pl.ds(t, 1)], xs_out.at[pl.ds(dest_ref[k, t], 1)], sem).start()
        return carry

    lax.fori_loop(0, rows, issue, 0)

    def drain(t, carry):
        for k in range(2):
            pltpu.make_async_copy(h_ref.at[pl.ds(0, 1)], xs_out.at[pl.ds(0, 1)], sem).wait()
        return carry

    lax.fori_loop(0, rows, drain, 0)


def _dispatch(dest_t, h, xs):
    t = h.shape[0]
    return pl.pallas_call(
        _dispatch_kernel,
        grid=(t // TM,),
        in_specs=[pl.BlockSpec((None, 8, TM), lambda n: (n, 0, 0), memory_space=pltpu.SMEM),
                  pl.BlockSpec((TM, D_MODEL), lambda n: (n, 0)),
                  pl.BlockSpec(memory_space=pl.ANY)],
        out_specs=pl.BlockSpec(memory_space=pl.ANY),
        out_shape=jax.ShapeDtypeStruct(xs.shape, xs.dtype),
        input_output_aliases={2: 0},
        scratch_shapes=[pltpu.SemaphoreType.DMA],
        compiler_params=pltpu.CompilerParams(
            dimension_semantics=("arbitrary",), vmem_limit_bytes=VMEM_LIMIT_BYTES),
        name="moe_dispatch",
    )(dest_t, h, xs)


def _expert_kernel(blk_ref, x_ref, wg_ref, wu_ref, wd_ref, y_ref, wg_bf, wu_bf, wd_bf):
    b = pl.program_id(0)
    used = b < blk_ref[1, 0]

    @pl.when(used)
    def _():
        prev = blk_ref[0, jnp.maximum(b - 1, 0)]

        @pl.when((b == 0) | (blk_ref[0, b] != prev))
        def _():
            wg_bf[...] = wg_ref[...].astype(BF16)
            wu_bf[...] = wu_ref[...].astype(BF16)
            wd_bf[...] = wd_ref[...].astype(BF16)

        x = x_ref[...].astype(BF16)
        gate = jnp.dot(x, wg_bf[...], preferred_element_type=F32)
        up = jnp.dot(x, wu_bf[...], preferred_element_type=F32)
        mid = (jax.nn.silu(gate) * up).astype(BF16)
        y_ref[...] = jnp.dot(mid, wd_bf[...], preferred_element_type=F32)

    @pl.when(jnp.logical_not(used))
    def _():
        y_ref[...] = jnp.zeros(y_ref.shape, F32)


def _experts(blk_t, xs, w_g, w_u, w_d):
    n_rows = xs.shape[0]
    nb = n_rows // MOE_ROWS
    last = lambda b, blk: jnp.minimum(b, blk[1, 0] - 1)
    grid_spec = pltpu.PrefetchScalarGridSpec(
        num_scalar_prefetch=1,
        grid=(nb,),
        in_specs=[
            pl.BlockSpec((MOE_ROWS, D_MODEL), lambda b, blk: (last(b, blk), 0)),
            pl.BlockSpec((None, D_MODEL, D_EXPERT), lambda b, blk: (blk[0, last(b, blk)], 0, 0)),
            pl.BlockSpec((None, D_MODEL, D_EXPERT), lambda b, blk: (blk[0, last(b, blk)], 0, 0)),
            pl.BlockSpec((None, D_EXPERT, D_MODEL), lambda b, blk: (blk[0, last(b, blk)], 0, 0)),
        ],
        out_specs=pl.BlockSpec((MOE_ROWS, D_MODEL), lambda b, blk: (b, 0)),
        scratch_shapes=[pltpu.VMEM((D_MODEL, D_EXPERT), BF16), pltpu.VMEM((D_MODEL, D_EXPERT), BF16),
                        pltpu.VMEM((D_EXPERT, D_MODEL), BF16)],
    )
    return pl.pallas_call(
        _expert_kernel,
        grid_spec=grid_spec,
        out_shape=jax.ShapeDtypeStruct((n_rows, D_MODEL), F32),
        compiler_params=pltpu.CompilerParams(
            dimension_semantics=("arbitrary",), vmem_limit_bytes=VMEM_LIMIT_BYTES),
        name="moe_experts",
    )(blk_t, xs, w_g, w_u, w_d)


def _combine_kernel(n_first, dest_ref, x1_ref, route_ref, ys_ref, oa_ref, ob_ref, gbuf, sem):
    n = pl.program_id(0)
    rows = x1_ref.shape[0]

    def issue(t, carry):
        for k in range(2):
            pltpu.make_async_copy(ys_ref.at[pl.ds(dest_ref[k, t], 1)], gbuf.at[k, pl.ds(t, 1)], sem).start()
        return carry

    lax.fori_loop(0, rows, issue, 0)

    def drain(t, carry):
        for k in range(2):
            pltpu.make_async_copy(ys_ref.at[pl.ds(0, 1)], gbuf.at[0, pl.ds(0, 1)], sem).wait()
        return carry

    lax.fori_loop(0, rows, drain, 0)
    rec = route_ref[...]
    lane = lax.broadcasted_iota(jnp.int32, rec.shape, 1)
    w0 = jnp.sum(jnp.where(lane == L_W0, rec, 0.0), axis=1, keepdims=True)
    w1 = jnp.sum(jnp.where(lane == L_W1, rec, 0.0), axis=1, keepdims=True)
    res = x1_ref[...] + (gbuf[0] * w0 + gbuf[1] * w1)

    @pl.when(n < n_first)
    def _():
        oa_ref[...] = res

    @pl.when(n >= n_first)
    def _():
        ob_ref[...] = res


def _combine(dest_t, x1, route, ys, t_first):
    t = x1.shape[0]
    n_first = t_first // TM
    return pl.pallas_call(
        functools.partial(_combine_kernel, n_first),
        grid=(t // TM,),
        in_specs=[pl.BlockSpec((None, 8, TM), lambda n: (n, 0, 0), memory_space=pltpu.SMEM),
                  pl.BlockSpec((TM, D_MODEL), lambda n: (n, 0)),
                  pl.BlockSpec((TM, LANES), lambda n: (n, 0)),
                  pl.BlockSpec(memory_space=pl.ANY)],
        out_specs=[pl.BlockSpec((TM, D_MODEL), lambda n: (jnp.minimum(n, n_first - 1), 0)),
                   pl.BlockSpec((TM, D_MODEL), lambda n: (jnp.maximum(n - n_first, 0), 0))],
        out_shape=[jax.ShapeDtypeStruct((t_first, D_MODEL), F32),
                   jax.ShapeDtypeStruct((t - t_first, D_MODEL), F32)],
        scratch_shapes=[pltpu.VMEM((2, TM, D_MODEL), F32), pltpu.SemaphoreType.DMA],
        compiler_params=pltpu.CompilerParams(
            dimension_semantics=("arbitrary",), vmem_limit_bytes=VMEM_LIMIT_BYTES),
        name="moe_combine",
    )(dest_t, x1, route, ys)


def kernel(x_prompt, x_sample, cache_kv_w128, cache_kv_w512, cache_kv_w2048, state_pool, norm1_w, w_in, q_norm_w, k_norm_w, pool_w, pool_scale, w_branch_pool, w_branch_attn, w_out, norm2_w, w_router_group, b_router_group, w_router_expert, b_router_expert, w_expert_gate, w_expert_up, w_expert_down):
    assert x_prompt.shape[0] == 1 and norm1_w.shape[0] == 1
    layer = 0
    s_p = x_prompt.shape[1]
    n_b, n_t, _ = x_sample.shape
    t_s = n_b * n_t
    t_all = s_p + t_s
    caches = (cache_kv_w128[layer], cache_kv_w512[layer], cache_kv_w2048[layer])

    w_bf = w_in[layer].astype(BF16)
    qw_row = jnp.tile(q_norm_w[layer][:, None, :], (1, N_HEADS, 1)).reshape(-1) * (HEAD_DIM ** -0.5)
    kw_row = jnp.tile(k_norm_w[layer][:, None, :], (1, N_HEADS, 1)).reshape(-1)
    poolw_bf = pool_w[layer].astype(BF16)
    wbp, wba, wo = (w.astype(BF16) for w in (w_branch_pool[layer], w_branch_attn[layer], w_out[layer]))
    wr_hi, wr_lo, br = _router_weights(w_router_group[layer], b_router_group[layer],
                                       w_router_expert[layer], b_router_expert[layer])

    ya_p, gates_p, q_r, k_r, v_r, kvt, ut = _proj_prompt(
        x_prompt[0], norm1_w[layer], w_bf, qw_row, kw_row, poolw_bf, pool_scale[layer])
    yb_p = _attn_prompt(q_r, k_r, v_r)

    xs_t = jnp.transpose(x_sample, (1, 0, 2)).reshape(t_s, D_MODEL)
    state_t = jnp.transpose(state_pool[layer], (1, 0, 2))
    ya_s, gates_s, q_s, k_s, v_s, u_s = _proj_sample(
        xs_t, norm1_w[layer], w_bf, qw_row, kw_row, poolw_bf, pool_scale[layer], state_t, n_b, n_t)
    per_b = lambda a: jnp.transpose(a.reshape(n_t, n_b, N_DIL, N_HEADS, HEAD_DIM), (1, 2, 0, 3, 4))
    q_b, kn_b, vn_b = per_b(q_s), per_b(k_s), per_b(v_s)
    flat = lambda a: a.reshape(n_b, N_DIL, n_t * N_HEADS, HEAD_DIM)
    o_s = _attn_sample(flat(q_b), flat(kn_b), flat(vn_b), *caches)
    yb_s = jnp.transpose(o_s.reshape(n_b, n_t, ATTN_WIDTH), (1, 0, 2)).reshape(t_s, ATTN_WIDTH).astype(BF16)
    news = [jnp.stack([kn_b[:, g], vn_b[:, g]], axis=2) for g in range(N_DIL)]
    kv_s = _cache_roll(caches, news, n_t)

    x1, h, route, cnt = _merge_route(x_prompt[0], xs_t, ya_p, ya_s, yb_p, yb_s, gates_p, gates_s,
                                     wbp, wba, wo, norm2_w[layer], wr_hi, wr_lo, br)

    n_rows = _moe_blocks(t_all) * MOE_ROWS
    dest, blk, xs = _plan(cnt, route, n_rows)
    dest_t = jnp.transpose(dest[:, :8].reshape(t_all // TM, TM, 8), (0, 2, 1))
    blk_t = jnp.transpose(blk[:, :2])
    xs = _dispatch(dest_t, h, xs)
    ys = _experts(blk_t, xs, w_expert_gate[layer], w_expert_up[layer], w_expert_down[layer])
    y_p, y_s = _combine(dest_t, x1, route, ys, s_p)

    y_prompt = y_p[None]
    y_sample = jnp.transpose(y_s.reshape(n_t, n_b, D_MODEL), (1, 0, 2))
    pool_prompt = ut[1:][None, None]
    u_new = jnp.transpose(u_s.reshape(n_t, n_b, POOL_WIDTH), (1, 0, 2))
    pool_sample = jnp.concatenate([state_pool[layer][:, n_t:], u_new], axis=1)[None]
    inv = np.argsort(np.asarray(_class_order()))
    kv_nat = jnp.transpose(kvt[inv], (1, 0, 2)).reshape(-1, 2, N_DIL, N_HEADS, HEAD_DIM)
    outs = [y_prompt, y_sample, pool_prompt, pool_sample]
    for g, (win, _) in enumerate(DILATION_PAIRS):
        keep = min(win, s_p)
        outs.append(kv_nat[kv_nat.shape[0] - keep:, :, g][None, None])
        outs.append(kv_s[g][None])
    return tuple(outs)
```

```python
import functools

import numpy as np
import jax
import jax.numpy as jnp
from jax import lax
from jax.experimental import pallas as pl
from jax.experimental.pallas import tpu as pltpu

F32 = jnp.float32
BF16 = jnp.bfloat16

D_MODEL = 1024
PAST_LEN = 8192
POOL_WIDTH = 512
POOL_WINDOWS = (2, 4, 8, 16)
POOL_GROUP_DIM = 128
POOL_STATE = 15
HEAD_DIM = 64
N_HEADS = 8
DILATION_PAIRS = ((128, 1), (512, 4), (2048, 16))
N_DIL = 3
ATTN_WIDTH = 512
QKV_WIDTH = 1536
ROT_DIM = 16
ROPE_THETA = 500000.0
N_BRANCH = 2
IN_COLS = POOL_WIDTH + 3 * QKV_WIDTH + N_BRANCH * D_MODEL
N_EXPERT_GROUPS = 4
EXPERTS_PER_GROUP = 8
N_EXPERTS = 32
D_EXPERT = 512
EPS = 1e-6

LANES = 128
VMEM_LIMIT_BYTES = 56 * 1024 * 1024

C_U = 0
C_Q = POOL_WIDTH
C_K = C_Q + QKV_WIDTH
C_V = C_K + QKV_WIDTH
C_G = C_V + QKV_WIDTH

N_CLASS = 16
TI = 256
ROWS_PER_CLASS = TI // N_CLASS
NEG = -1e30


def _class_order():
    return [(c // 4) + 4 * (c % 4) for c in range(N_CLASS)]


def _const_spec(shape, single_buffer=False):
    nd = len(shape)
    kw = {}
    if single_buffer:
        kw["pipeline_mode"] = pl.Buffered(1)
    return pl.BlockSpec(shape, lambda *_: (0,) * nd, **kw)


def _rms(x, w):
    ms = jnp.mean(x * x, axis=-1, keepdims=True)
    return x * lax.rsqrt(ms + EPS) * w


def _head_norm_rope(y, wrow, seg, segt, cos, sina, sinb):
    sq = (y * y).astype(BF16)
    ssum = jnp.dot(sq, seg, preferred_element_type=F32)
    r = lax.rsqrt(ssum * (1.0 / HEAD_DIM) + EPS)
    r_hi = r.astype(BF16)
    r_lo = (r - r_hi.astype(F32)).astype(BF16)
    rexp = (jnp.dot(r_hi, segt, preferred_element_type=F32)
            + jnp.dot(r_lo, segt, preferred_element_type=F32))
    yn = y * rexp * wrow
    outs = []
    for j in range(QKV_WIDTH // LANES):
        c = yn[:, j * LANES:(j + 1) * LANES]
        outs.append(c * cos + pltpu.roll(c, LANES - ROT_DIM // 2, 1) * sina
                    + pltpu.roll(c, ROT_DIM // 2, 1) * sinb)
    return jnp.concatenate(outs, axis=1)


def _pool_mix_out(pooled, poolw_ref, pool_scale):
    outs = []
    for g in range(len(POOL_WINDOWS)):
        outs.append(jnp.dot(pooled[g].astype(BF16), poolw_ref[g], preferred_element_type=F32))
    return jnp.concatenate(outs, axis=1) * pool_scale


def _proj_prompt_kernel(x_ref, n1w_ref, w_ref, qw_ref, kw_ref, cos_ref, sina_ref, sinb_ref,
                        perm_ref, seg_ref, segt_ref, poolw_ref, pools_ref,
                        ya_ref, gates_ref, q_ref, k_ref, v_ref, kvt_ref, ut_ref,
                        uext_ref):
    n = pl.program_id(0)
    xn = _rms(x_ref[...], n1w_ref[...]).astype(BF16)

    u = jnp.dot(xn, w_ref[:, C_U:C_U + POOL_WIDTH], preferred_element_type=F32)

    @pl.when(n == 0)
    def _():
        uext_ref[0:16, :] = jnp.zeros((16, POOL_WIDTH), F32)

    uext_ref[16:16 + TI, :] = u
    pos = n * TI + lax.broadcasted_iota(jnp.int32, (TI, 1), 0)
    pooled = []
    for g, w in enumerate(POOL_WINDOWS):
        lanes = slice(g * POOL_GROUP_DIM, (g + 1) * POOL_GROUP_DIM)
        s = uext_ref[16:16 + TI, lanes]
        for m in range(1, w):
            s = s + uext_ref[16 - m:16 - m + TI, lanes]
        cnt = jnp.minimum(pos + 1, w).astype(F32)
        pooled.append(s / cnt - u[:, lanes])
    ya_ref[...] = _pool_mix_out(pooled, poolw_ref, pools_ref[...]).astype(BF16)
    ut_ref[...] = u[TI - 16:, :]
    uext_ref[0:16, :] = uext_ref[TI:TI + 16, :]

    gl = jnp.dot(xn, w_ref[:, C_G:C_G + N_BRANCH * D_MODEL], preferred_element_type=F32)
    gates_ref[...] = jax.nn.sigmoid(gl).astype(BF16)

    xp = jnp.dot(perm_ref[...], xn, preferred_element_type=F32).astype(BF16)
    cos, sina, sinb = cos_ref[...], sina_ref[...], sinb_ref[...]
    seg, segt = seg_ref[...], segt_ref[...]
    shp = (N_CLASS, ROWS_PER_CLASS, QKV_WIDTH)
    yq = jnp.dot(xp, w_ref[:, C_Q:C_Q + QKV_WIDTH], preferred_element_type=F32)
    q_ref[...] = _head_norm_rope(yq, qw_ref[...], seg, segt, cos, sina, sinb).astype(BF16).reshape(shp)
    yk = jnp.dot(xp, w_ref[:, C_K:C_K + QKV_WIDTH], preferred_element_type=F32)
    kr = _head_norm_rope(yk, kw_ref[...], seg, segt, cos, sina, sinb)
    k_ref[...] = kr.astype(BF16).reshape(shp)
    kvt_ref[:, :, 0:QKV_WIDTH] = kr.reshape(shp)
    yv = jnp.dot(xp, w_ref[:, C_V:C_V + QKV_WIDTH], preferred_element_type=F32)
    v_ref[...] = yv.astype(BF16).reshape(shp)
    kvt_ref[:, :, QKV_WIDTH:2 * QKV_WIDTH] = yv.reshape(shp)


def _rope_tables(pos):
    half = ROT_DIM // 2
    inv = ROPE_THETA ** (-(jnp.arange(half, dtype=F32) * 2.0 / ROT_DIM))
    ang = pos.astype(F32)[:, None] * inv[None, :]
    c, s = jnp.cos(ang), jnp.sin(ang)
    n = pos.shape[0]
    ones = jnp.ones((n, HEAD_DIM - ROT_DIM), F32)
    zeros = jnp.zeros((n, HEAD_DIM - ROT_DIM), F32)
    z8 = jnp.zeros((n, half), F32)
    cos64 = jnp.concatenate([c, c, ones], axis=1)
    sina64 = jnp.concatenate([-s, z8, zeros], axis=1)
    sinb64 = jnp.concatenate([z8, s, zeros], axis=1)
    rep = lambda t: jnp.concatenate([t, t], axis=1)
    return rep(cos64), rep(sina64), rep(sinb64)


def _seg_mats():
    h = np.arange(QKV_WIDTH) // HEAD_DIM
    seg = (h[:, None] == np.arange(LANES)[None, :]).astype(np.float32)
    return jnp.asarray(seg, BF16), jnp.asarray(seg.T, BF16)


def _perm_mat():
    cls = _class_order()
    p = np.zeros((TI, TI), np.float32)
    for c in range(N_CLASS):
        for j in range(ROWS_PER_CLASS):
            p[c * ROWS_PER_CLASS + j, N_CLASS * j + cls[c]] = 1.0
    return p


def _proj_prompt(x, n1w, w_bf, qw_row, kw_row, poolw_bf, pool_scale):
    s = x.shape[0]
    nt = s // TI
    cls = np.asarray(_class_order())
    perm = _perm_mat()
    j = np.arange(ROWS_PER_CLASS)
    inner = (N_CLASS * j[None, :] + cls[:, None]).reshape(-1)
    pos_perm = (np.arange(nt)[:, None] * TI + inner[None, :]).reshape(-1)
    cos, sina, sinb = _rope_tables(jnp.asarray(pos_perm, jnp.int32))
    seg, segt = _seg_mats()
    n_i = s // N_CLASS
    tail_i = DILATION_PAIRS[-1][0] // N_CLASS
    tail_blocks = tail_i // ROWS_PER_CLASS
    row = lambda a: a.reshape(1, -1)
    tile = lambda n: (n, 0)
    rblk = lambda n: (0, n, 0)
    tblk = lambda n: (0, jnp.maximum(n - (nt - tail_blocks), 0), 0)
    outs = pl.pallas_call(
        _proj_prompt_kernel,
        grid=(nt,),
        in_specs=[
            pl.BlockSpec((TI, D_MODEL), tile),
            _const_spec((1, D_MODEL)),
            _const_spec((D_MODEL, IN_COLS), single_buffer=True),
            _const_spec((1, QKV_WIDTH)),
            _const_spec((1, QKV_WIDTH)),
            pl.BlockSpec((TI, LANES), tile),
            pl.BlockSpec((TI, LANES), tile),
            pl.BlockSpec((TI, LANES), tile),
            _const_spec((TI, TI)),
            _const_spec((QKV_WIDTH, LANES)),
            _const_spec((LANES, QKV_WIDTH)),
            _const_spec((len(POOL_WINDOWS), POOL_GROUP_DIM, POOL_GROUP_DIM)),
            _const_spec((1, POOL_WIDTH)),
        ],
        out_specs=[
            pl.BlockSpec((TI, POOL_WIDTH), tile),
            pl.BlockSpec((TI, N_BRANCH * D_MODEL), tile),
            pl.BlockSpec((N_CLASS, ROWS_PER_CLASS, QKV_WIDTH), rblk),
            pl.BlockSpec((N_CLASS, ROWS_PER_CLASS, QKV_WIDTH), rblk),
            pl.BlockSpec((N_CLASS, ROWS_PER_CLASS, QKV_WIDTH), rblk),
            pl.BlockSpec((N_CLASS, ROWS_PER_CLASS, 2 * QKV_WIDTH), tblk),
            _const_spec((16, POOL_WIDTH)),
        ],
        out_shape=[
            jax.ShapeDtypeStruct((s, POOL_WIDTH), BF16),
            jax.ShapeDtypeStruct((s, N_BRANCH * D_MODEL), BF16),
            jax.ShapeDtypeStruct((N_CLASS, n_i, QKV_WIDTH), BF16),
            jax.ShapeDtypeStruct((N_CLASS, n_i, QKV_WIDTH), BF16),
            jax.ShapeDtypeStruct((N_CLASS, n_i, QKV_WIDTH), BF16),
            jax.ShapeDtypeStruct((N_CLASS, tail_i, 2 * QKV_WIDTH), F32),
            jax.ShapeDtypeStruct((16, POOL_WIDTH), F32),
        ],
        scratch_shapes=[pltpu.VMEM((16 + TI, POOL_WIDTH), F32)],
        compiler_params=pltpu.CompilerParams(
            dimension_semantics=("arbitrary",), vmem_limit_bytes=VMEM_LIMIT_BYTES),
        name="proj_prompt",
    )(x, row(n1w), w_bf, row(qw_row), row(kw_row), cos, sina, sinb,
      jnp.asarray(perm, BF16), seg, segt, poolw_bf, row(pool_scale))
    return outs


AT_I = 128
N_KEYS_BACK = 128


def _attn_update(q, k, v, bias, m_old, l_old, acc_old):
    nq = q.shape[0]
    lane = lax.broadcasted_iota(jnp.int32, (nq, LANES), 1)
    first = lane < HEAD_DIM
    m_cur, l_cur, pv = [], [], []
    for hp in range(N_HEADS // 2):
        sl = slice(hp * LANES, (hp + 1) * LANES)
        qp, kp, vp = q[:, sl], k[:, sl], v[:, sl]
        m_pair, l_pair, pv_pair = [], [], []
        for e in range(2):
            h = 2 * hp + e
            qm = jnp.where(first if e == 0 else jnp.logical_not(first), qp, jnp.zeros_like(qp))
            s = lax.dot_general(qm, kp, (((1,), (1,)), ((), ())), preferred_element_type=F32) + bias
            mc = jnp.max(s, axis=1, keepdims=True)
            mn = jnp.maximum(m_old[:, h * HEAD_DIM:h * HEAD_DIM + 1], mc)
            p = jnp.exp(s - mn)
            l_pair.append(jnp.sum(p, axis=1, keepdims=True))
            m_pair.append(mc)
            pv_pair.append(jnp.dot(p.astype(BF16), vp, preferred_element_type=F32))
        m_cur.append(jnp.where(first, m_pair[0], m_pair[1]))
        l_cur.append(jnp.where(first, l_pair[0], l_pair[1]))
        pv.append(jnp.where(first, pv_pair[0], pv_pair[1]))
    m_cur = jnp.concatenate(m_cur, axis=1)
    l_cur = jnp.concatenate(l_cur, axis=1)
    pv = jnp.concatenate(pv, axis=1)
    m_new = jnp.maximum(m_old, m_cur)
    alpha = jnp.exp(m_old - m_new)
    return m_new, alpha * l_old + l_cur, alpha * acc_old + pv


def _attn_prompt_kernel(q_ref, kc_ref, kp_ref, vc_ref, vp_ref, b0_ref, b1_ref, b2_ref,
                        r0_ref, r1_ref, r2_ref, o_ref,
                        kcat, vcat, acc, m_s, l_s, ost):
    n = pl.program_id(0)
    g = pl.program_id(1)
    i0 = n * AT_I
    kcat[:, 0:AT_I, :] = kp_ref[...]
    kcat[:, AT_I:2 * AT_I, :] = kc_ref[...]
    vcat[:, 0:AT_I, :] = vp_ref[...]
    vcat[:, AT_I:2 * AT_I, :] = vc_ref[...]

    def key_bias(b_ref, r_ref, base):
        return b_ref[...] + jnp.where(base + r_ref[...] >= 0, 0.0, NEG)

    @pl.when(g == 0)
    def _():
        m_s[...] = jnp.full(m_s.shape, NEG, F32)
        l_s[...] = jnp.zeros(l_s.shape, F32)
        acc[...] = jnp.zeros(acc.shape, F32)

        def body(s, carry):
            qo = pl.multiple_of(16 * s, 16)
            ko = pl.multiple_of(AT_I - 16 + 16 * s, 16)
            cat = lambda f: jnp.concatenate([f(c) for c in range(N_CLASS)], axis=0)
            q = cat(lambda c: q_ref[c, pl.ds(qo, 16), :])
            k = cat(lambda c: kcat[c, pl.ds(ko, 32), :])
            v = cat(lambda c: vcat[c, pl.ds(ko, 32), :])
            mo = cat(lambda c: m_s[c, pl.ds(qo, 16), :])
            lo = cat(lambda c: l_s[c, pl.ds(qo, 16), :])
            ao = cat(lambda c: acc[c, pl.ds(qo, 16), :])
            bias = key_bias(b0_ref, r0_ref, i0 + 16 * s)
            mn, ln, an = _attn_update(q, k, v, bias, mo, lo, ao)
            for c in range(N_CLASS):
                m_s[c, pl.ds(qo, 16), :] = mn[16 * c:16 * c + 16]
                l_s[c, pl.ds(qo, 16), :] = ln[16 * c:16 * c + 16]
                acc[c, pl.ds(qo, 16), :] = an[16 * c:16 * c + 16]
            return carry

        lax.fori_loop(0, AT_I // 16, body, 0)

    @pl.when(g == 1)
    def _():
        def body(idx, carry):
            r4 = idx // 4
            s = idx % 4
            qo = pl.multiple_of(32 * s, 32)
            ko = pl.multiple_of(AT_I - 32 + 32 * s, 32)
            cat = lambda f: jnp.concatenate([f(4 * r4 + a) for a in range(4)], axis=0)
            q = cat(lambda c: q_ref[c, pl.ds(qo, 32), :])
            k = cat(lambda c: kcat[c, pl.ds(ko, 64), :])
            v = cat(lambda c: vcat[c, pl.ds(ko, 64), :])
            mo = cat(lambda c: m_s[c, pl.ds(qo, 32), :])
            lo = cat(lambda c: l_s[c, pl.ds(qo, 32), :])
            ao = cat(lambda c: acc[c, pl.ds(qo, 32), :])
            bias = key_bias(b1_ref, r1_ref, i0 + 32 * s)
            mn, ln, an = _attn_update(q, k, v, bias, mo, lo, ao)
            for a in range(4):
                c = 4 * r4 + a
                m_s[c, pl.ds(qo, 32), :] = mn[32 * a:32 * a + 32]
                l_s[c, pl.ds(qo, 32), :] = ln[32 * a:32 * a + 32]
                acc[c, pl.ds(qo, 32), :] = an[32 * a:32 * a + 32]
            return carry

        lax.fori_loop(0, 16, body, 0)

    @pl.when(g == 2)
    def _():
        bias = key_bias(b2_ref, r2_ref, i0)

        def body(c, carry):
            mn, ln, an = _attn_update(q_ref[c], kcat[c], vcat[c], bias, m_s[c], l_s[c], acc[c])
            m_s[c] = mn
            l_s[c] = ln
            acc[c] = an
            return carry

        lax.fori_loop(0, N_CLASS, body, 0)

        cls = _class_order()
        for c in range(N_CLASS):
            o = acc[c] / l_s[c]
            for jc in range(ATTN_WIDTH // LANES):
                ost[jc, pl.ds(cls[c], AT_I, stride=N_CLASS), :] = o[:, jc * LANES:(jc + 1) * LANES]
        o_ref[...] = jnp.concatenate([ost[jc] for jc in range(ATTN_WIDTH // LANES)], axis=1).astype(BF16)


def _attn_bias_tables():
    cls = np.asarray(_class_order())

    def bias(diff):
        return np.where((diff >= 0) & (diff <= N_KEYS_BACK), 0.0, NEG).astype(np.float32)

    c, r = np.divmod(np.arange(256), 16)
    c2, r2 = np.divmod(np.arange(512), 32)
    d0 = 16 * (r[:, None] - r2[None, :] + 16) + cls[c][:, None] - cls[c2][None, :]
    rel0 = (r2 - 16).astype(np.int32)[None, :]
    a, r = np.divmod(np.arange(128), 32)
    a2, r2 = np.divmod(np.arange(256), 64)
    d1 = 4 * (r[:, None] - r2[None, :] + 32) + a[:, None] - a2[None, :]
    rel1 = (r2 - 32).astype(np.int32)[None, :]
    r = np.arange(128)
    r2 = np.arange(256)
    d2 = r[:, None] - r2[None, :] + 128
    rel2 = (r2 - 128).astype(np.int32)[None, :]
    return (jnp.asarray(bias(d0)), jnp.asarray(bias(d1)), jnp.asarray(bias(d2)),
            jnp.asarray(rel0), jnp.asarray(rel1), jnp.asarray(rel2))


def _attn_prompt(q_r, k_r, v_r):
    n_i = q_r.shape[1]
    s = n_i * N_CLASS
    nsteps = n_i // AT_I
    b0, b1, b2, r0, r1, r2 = _attn_bias_tables()
    blk = (N_CLASS, AT_I, ATTN_WIDTH)
    cur = lambda n, g: (0, n, g)
    prev = lambda n, g: (0, jnp.maximum(n - 1, 0), g)
    return pl.pallas_call(
        _attn_prompt_kernel,
        grid=(nsteps, N_DIL),
        in_specs=[
            pl.BlockSpec(blk, cur), pl.BlockSpec(blk, cur), pl.BlockSpec(blk, prev),
            pl.BlockSpec(blk, cur), pl.BlockSpec(blk, prev),
            _const_spec(b0.shape), _const_spec(b1.shape), _const_spec(b2.shape),
            _const_spec(r0.shape), _const_spec(r1.shape), _const_spec(r2.shape),
        ],
        out_specs=pl.BlockSpec((AT_I * N_CLASS, ATTN_WIDTH), lambda n, g: (n, 0)),
        out_shape=jax.ShapeDtypeStruct((s, ATTN_WIDTH), BF16),
        scratch_shapes=[
            pltpu.VMEM((N_CLASS, 2 * AT_I, ATTN_WIDTH), BF16),
            pltpu.VMEM((N_CLASS, 2 * AT_I, ATTN_WIDTH), BF16),
            pltpu.VMEM((N_CLASS, AT_I, ATTN_WIDTH), F32),
            pltpu.VMEM((N_CLASS, AT_I, ATTN_WIDTH), F32),
            pltpu.VMEM((N_CLASS, AT_I, ATTN_WIDTH), F32),
            pltpu.VMEM((ATTN_WIDTH // LANES, AT_I * N_CLASS, LANES), F32),
        ],
        compiler_params=pltpu.CompilerParams(
            dimension_semantics=("arbitrary", "arbitrary"), vmem_limit_bytes=VMEM_LIMIT_BYTES),
        name="attn_prompt",
    )(q_r, k_r, k_r, v_r, v_r, b0, b1, b2, r0, r1, r2)


def _proj_sample_kernel(n_b, n_t, x_ref, n1w_ref, w_ref, qw_ref, kw_ref, cos_ref, sina_ref, sinb_ref,
                        seg_ref, segt_ref, poolw_ref, pools_ref, state_ref,
                        ya_ref, gates_ref, q_ref, k_ref, v_ref, u_ref):
    xn = _rms(x_ref[...], n1w_ref[...]).astype(BF16)
    u = jnp.dot(xn, w_ref[:, C_U:C_U + POOL_WIDTH], preferred_element_type=F32)
    u_ref[...] = u

    def ext(j, lanes):
        if j < POOL_STATE:
            return state_ref[j, :, lanes]
        return u[(j - POOL_STATE) * n_b:(j - POOL_STATE + 1) * n_b, lanes]

    pooled = []
    for g, w in enumerate(POOL_WINDOWS):
        lanes = slice(g * POOL_GROUP_DIM, (g + 1) * POOL_GROUP_DIM)
        rows = []
        for i in range(n_t):
            s = ext(POOL_STATE + i, lanes)
            for m in range(1, w):
                s = s + ext(POOL_STATE + i - m, lanes)
            cnt = float(min(PAST_LEN + i + 1, w))
            rows.append(s / cnt - ext(POOL_STATE + i, lanes))
        pooled.append(jnp.concatenate(rows, axis=0))
    ya_ref[...] = _pool_mix_out(pooled, poolw_ref, pools_ref[...]).astype(BF16)

    gl = jnp.dot(xn, w_ref[:, C_G:C_G + N_BRANCH * D_MODEL], preferred_element_type=F32)
    gates_ref[...] = jax.nn.sigmoid(gl).astype(BF16)

    cos, sina, sinb = cos_ref[...], sina_ref[...], sinb_ref[...]
    seg, segt = seg_ref[...], segt_ref[...]
    yq = jnp.dot(xn, w_ref[:, C_Q:C_Q + QKV_WIDTH], preferred_element_type=F32)
    q_ref[...] = _head_norm_rope(yq, qw_ref[...], seg, segt, cos, sina, sinb)
    yk = jnp.dot(xn, w_ref[:, C_K:C_K + QKV_WIDTH], preferred_element_type=F32)
    k_ref[...] = _head_norm_rope(yk, kw_ref[...], seg, segt, cos, sina, sinb)
    v_ref[...] = jnp.dot(xn, w_ref[:, C_V:C_V + QKV_WIDTH], preferred_element_type=F32)


def _proj_sample(x, n1w, w_bf, qw_row, kw_row, poolw_bf, pool_scale, state_t, n_b, n_t):
    rows = n_b * n_t
    pos = PAST_LEN + jnp.arange(rows, dtype=jnp.int32) // n_b
    cos, sina, sinb = _rope_tables(pos)
    seg, segt = _seg_mats()
    row = lambda a: a.reshape(1, -1)
    args = (x, row(n1w), w_bf, row(qw_row), row(kw_row), cos, sina, sinb, seg, segt,
            poolw_bf, row(pool_scale), state_t)
    f32_qkv = jax.ShapeDtypeStruct((rows, QKV_WIDTH), F32)
    out_shape = [
        jax.ShapeDtypeStruct((rows, POOL_WIDTH), BF16),
        jax.ShapeDtypeStruct((rows, N_BRANCH * D_MODEL), BF16),
        f32_qkv, f32_qkv, f32_qkv,
        jax.ShapeDtypeStruct((rows, POOL_WIDTH), F32),
    ]
    return pl.pallas_call(
        functools.partial(_proj_sample_kernel, n_b, n_t),
        grid=(1,),
        in_specs=[_const_spec(a.shape) for a in args],
        out_specs=[_const_spec(o.shape) for o in out_shape],
        out_shape=out_shape,
        compiler_params=pltpu.CompilerParams(
            dimension_semantics=("arbitrary",), vmem_limit_bytes=VMEM_LIMIT_BYTES),
        name="proj_sample",
    )(*args)


def _attn_sample_kernel(q_ref, kn_ref, vn_ref, c0_ref, c1_ref, c2_ref, b0_ref, b1_ref, b2_ref, o_ref):
    caches = (c0_ref, c1_ref, c2_ref)
    biases = (b0_ref, b1_ref, b2_ref)
    s_all, v_all = [], []
    for g in range(N_DIL):
        c = caches[g]
        if g == 2:
            kc = c[:, :, 0].reshape(-1, HEAD_DIM)
            vc = c[:, :, 1].reshape(-1, HEAD_DIM)
        else:
            kc = c[:, 0].reshape(-1, HEAD_DIM)
            vc = c[:, 1].reshape(-1, HEAD_DIM)
        k = jnp.concatenate([kc, kn_ref[g]], axis=0).astype(BF16)
        v = jnp.concatenate([vc, vn_ref[g]], axis=0).astype(BF16)
        q = q_ref[g].astype(BF16)
        s = lax.dot_general(q, k, (((1,), (1,)), ((), ())), preferred_element_type=F32)
        s_all.append(s + biases[g][...])
        v_all.append(v)
    m = functools.reduce(jnp.maximum, [jnp.max(s, axis=1, keepdims=True) for s in s_all])
    l = jnp.zeros_like(m)
    o = jnp.zeros((q_ref.shape[1], HEAD_DIM), F32)
    for s, v in zip(s_all, v_all):
        p = jnp.exp(s - m)
        l = l + jnp.sum(p, axis=1, keepdims=True)
        o = o + jnp.dot(p.astype(BF16), v, preferred_element_type=F32)
    o_ref[...] = o / l


def _attn_sample_bias(n_t):
    nq = n_t * N_HEADS
    qi, qh = np.divmod(np.arange(nq), N_HEADS)

    def fin(valid):
        return jnp.asarray(np.where(valid, 0.0, NEG).astype(np.float32))

    out = []
    for g, (win, dil) in enumerate(DILATION_PAIRS):
        length = min(win, PAST_LEN)
        if g == 2:
            cols = (length // N_CLASS) * n_t * N_HEADS
            rest, kh = np.divmod(np.arange(cols), N_HEADS)
            m_idx, cls = np.divmod(rest, n_t)
            row = m_idx * N_CLASS + cls
        else:
            cols = length * N_HEADS
            row, kh = np.divmod(np.arange(cols), N_HEADS)
        delta = length + qi[:, None] - row[None, :]
        valid_c = (qh[:, None] == kh[None, :]) & (delta % dil == 0) & (delta // dil >= 1) & (delta // dil <= win // dil)
        nj, nh = np.divmod(np.arange(nq), N_HEADS)
        dn = qi[:, None] - nj[None, :]
        valid_n = (qh[:, None] == nh[None, :]) & (dn >= 0) & (dn % dil == 0) & (dn // dil <= win // dil)
        out.append(fin(np.concatenate([valid_c, valid_n], axis=1)))
    return out


def _attn_sample(q_b, kn_b, vn_b, c128, c512, c2048):
    n_b, _, nq, _ = q_b.shape
    n_t = nq // N_HEADS
    b0, b1, b2 = _attn_sample_bias(n_t)
    c2v = c2048.reshape(n_b, c2048.shape[1] // N_CLASS, N_CLASS, 2, N_HEADS, HEAD_DIM)
    per_b = lambda shape: pl.BlockSpec((None,) + shape, lambda b: (b,) + (0,) * len(shape))
    return pl.pallas_call(
        _attn_sample_kernel,
        grid=(n_b,),
        in_specs=[
            per_b((N_DIL, nq, HEAD_DIM)), per_b((N_DIL, nq, HEAD_DIM)), per_b((N_DIL, nq, HEAD_DIM)),
            per_b(c128.shape[1:]), per_b(c512.shape[1:]),
            per_b((c2v.shape[1], n_t, 2, N_HEADS, HEAD_DIM)),
            _const_spec(b0.shape), _const_spec(b1.shape), _const_spec(b2.shape),
        ],
        out_specs=per_b((nq, HEAD_DIM)),
        out_shape=jax.ShapeDtypeStruct((n_b, nq, HEAD_DIM), F32),
        compiler_params=pltpu.CompilerParams(
            dimension_semantics=("arbitrary",), vmem_limit_bytes=VMEM_LIMIT_BYTES),
        name="attn_sample",
    )(q_b, kn_b, vn_b, c128, c512, c2v, b0, b1, b2)


CACHE_DMA_CHUNKS = 4


def _cache_roll_kernel(n_t, c0, c1, c2, n0, n1, n2, o0, o1, o2, sem):
    copies = []
    for c, nw, o in ((c0, n0, o0), (c1, n1, o1), (c2, n2, o2)):
        n_b, length = c.shape[0], c.shape[1]
        step = n_b // CACHE_DMA_CHUNKS
        for ch in range(CACHE_DMA_CHUNKS):
            bs = pl.ds(ch * step, step)
            copies.append(pltpu.make_async_copy(
                c.at[bs, pl.ds(n_t, length - n_t)], o.at[bs, pl.ds(0, length - n_t)], sem))
            copies.append(pltpu.make_async_copy(nw.at[bs], o.at[bs, pl.ds(length - n_t, n_t)], sem))
    for cp in copies:
        cp.start()
    for cp in copies:
        cp.wait()


def _cache_roll(caches, news, n_t):
    anyspec = pl.BlockSpec(memory_space=pl.ANY)
    return pl.pallas_call(
        functools.partial(_cache_roll_kernel, n_t),
        in_specs=[anyspec] * 6,
        out_specs=[anyspec] * 3,
        out_shape=[jax.ShapeDtypeStruct(c.shape, c.dtype) for c in caches],
        scratch_shapes=[pltpu.SemaphoreType.DMA],
        name="cache_roll",
    )(*caches, *news)


HEADS_PER_STEP = 4


def _attn_roll_kernel(n_t, q_ref, kn_ref, vn_ref, nt_ref, c0_ref, c1_ref, c2_ref,
                      b0_ref, b1_ref, b2_ref, bn_ref, o_ref, o0_ref, o1_ref, o2_ref):
    caches = (c0_ref, c1_ref, c2_ref)
    outs = (o0_ref, o1_ref, o2_ref)
    biases = (b0_ref, b1_ref, b2_ref)
    nt_contract = (((1,), (1,)), ((), ()))
    for hh in range(HEADS_PER_STEP):
        scores, values = [], []
        for g in range(N_DIL):
            q = q_ref[hh, g].astype(BF16)
            kc = caches[g][0, hh].astype(BF16)
            scores.append(jnp.dot(q, kc, preferred_element_type=F32) + biases[g][...])
            kn = kn_ref[hh, g].astype(BF16)
            scores.append(lax.dot_general(q, kn, nt_contract, preferred_element_type=F32) + bn_ref[g])
            values.append(caches[g][1, hh].astype(BF16))
            values.append(vn_ref[hh, g].astype(BF16))
        m = functools.reduce(jnp.maximum, [jnp.max(s, axis=1, keepdims=True) for s in scores])
        l = jnp.zeros_like(m)
        o = jnp.zeros((n_t, HEAD_DIM), F32)
        for idx, (s, v) in enumerate(zip(scores, values)):
            p = jnp.exp(s - m)
            l = l + jnp.sum(p, axis=1, keepdims=True)
            pb = p.astype(BF16)
            if idx % 2 == 0:
                o = o + lax.dot_general(pb, v, nt_contract, preferred_element_type=F32)
            else:
                o = o + jnp.dot(pb, v, preferred_element_type=F32)
        o_ref[hh] = o / l

        for g in range(N_DIL):
            length = caches[g].shape[-1]
            lane = lax.broadcasted_iota(jnp.int32, (HEAD_DIM, LANES), 1)
            for kv in range(2):
                rolled = pltpu.roll(caches[g][kv, hh], length - n_t, 1)
                new = nt_ref[g, kv, hh]
                if length > LANES:
                    outs[g][kv, hh, :, 0:length - LANES] = rolled[:, 0:length - LANES]
                outs[g][kv, hh, :, length - LANES:length] = jnp.where(
                    lane >= LANES - n_t, new, rolled[:, length - LANES:length])


def _attn_roll_bias(n_t):
    i = np.arange(n_t)
    bias, bias_new = [], []
    for win, dil in DILATION_PAIRS:
        length = min(win, PAST_LEN)
        delta = length + i[:, None] - np.arange(length)[None, :]
        ok = (delta % dil == 0) & (delta // dil >= 1) & (delta // dil <= win // dil)
        bias.append(jnp.asarray(np.where(ok, 0.0, NEG).astype(np.float32)))
        dn = i[:, None] - i[None, :]
        okn = (dn >= 0) & (dn % dil == 0) & (dn // dil <= win // dil)
        bias_new.append(np.where(okn, 0.0, NEG).astype(np.float32))
    return bias, jnp.asarray(np.stack(bias_new))


def _attn_roll(q_h, kn_h, vn_h, new_t, caches_t):
    n_b, _, _, n_t, _ = q_h.shape
    bias, bias_new = _attn_roll_bias(n_t)
    hs = HEADS_PER_STEP
    qspec = pl.BlockSpec((None, hs, N_DIL, n_t, HEAD_DIM), lambda b, h: (b, h, 0, 0, 0))
    cspec = lambda c: pl.BlockSpec((None, 2, hs, HEAD_DIM, c.shape[-1]), lambda b, h: (b, 0, h, 0, 0))
    return pl.pallas_call(
        functools.partial(_attn_roll_kernel, n_t),
        grid=(n_b, N_HEADS // hs),
        in_specs=[qspec, qspec, qspec,
                  pl.BlockSpec((None, N_DIL, 2, hs, HEAD_DIM, LANES), lambda b, h: (b, 0, 0, h, 0, 0))]
                 + [cspec(c) for c in caches_t]
                 + [_const_spec(b.shape) for b in bias] + [_const_spec(bias_new.shape)],
        out_specs=[pl.BlockSpec((None, hs, n_t, HEAD_DIM), lambda b, h: (b, h, 0, 0))]
                  + [cspec(c) for c in caches_t],
        out_shape=[jax.ShapeDtypeStruct((n_b, N_HEADS, n_t, HEAD_DIM), F32)]
                  + [jax.ShapeDtypeStruct(c.shape, c.dtype) for c in caches_t],
        compiler_params=pltpu.CompilerParams(
            dimension_semantics=("arbitrary", "arbitrary"), vmem_limit_bytes=VMEM_LIMIT_BYTES),
        name="attn_roll",
    )(q_h, kn_h, vn_h, new_t, *caches_t, *bias, bias_new)


TM = 512
MOE_ROWS = 256
L_E0, L_E1, L_W0, L_W1, L_R0, L_R1 = 0, 1, 2, 3, 4, 5
L_GROUP = N_EXPERTS


def _lane_min_index(mask, lane):
    return jnp.min(jnp.where(mask, lane, float(LANES)), axis=1, keepdims=True)


def _merge_route_kernel(n_first, xa_ref, xb_ref, yaa_ref, yab_ref, yba_ref, ybb_ref, ga_ref, gb_ref,
                        wbp_ref, wba_ref, wo_ref, n2w_ref, wrh_ref, wrl_ref, br_ref, ltri_ref,
                        x1_ref, h_ref, route_ref, cnt_ref):
    n = pl.program_id(0)
    first = n < n_first
    pick = lambda a, b: jnp.where(first, a[...], b[...])
    pa = jnp.dot(pick(yaa_ref, yab_ref), wbp_ref[...], preferred_element_type=F32)
    pb = jnp.dot(pick(yba_ref, ybb_ref), wba_ref[...], preferred_element_type=F32)
    gts = pick(ga_ref, gb_ref)
    merged = gts[:, :D_MODEL].astype(F32) * pa + gts[:, D_MODEL:].astype(F32) * pb
    x1 = pick(xa_ref, xb_ref) + jnp.dot(merged.astype(BF16), wo_ref[...], preferred_element_type=F32)
    x1_ref[...] = x1
    h = _rms(x1, n2w_ref[...])
    h_ref[...] = h

    h_hi = h.astype(BF16)
    h_lo = (h - h_hi.astype(F32)).astype(BF16)
    logits = (jnp.dot(h_hi, wrh_ref[...], preferred_element_type=F32)
              + jnp.dot(h_hi, wrl_ref[...], preferred_element_type=F32)
              + jnp.dot(h_lo, wrh_ref[...], preferred_element_type=F32)) + br_ref[...]
    rows = logits.shape[0]
    lane = lax.broadcasted_iota(jnp.int32, (rows, LANES), 1).astype(F32)
    is_group = (lane >= L_GROUP) & (lane < L_GROUP + N_EXPERT_GROUPS)
    gl = jnp.where(is_group, logits, NEG)
    gmax = jnp.max(gl, axis=1, keepdims=True)
    g_w = 1.0 / jnp.sum(jnp.exp(gl - gmax), axis=1, keepdims=True)
    g_sel = _lane_min_index(gl == gmax, lane) - L_GROUP
    in_group = (lane >= g_sel * EXPERTS_PER_GROUP) & (lane < (g_sel + 1.0) * EXPERTS_PER_GROUP)
    el = jnp.where(in_group, logits, NEG)
    v0 = jnp.max(el, axis=1, keepdims=True)
    e0 = _lane_min_index(el == v0, lane)
    el2 = jnp.where(lane == e0, NEG, el)
    v1 = jnp.max(el2, axis=1, keepdims=True)
    e1 = _lane_min_index(el2 == v1, lane)
    t = jnp.exp(v1 - v0)
    w0 = g_w / (1.0 + t)
    w1 = g_w * t / (1.0 + t)

    hot0 = lane == e0
    hot1 = lane == e1
    onehot = jnp.where(hot0 | hot1, 1.0, 0.0)

    @pl.when(n == 0)
    def _():
        cnt_ref[...] = jnp.zeros(cnt_ref.shape, F32)

    before = jnp.dot(ltri_ref[...], onehot.astype(BF16), preferred_element_type=F32) + cnt_ref[...]
    r0 = jnp.sum(jnp.where(hot0, before, 0.0), axis=1, keepdims=True)
    r1 = jnp.sum(jnp.where(hot1, before, 0.0), axis=1, keepdims=True)
    cnt_ref[...] = cnt_ref[...] + jnp.sum(onehot, axis=0, keepdims=True)

    rec = jnp.zeros((rows, LANES), F32)
    for ln, val in ((L_E0, e0), (L_E1, e1), (L_W0, w0), (L_W1, w1), (L_R0, r0), (L_R1, r1)):
        rec = jnp.where(lane == ln, val, rec)
    route_ref[...] = rec


def _router_weights(w_rg, b_rg, w_re, b_re):
    w = jnp.zeros((D_MODEL, LANES), F32)
    w = w.at[:, :N_EXPERTS].set(w_re).at[:, L_GROUP:L_GROUP + N_EXPERT_GROUPS].set(w_rg)
    b = jnp.zeros((1, LANES), F32)
    b = b.at[0, :N_EXPERTS].set(b_re).at[0, L_GROUP:L_GROUP + N_EXPERT_GROUPS].set(b_rg)
    w_hi = w.astype(BF16)
    w_lo = (w - w_hi.astype(F32)).astype(BF16)
    return w_hi, w_lo, b


def _merge_route(xa, xb, yaa, yab, yba, ybb, ga, gb, wbp, wba, wo, n2w, wr_hi, wr_lo, br):
    n_first = xa.shape[0] // TM
    t_all = xa.shape[0] + xb.shape[0]
    ltri = jnp.asarray(np.tril(np.ones((TM, TM), np.float32), -1), BF16)
    ta = lambda w: pl.BlockSpec((TM, w), lambda n: (jnp.minimum(n, n_first - 1), 0))
    tb = lambda w: pl.BlockSpec((TM, w), lambda n: (jnp.maximum(n - n_first, 0), 0))
    tile = lambda w: pl.BlockSpec((TM, w), lambda n: (n, 0))
    return pl.pallas_call(
        functools.partial(_merge_route_kernel, n_first),
        grid=(t_all // TM,),
        in_specs=[
            ta(D_MODEL), tb(D_MODEL), ta(POOL_WIDTH), tb(POOL_WIDTH), ta(ATTN_WIDTH), tb(ATTN_WIDTH),
            ta(N_BRANCH * D_MODEL), tb(N_BRANCH * D_MODEL),
            _const_spec(wbp.shape), _const_spec(wba.shape), _const_spec(wo.shape),
            _const_spec((1, D_MODEL)),
            _const_spec(wr_hi.shape), _const_spec(wr_lo.shape), _const_spec(br.shape),
            _const_spec(ltri.shape),
        ],
        out_specs=[tile(D_MODEL), tile(D_MODEL), tile(LANES), _const_spec((1, LANES))],
        out_shape=[
            jax.ShapeDtypeStruct((t_all, D_MODEL), F32),
            jax.ShapeDtypeStruct((t_all, D_MODEL), F32),
            jax.ShapeDtypeStruct((t_all, LANES), F32),
            jax.ShapeDtypeStruct((1, LANES), F32),
        ],
        compiler_params=pltpu.CompilerParams(
            dimension_semantics=("arbitrary",), vmem_limit_bytes=VMEM_LIMIT_BYTES),
        name="merge_route",
    )(xa, xb, yaa, yab, yba, ybb, ga, gb, wbp, wba, wo, n2w.reshape(1, -1), wr_hi, wr_lo, br, ltri)


def _moe_blocks(n_tokens):
    return -(-(2 * n_tokens + N_EXPERTS * (MOE_ROWS - 1)) // MOE_ROWS)


def _plan_kernel(nb_pad, cnt_ref, route_ref, upper_ref, dest_ref, blk_ref, xs_ref, zbuf, sem):
    n = pl.program_id(0)
    cnt = cnt_ref[...]
    nblk = jnp.floor((cnt + (MOE_ROWS - 1)) * (1.0 / MOE_ROWS))
    nb8 = jnp.broadcast_to(nblk, (8, LANES)).astype(BF16)
    bstart = jnp.dot(nb8, upper_ref[...], preferred_element_type=F32)[0:1]
    bend = bstart + nblk
    pstart = bstart * MOE_ROWS

    rec = route_ref[...]
    rows = rec.shape[0]
    lane = lax.broadcasted_iota(jnp.int32, (rows, LANES), 1).astype(F32)
    col = lambda ln: jnp.sum(jnp.where(lane == ln, rec, 0.0), axis=1, keepdims=True)
    look = lambda e: jnp.sum(jnp.where(lane == e, pstart, 0.0), axis=1, keepdims=True)
    d0 = look(col(L_E0)) + col(L_R0)
    d1 = look(col(L_E1)) + col(L_R1)
    dest_ref[...] = jnp.where(lane == 0, d0, jnp.where(lane == 1, d1, 0.0)).astype(jnp.int32)

    @pl.when(n == 0)
    def _():
        b = lax.broadcasted_iota(jnp.int32, (nb_pad, LANES), 0).astype(F32)
        lane_b = lax.broadcasted_iota(jnp.int32, (nb_pad, LANES), 1)
        done = jnp.where((bend <= b) & (lane_b < N_EXPERTS), 1.0, 0.0)
        e_of_b = jnp.minimum(jnp.sum(done, axis=1, keepdims=True), float(N_EXPERTS - 1))
        total = jnp.max(jnp.where(lane_b < N_EXPERTS, bend, 0.0), axis=1, keepdims=True)
        blk_ref[...] = jnp.where(lane_b == 0, e_of_b, jnp.where(lane_b == 1, total, 0.0)).astype(jnp.int32)
        zbuf[...] = jnp.zeros(zbuf.shape, F32)

    slab = zbuf.shape[0]
    start = jnp.minimum(n * slab, xs_ref.shape[0] - slab)
    cp = pltpu.make_async_copy(zbuf, xs_ref.at[pl.ds(pl.multiple_of(start, MOE_ROWS), slab)], sem)
    cp.start()
    cp.wait()


def _plan(cnt, route, n_rows):
    t = route.shape[0]
    nb = n_rows // MOE_ROWS
    nb_pad = -(-nb // 8) * 8
    slab_blocks = -(-nb // (t // TM))
    upper = jnp.asarray(np.triu(np.ones((LANES, LANES), np.float32), 1), BF16)
    return pl.pallas_call(
        functools.partial(_plan_kernel, nb_pad),
        grid=(t // TM,),
        in_specs=[_const_spec((1, LANES)), pl.BlockSpec((TM, LANES), lambda n: (n, 0)),
                  _const_spec((LANES, LANES))],
        out_specs=[pl.BlockSpec((TM, LANES), lambda n: (n, 0)), _const_spec((nb_pad, LANES)),
                   pl.BlockSpec(memory_space=pl.ANY)],
        out_shape=[jax.ShapeDtypeStruct((t, LANES), jnp.int32),
                   jax.ShapeDtypeStruct((nb_pad, LANES), jnp.int32),
                   jax.ShapeDtypeStruct((n_rows, D_MODEL), F32)],
        scratch_shapes=[pltpu.VMEM((slab_blocks * MOE_ROWS, D_MODEL), F32), pltpu.SemaphoreType.DMA],
        compiler_params=pltpu.CompilerParams(
            dimension_semantics=("arbitrary",), vmem_limit_bytes=VMEM_LIMIT_BYTES),
        name="moe_plan",
    )(cnt, route, upper)


def _dispatch_kernel(dest_ref, h_ref, xs_in, xs_out, sem):
    del xs_in
    rows = h_ref.shape[0]

    def issue(t, carry):
        for k in range(2):
            pltpu.make_async_copy(h_ref.at[pl.ds(t, 1)], xs_out.at[pl.ds(dest_ref[k, t], 1)], sem).start()
        return carry

    lax.fori_loop(0, rows, issue, 0, unroll=8)
    for k in range(2):
        pltpu.make_async_copy(h_ref, xs_out.at[pl.ds(0, rows)], sem).wait()


def _dispatch(dest_t, h, xs):
    t = h.shape[0]
    return pl.pallas_call(
        _dispatch_kernel,
        grid=(t // TM,),
        in_specs=[pl.BlockSpec((None, 8, TM), lambda n: (n, 0, 0), memory_space=pltpu.SMEM),
                  pl.BlockSpec((TM, D_MODEL), lambda n: (n, 0)),
                  pl.BlockSpec(memory_space=pl.ANY)],
        out_specs=pl.BlockSpec(memory_space=pl.ANY),
        out_shape=jax.ShapeDtypeStruct(xs.shape, xs.dtype),
        input_output_aliases={2: 0},
        scratch_shapes=[pltpu.SemaphoreType.DMA],
        compiler_params=pltpu.CompilerParams(
            dimension_semantics=("arbitrary",), vmem_limit_bytes=VMEM_LIMIT_BYTES),
        name="moe_dispatch",
    )(dest_t, h, xs)


def _expert_kernel(blk_ref, x_ref, wg_ref, wu_ref, wd_ref, y_ref, wg_bf, wu_bf, wd_bf):
    b = pl.program_id(0)
    used = b < blk_ref[1, 0]

    @pl.when(used)
    def _():
        prev = blk_ref[0, jnp.maximum(b - 1, 0)]

        @pl.when((b == 0) | (blk_ref[0, b] != prev))
        def _():
            wg_bf[...] = wg_ref[...].astype(BF16)
            wu_bf[...] = wu_ref[...].astype(BF16)
            wd_bf[...] = wd_ref[...].astype(BF16)

        x = x_ref[...].astype(BF16)
        gate = jnp.dot(x, wg_bf[...], preferred_element_type=F32)
        up = jnp.dot(x, wu_bf[...], preferred_element_type=F32)
        mid = (jax.nn.silu(gate) * up).astype(BF16)
        y_ref[...] = jnp.dot(mid, wd_bf[...], preferred_element_type=F32)

    @pl.when(jnp.logical_not(used))
    def _():
        y_ref[...] = jnp.zeros(y_ref.shape, F32)


def _experts(blk_t, xs, w_g, w_u, w_d):
    n_rows = xs.shape[0]
    nb = n_rows // MOE_ROWS
    last = lambda b, blk: jnp.minimum(b, blk[1, 0] - 1)
    grid_spec = pltpu.PrefetchScalarGridSpec(
        num_scalar_prefetch=1,
        grid=(nb,),
        in_specs=[
            pl.BlockSpec((MOE_ROWS, D_MODEL), lambda b, blk: (last(b, blk), 0)),
            pl.BlockSpec((None, D_MODEL, D_EXPERT), lambda b, blk: (blk[0, last(b, blk)], 0, 0)),
            pl.BlockSpec((None, D_MODEL, D_EXPERT), lambda b, blk: (blk[0, last(b, blk)], 0, 0)),
            pl.BlockSpec((None, D_EXPERT, D_MODEL), lambda b, blk: (blk[0, last(b, blk)], 0, 0)),
        ],
        out_specs=pl.BlockSpec((MOE_ROWS, D_MODEL), lambda b, blk: (b, 0)),
        scratch_shapes=[pltpu.VMEM((D_MODEL, D_EXPERT), BF16), pltpu.VMEM((D_MODEL, D_EXPERT), BF16),
                        pltpu.VMEM((D_EXPERT, D_MODEL), BF16)],
    )
    return pl.pallas_call(
        _expert_kernel,
        grid_spec=grid_spec,
        out_shape=jax.ShapeDtypeStruct((n_rows, D_MODEL), F32),
        compiler_params=pltpu.CompilerParams(
            dimension_semantics=("arbitrary",), vmem_limit_bytes=VMEM_LIMIT_BYTES),
        name="moe_experts",
    )(blk_t, xs, w_g, w_u, w_d)


def _combine_kernel(n_first, dest_ref, x1_ref, route_ref, ys_ref, oa_ref, ob_ref, gbuf, sem):
    n = pl.program_id(0)
    rows = x1_ref.shape[0]

    def issue(t, carry):
        for k in range(2):
            pltpu.make_async_copy(ys_ref.at[pl.ds(dest_ref[k, t], 1)], gbuf.at[k, pl.ds(t, 1)], sem).start()
        return carry

    lax.fori_loop(0, rows, issue, 0, unroll=8)
    for k in range(2):
        pltpu.make_async_copy(ys_ref.at[pl.ds(0, rows)], gbuf.at[k], sem).wait()
    rec = route_ref[...]
    lane = lax.broadcasted_iota(jnp.int32, rec.shape, 1)
    w0 = jnp.sum(jnp.where(lane == L_W0, rec, 0.0), axis=1, keepdims=True)
    w1 = jnp.sum(jnp.where(lane == L_W1, rec, 0.0), axis=1, keepdims=True)
    res = x1_ref[...] + (gbuf[0] * w0 + gbuf[1] * w1)

    @pl.when(n < n_first)
    def _():
        oa_ref[...] = res

    @pl.when(n >= n_first)
    def _():
        ob_ref[...] = res


def _combine(dest_t, x1, route, ys, t_first):
    t = x1.shape[0]
    n_first = t_first // TM
    return pl.pallas_call(
        functools.partial(_combine_kernel, n_first),
        grid=(t // TM,),
        in_specs=[pl.BlockSpec((None, 8, TM), lambda n: (n, 0, 0), memory_space=pltpu.SMEM),
                  pl.BlockSpec((TM, D_MODEL), lambda n: (n, 0)),
                  pl.BlockSpec((TM, LANES), lambda n: (n, 0)),
                  pl.BlockSpec(memory_space=pl.ANY)],
        out_specs=[pl.BlockSpec((TM, D_MODEL), lambda n: (jnp.minimum(n, n_first - 1), 0)),
                   pl.BlockSpec((TM, D_MODEL), lambda n: (jnp.maximum(n - n_first, 0), 0))],
        out_shape=[jax.ShapeDtypeStruct((t_first, D_MODEL), F32),
                   jax.ShapeDtypeStruct((t - t_first, D_MODEL), F32)],
        scratch_shapes=[pltpu.VMEM((2, TM, D_MODEL), F32), pltpu.SemaphoreType.DMA],
        compiler_params=pltpu.CompilerParams(
            dimension_semantics=("arbitrary",), vmem_limit_bytes=VMEM_LIMIT_BYTES),
        name="moe_combine",
    )(dest_t, x1, route, ys)


def kernel(x_prompt, x_sample, cache_kv_w128, cache_kv_w512, cache_kv_w2048, state_pool, norm1_w, w_in, q_norm_w, k_norm_w, pool_w, pool_scale, w_branch_pool, w_branch_attn, w_out, norm2_w, w_router_group, b_router_group, w_router_expert, b_router_expert, w_expert_gate, w_expert_up, w_expert_down):
    assert x_prompt.shape[0] == 1 and norm1_w.shape[0] == 1
    layer = 0
    s_p = x_prompt.shape[1]
    n_b, n_t, _ = x_sample.shape
    t_s = n_b * n_t
    t_all = s_p + t_s
    caches = (cache_kv_w128[layer], cache_kv_w512[layer], cache_kv_w2048[layer])

    w_bf = w_in[layer].astype(BF16)
    qw_row = jnp.tile(q_norm_w[layer][:, None, :], (1, N_HEADS, 1)).reshape(-1) * (HEAD_DIM ** -0.5)
    kw_row = jnp.tile(k_norm_w[layer][:, None, :], (1, N_HEADS, 1)).reshape(-1)
    poolw_bf = pool_w[layer].astype(BF16)
    wbp, wba, wo = (w.astype(BF16) for w in (w_branch_pool[layer], w_branch_attn[layer], w_out[layer]))
    wr_hi, wr_lo, br = _router_weights(w_router_group[layer], b_router_group[layer],
                                       w_router_expert[layer], b_router_expert[layer])

    ya_p, gates_p, q_r, k_r, v_r, kvt, ut = _proj_prompt(
        x_prompt[0], norm1_w[layer], w_bf, qw_row, kw_row, poolw_bf, pool_scale[layer])
    yb_p = _attn_prompt(q_r, k_r, v_r)

    xs_t = jnp.transpose(x_sample, (1, 0, 2)).reshape(t_s, D_MODEL)
    state_t = jnp.transpose(state_pool[layer], (1, 0, 2))
    ya_s, gates_s, q_s, k_s, v_s, u_s = _proj_sample(
        xs_t, norm1_w[layer], w_bf, qw_row, kw_row, poolw_bf, pool_scale[layer], state_t, n_b, n_t)
    split = lambda a: a.reshape(n_t, n_b, N_DIL, N_HEADS, HEAD_DIM)
    per_head = lambda a: jnp.transpose(split(a), (1, 3, 2, 0, 4))
    new_t = jnp.transpose(jnp.stack([split(k_s), split(v_s)]), (2, 3, 0, 4, 5, 1))
    new_t = jnp.pad(new_t, ((0, 0),) * 5 + ((LANES - n_t, 0),))
    caches_t = [jnp.transpose(c, (0, 2, 3, 4, 1)) for c in caches]
    o_s, *kv_t = _attn_roll(per_head(q_s), per_head(k_s), per_head(v_s), new_t, caches_t)
    yb_s = jnp.transpose(o_s, (2, 0, 1, 3)).reshape(t_s, ATTN_WIDTH).astype(BF16)
    kv_s = [jnp.transpose(c, (0, 4, 1, 2, 3)) for c in kv_t]

    x1, h, route, cnt = _merge_route(x_prompt[0], xs_t, ya_p, ya_s, yb_p, yb_s, gates_p, gates_s,
                                     wbp, wba, wo, norm2_w[layer], wr_hi, wr_lo, br)

    n_rows = _moe_blocks(t_all) * MOE_ROWS
    dest, blk, xs = _plan(cnt, route, n_rows)
    dest_t = jnp.transpose(dest[:, :8].reshape(t_all // TM, TM, 8), (0, 2, 1))
    blk_t = jnp.transpose(blk[:, :2])
    xs = _dispatch(dest_t, h, xs)
    ys = _experts(blk_t, xs, w_expert_gate[layer], w_expert_up[layer], w_expert_down[layer])
    y_p, y_s = _combine(dest_t, x1, route, ys, s_p)

    y_prompt = y_p[None]
    y_sample = jnp.transpose(y_s.reshape(n_t, n_b, D_MODEL), (1, 0, 2))
    pool_prompt = ut[1:][None, None]
    u_new = jnp.transpose(u_s.reshape(n_t, n_b, POOL_WIDTH), (1, 0, 2))
    pool_sample = jnp.concatenate([state_pool[layer][:, n_t:], u_new], axis=1)[None]
    inv = np.argsort(np.asarray(_class_order()))
    kv_nat = jnp.transpose(kvt[inv], (1, 0, 2)).reshape(-1, 2, N_DIL, N_HEADS, HEAD_DIM)
    outs = [y_prompt, y_sample, pool_prompt, pool_sample]
    for g, (win, _) in enumerate(DILATION_PAIRS):
        keep = min(win, s_p)
        outs.append(kv_nat[kv_nat.shape[0] - keep:, :, g][None, None])
        outs.append(kv_s[g][None])
    return tuple(outs)
```

```python
import functools

import numpy as np
import jax
import jax.numpy as jnp
from jax import lax
from jax.experimental import pallas as pl
from jax.experimental.pallas import tpu as pltpu

F32 = jnp.float32
BF16 = jnp.bfloat16

D_MODEL = 1024
PAST_LEN = 8192
POOL_WIDTH = 512
POOL_WINDOWS = (2, 4, 8, 16)
POOL_GROUP_DIM = 128
POOL_STATE = 15
HEAD_DIM = 64
N_HEADS = 8
DILATION_PAIRS = ((128, 1), (512, 4), (2048, 16))
N_DIL = 3
ATTN_WIDTH = 512
QKV_WIDTH = 1536
ROT_DIM = 16
ROPE_THETA = 500000.0
N_BRANCH = 2
IN_COLS = POOL_WIDTH + 3 * QKV_WIDTH + N_BRANCH * D_MODEL
N_EXPERT_GROUPS = 4
EXPERTS_PER_GROUP = 8
N_EXPERTS = 32
D_EXPERT = 512
EPS = 1e-6

LANES = 128
VMEM_LIMIT_BYTES = 56 * 1024 * 1024

C_U = 0
C_Q = POOL_WIDTH
C_K = C_Q + QKV_WIDTH
C_V = C_K + QKV_WIDTH
C_G = C_V + QKV_WIDTH

N_CLASS = 16
TI = 256
ROWS_PER_CLASS = TI // N_CLASS
NEG = -1e30


def _class_order():
    return [(c // 4) + 4 * (c % 4) for c in range(N_CLASS)]


def _const_spec(shape, single_buffer=False):
    nd = len(shape)
    kw = {}
    if single_buffer:
        kw["pipeline_mode"] = pl.Buffered(1)
    return pl.BlockSpec(shape, lambda *_: (0,) * nd, **kw)


def _rms(x, w):
    ms = jnp.mean(x * x, axis=-1, keepdims=True)
    return x * lax.rsqrt(ms + EPS) * w


def _head_norm_rope(y, wrow, seg, segt, cos, sina, sinb):
    sq = (y * y).astype(BF16)
    ssum = jnp.dot(sq, seg, preferred_element_type=F32)
    r = lax.rsqrt(ssum * (1.0 / HEAD_DIM) + EPS)
    r_hi = r.astype(BF16)
    r_lo = (r - r_hi.astype(F32)).astype(BF16)
    rexp = jnp.dot(jnp.concatenate([r_hi, r_lo], axis=1), segt, preferred_element_type=F32)
    yn = y * rexp * wrow
    outs = []
    for j in range(QKV_WIDTH // LANES):
        c = yn[:, j * LANES:(j + 1) * LANES]
        outs.append(c * cos + pltpu.roll(c, LANES - ROT_DIM // 2, 1) * sina
                    + pltpu.roll(c, ROT_DIM // 2, 1) * sinb)
    return jnp.concatenate(outs, axis=1)


def _pool_mix_out(pooled, poolw_ref, pool_scale):
    outs = []
    for g in range(len(POOL_WINDOWS)):
        outs.append(jnp.dot(pooled[g].astype(BF16), poolw_ref[g], preferred_element_type=F32))
    return jnp.concatenate(outs, axis=1) * pool_scale


def _proj_prompt_kernel(x_ref, n1w_ref, w_ref, qw_ref, kw_ref, cosr_ref, sinr_ref, cosn_ref, sinn_ref,
                        rotm_ref, perm_ref, seg_ref, segt_ref, poolw_ref, pools_ref,
                        ya_ref, gates_ref, q_ref, k_ref, v_ref, kvt_ref, ut_ref,
                        uext_ref):
    n = pl.program_id(0)
    xn = _rms(x_ref[...], n1w_ref[...]).astype(BF16)

    u = jnp.dot(xn, w_ref[:, C_U:C_U + POOL_WIDTH], preferred_element_type=F32)

    @pl.when(n == 0)
    def _():
        uext_ref[0:16, :] = jnp.zeros((16, POOL_WIDTH), F32)

    uext_ref[16:16 + TI, :] = u
    pos = n * TI + lax.broadcasted_iota(jnp.int32, (TI, 1), 0)
    pooled = []
    for g, w in enumerate(POOL_WINDOWS):
        lanes = slice(g * POOL_GROUP_DIM, (g + 1) * POOL_GROUP_DIM)
        s = uext_ref[16:16 + TI, lanes]
        for m in range(1, w):
            s = s + uext_ref[16 - m:16 - m + TI, lanes]
        cnt = jnp.minimum(pos + 1, w).astype(F32)
        pooled.append(s / cnt - u[:, lanes])
    ya_ref[...] = _pool_mix_out(pooled, poolw_ref, pools_ref[...]).astype(BF16)
    ut_ref[...] = u[TI - 16:, :]
    uext_ref[0:16, :] = uext_ref[TI:TI + 16, :]

    gl = jnp.dot(xn, w_ref[:, C_G:C_G + N_BRANCH * D_MODEL], preferred_element_type=F32)
    gates_ref[...] = jax.nn.sigmoid(gl).astype(BF16)

    xp = jnp.dot(perm_ref[...], xn, preferred_element_type=F32).astype(BF16)
    cn, sn = cosn_ref[pl.ds(n, 1), :], sinn_ref[pl.ds(n, 1), :]
    cr, sr = cosr_ref[...], sinr_ref[...]
    cos = cn * cr - sn * sr
    sin = sn * cr + cn * sr
    sina, sinb = sin * rotm_ref[0:1, :], sin * rotm_ref[1:2, :]
    seg, segt = seg_ref[...], segt_ref[...]
    shp = (N_CLASS, ROWS_PER_CLASS, QKV_WIDTH)
    yq = jnp.dot(xp, w_ref[:, C_Q:C_Q + QKV_WIDTH], preferred_element_type=F32)
    q_ref[...] = _head_norm_rope(yq, qw_ref[...], seg, segt, cos, sina, sinb).astype(BF16).reshape(shp)
    yk = jnp.dot(xp, w_ref[:, C_K:C_K + QKV_WIDTH], preferred_element_type=F32)
    kr = _head_norm_rope(yk, kw_ref[...], seg, segt, cos, sina, sinb)
    k_ref[...] = kr.astype(BF16).reshape(shp)
    kvt_ref[:, :, 0:QKV_WIDTH] = kr.reshape(shp)
    yv = jnp.dot(xp, w_ref[:, C_V:C_V + QKV_WIDTH], preferred_element_type=F32)
    v_ref[...] = yv.astype(BF16).reshape(shp)
    kvt_ref[:, :, QKV_WIDTH:2 * QKV_WIDTH] = yv.reshape(shp)


def _rope_tables(pos):
    half = ROT_DIM // 2
    inv = ROPE_THETA ** (-(jnp.arange(half, dtype=F32) * 2.0 / ROT_DIM))
    ang = pos.astype(F32)[:, None] * inv[None, :]
    c, s = jnp.cos(ang), jnp.sin(ang)
    n = pos.shape[0]
    ones = jnp.ones((n, HEAD_DIM - ROT_DIM), F32)
    zeros = jnp.zeros((n, HEAD_DIM - ROT_DIM), F32)
    z8 = jnp.zeros((n, half), F32)
    cos64 = jnp.concatenate([c, c, ones], axis=1)
    sina64 = jnp.concatenate([-s, z8, zeros], axis=1)
    sinb64 = jnp.concatenate([z8, s, zeros], axis=1)
    rep = lambda t: jnp.concatenate([t, t], axis=1)
    return rep(cos64), rep(sina64), rep(sinb64)


def _rope_lane_freq():
    half = ROT_DIM // 2
    inv = ROPE_THETA ** (-(np.arange(half, dtype=np.float64) * 2.0 / ROT_DIM))
    d = np.arange(LANES) % HEAD_DIM
    freq = np.where(d < ROT_DIM, inv[d % half], 0.0)
    mask = np.zeros((8, LANES))
    mask[0] = np.where(d < half, -1.0, 0.0)
    mask[1] = np.where((d >= half) & (d < ROT_DIM), 1.0, 0.0)
    return freq, mask


def _seg_mats():
    h = np.arange(QKV_WIDTH) // HEAD_DIM
    seg = (h[:, None] == np.arange(LANES)[None, :]).astype(np.float32)
    return jnp.asarray(seg, BF16), jnp.asarray(np.concatenate([seg.T, seg.T], axis=0), BF16)


def _perm_mat():
    cls = _class_order()
    p = np.zeros((TI, TI), np.float32)
    for c in range(N_CLASS):
        for j in range(ROWS_PER_CLASS):
            p[c * ROWS_PER_CLASS + j, N_CLASS * j + cls[c]] = 1.0
    return p


def _proj_prompt(x, n1w, w_bf, qw_row, kw_row, poolw_bf, pool_scale):
    s = x.shape[0]
    nt = s // TI
    cls = np.asarray(_class_order())
    perm = _perm_mat()
    j = np.arange(ROWS_PER_CLASS)
    inner = (N_CLASS * j[None, :] + cls[:, None]).reshape(-1).astype(np.float64)
    freq, rot_mask = _rope_lane_freq()
    ang_r = inner[:, None] * freq[None, :]
    ang_n = (np.arange(nt, dtype=np.float64) * TI)[:, None] * freq[None, :]
    f32c = lambda a: jnp.asarray(a.astype(np.float32))
    cosr, sinr, cosn, sinn = f32c(np.cos(ang_r)), f32c(np.sin(ang_r)), f32c(np.cos(ang_n)), f32c(np.sin(ang_n))
    rotm = f32c(rot_mask)
    seg, segt = _seg_mats()
    n_i = s // N_CLASS
    tail_i = DILATION_PAIRS[-1][0] // N_CLASS
    tail_blocks = tail_i // ROWS_PER_CLASS
    row = lambda a: a.reshape(1, -1)
    tile = lambda n: (n, 0)
    rblk = lambda n: (0, n, 0)
    tblk = lambda n: (0, jnp.maximum(n - (nt - tail_blocks), 0), 0)
    outs = pl.pallas_call(
        _proj_prompt_kernel,
        grid=(nt,),
        in_specs=[
            pl.BlockSpec((TI, D_MODEL), tile),
            _const_spec((1, D_MODEL)),
            _const_spec((D_MODEL, IN_COLS), single_buffer=True),
            _const_spec((1, QKV_WIDTH)),
            _const_spec((1, QKV_WIDTH)),
            _const_spec((TI, LANES)), _const_spec((TI, LANES)),
            _const_spec((nt, LANES)), _const_spec((nt, LANES)), _const_spec((8, LANES)),
            _const_spec((TI, TI)),
            _const_spec((QKV_WIDTH, LANES)),
            _const_spec((2 * LANES, QKV_WIDTH)),
            _const_spec((len(POOL_WINDOWS), POOL_GROUP_DIM, POOL_GROUP_DIM)),
            _const_spec((1, POOL_WIDTH)),
        ],
        out_specs=[
            pl.BlockSpec((TI, POOL_WIDTH), tile),
            pl.BlockSpec((TI, N_BRANCH * D_MODEL), tile),
            pl.BlockSpec((N_CLASS, ROWS_PER_CLASS, QKV_WIDTH), rblk),
            pl.BlockSpec((N_CLASS, ROWS_PER_CLASS, QKV_WIDTH), rblk),
            pl.BlockSpec((N_CLASS, ROWS_PER_CLASS, QKV_WIDTH), rblk),
            pl.BlockSpec((N_CLASS, ROWS_PER_CLASS, 2 * QKV_WIDTH), tblk),
            _const_spec((16, POOL_WIDTH)),
        ],
        out_shape=[
            jax.ShapeDtypeStruct((s, POOL_WIDTH), BF16),
            jax.ShapeDtypeStruct((s, N_BRANCH * D_MODEL), BF16),
            jax.ShapeDtypeStruct((N_CLASS, n_i, QKV_WIDTH), BF16),
            jax.ShapeDtypeStruct((N_CLASS, n_i, QKV_WIDTH), BF16),
            jax.ShapeDtypeStruct((N_CLASS, n_i, QKV_WIDTH), BF16),
            jax.ShapeDtypeStruct((N_CLASS, tail_i, 2 * QKV_WIDTH), F32),
            jax.ShapeDtypeStruct((16, POOL_WIDTH), F32),
        ],
        scratch_shapes=[pltpu.VMEM((16 + TI, POOL_WIDTH), F32)],
        compiler_params=pltpu.CompilerParams(
            dimension_semantics=("arbitrary",), vmem_limit_bytes=VMEM_LIMIT_BYTES),
        name="proj_prompt",
    )(x, row(n1w), w_bf, row(qw_row), row(kw_row), cosr, sinr, cosn, sinn, rotm,
      jnp.asarray(perm, BF16), seg, segt, poolw_bf, row(pool_scale))
    return outs


AT_I = 128
N_KEYS_BACK = 128
BLOCKS_PER_TRIP = 2


def _attn_update(q, k, v, bias, m_old, l_old, acc_old):
    nq, nk = q.shape[0], k.shape[0]
    lane = lax.broadcasted_iota(jnp.int32, (nq, LANES), 1)
    first = lane < HEAD_DIM
    bias2 = jnp.concatenate([bias, bias], axis=0)
    ones = jnp.ones((nk, LANES), BF16)
    m_out, l_out, a_out = [], [], []
    for hp in range(N_HEADS // 2):
        sl = slice(hp * LANES, (hp + 1) * LANES)
        qp, kp, vp = q[:, sl], k[:, sl], v[:, sl]
        mo, lo, ao = m_old[:, sl], l_old[:, sl], acc_old[:, sl]
        zero = jnp.zeros_like(qp)
        q2 = jnp.concatenate([jnp.where(first, qp, zero), jnp.where(first, zero, qp)], axis=0)
        s2 = lax.dot_general(q2, kp, (((1,), (1,)), ((), ())), preferred_element_type=F32) + bias2
        mo2 = jnp.concatenate([mo[:, 0:1], mo[:, HEAD_DIM:HEAD_DIM + 1]], axis=0)
        mn2 = jnp.maximum(mo2, jnp.max(s2, axis=1, keepdims=True))
        p2 = jnp.exp(s2 - mn2).astype(BF16)
        pv2 = jnp.dot(p2, jnp.concatenate([vp, ones], axis=1), preferred_element_type=F32)
        pv = jnp.where(first, pv2[:nq, :LANES], pv2[nq:, :LANES])
        lc = jnp.where(first, pv2[:nq, LANES:], pv2[nq:, LANES:])
        mn = jnp.where(first, mn2[:nq], mn2[nq:])
        alpha = jnp.exp(mo - mn)
        m_out.append(mn)
        l_out.append(alpha * lo + lc)
        a_out.append(alpha * ao + pv)
    cat = lambda xs: jnp.concatenate(xs, axis=1)
    return cat(m_out), cat(l_out), cat(a_out)


def _attn_prompt_kernel(q_ref, kc_ref, kp_ref, vc_ref, vp_ref, b0_ref, b1_ref, b2_ref,
                        r0_ref, r1_ref, r2_ref, o_ref,
                        kcat, vcat, acc, m_s, l_s, ost):
    n = pl.program_id(0)
    g = pl.program_id(1)
    i0 = n * AT_I
    kcat[:, 0:AT_I, :] = kp_ref[...]
    kcat[:, AT_I:2 * AT_I, :] = kc_ref[...]
    vcat[:, 0:AT_I, :] = vp_ref[...]
    vcat[:, AT_I:2 * AT_I, :] = vc_ref[...]

    def key_bias(b_ref, r_ref, base):
        return b_ref[...] + jnp.where(base + r_ref[...] >= 0, 0.0, NEG)

    @pl.when(g == 0)
    def _():
        m_s[...] = jnp.full(m_s.shape, NEG, F32)
        l_s[...] = jnp.zeros(l_s.shape, F32)
        acc[...] = jnp.zeros(acc.shape, F32)

        def body(s, carry):
            qo = pl.multiple_of(16 * s, 16)
            ko = pl.multiple_of(AT_I - 16 + 16 * s, 16)
            q16 = [q_ref[c, pl.ds(qo, 16), :] for c in range(N_CLASS)]
            k32 = [kcat[c, pl.ds(ko, 32), :] for c in range(N_CLASS)]
            v32 = [vcat[c, pl.ds(ko, 32), :] for c in range(N_CLASS)]
            cat = lambda f: jnp.concatenate([f(c) for c in range(N_CLASS)], axis=0)
            work = []
            for half in range(2):
                qh = pl.multiple_of(qo + 8 * half, 8)
                q = cat(lambda c: q16[c][8 * half:8 * half + 8])
                k = cat(lambda c: k32[c][8 + 8 * half:24 + 8 * half])
                v = cat(lambda c: v32[c][8 + 8 * half:24 + 8 * half])
                mo = cat(lambda c: m_s[c, pl.ds(qh, 8), :])
                lo = cat(lambda c: l_s[c, pl.ds(qh, 8), :])
                ao = cat(lambda c: acc[c, pl.ds(qh, 8), :])
                bias = key_bias(b0_ref, r0_ref, i0 + 16 * s + 8 * half)
                work.append((qh, (q, k, v, bias, mo, lo, ao)))
            done = [(qh, _attn_update(*args)) for qh, args in work]
            for qh, (mn, ln, an) in done:
                for c in range(N_CLASS):
                    m_s[c, pl.ds(qh, 8), :] = mn[8 * c:8 * c + 8]
                    l_s[c, pl.ds(qh, 8), :] = ln[8 * c:8 * c + 8]
                    acc[c, pl.ds(qh, 8), :] = an[8 * c:8 * c + 8]
            return carry

        lax.fori_loop(0, AT_I // 16, body, 0)

    @pl.when(g == 1)
    def _():
        def load(r4, s):
            qo = pl.multiple_of(32 * s, 32)
            ko = pl.multiple_of(AT_I - 32 + 32 * s, 32)
            cat = lambda f: jnp.concatenate([f(4 * r4 + a) for a in range(4)], axis=0)
            q = cat(lambda c: q_ref[c, pl.ds(qo, 32), :])
            k = cat(lambda c: kcat[c, pl.ds(ko, 64), :])
            v = cat(lambda c: vcat[c, pl.ds(ko, 64), :])
            mo = cat(lambda c: m_s[c, pl.ds(qo, 32), :])
            lo = cat(lambda c: l_s[c, pl.ds(qo, 32), :])
            ao = cat(lambda c: acc[c, pl.ds(qo, 32), :])
            bias = key_bias(b1_ref, r1_ref, i0 + 32 * s)
            return q, k, v, bias, mo, lo, ao

        def store(r4, s, mn, ln, an):
            qo = pl.multiple_of(32 * s, 32)
            for a in range(4):
                c = 4 * r4 + a
                m_s[c, pl.ds(qo, 32), :] = mn[32 * a:32 * a + 32]
                l_s[c, pl.ds(qo, 32), :] = ln[32 * a:32 * a + 32]
                acc[c, pl.ds(qo, 32), :] = an[32 * a:32 * a + 32]

        def body(idx, carry):
            per_class = 4 // BLOCKS_PER_TRIP
            where = [(idx // per_class, BLOCKS_PER_TRIP * (idx % per_class) + j) for j in range(BLOCKS_PER_TRIP)]
            work = [load(r4, s) for r4, s in where]
            done = [_attn_update(*args) for args in work]
            for (r4, s), res in zip(where, done):
                store(r4, s, *res)
            return carry

        lax.fori_loop(0, 16 // BLOCKS_PER_TRIP, body, 0)

    @pl.when(g == 2)
    def _():
        bias = key_bias(b2_ref, r2_ref, i0)

        def body(cc, carry):
            cs = [BLOCKS_PER_TRIP * cc + j for j in range(BLOCKS_PER_TRIP)]
            work = [(q_ref[c], kcat[c], vcat[c], bias, m_s[c], l_s[c], acc[c]) for c in cs]
            done = [_attn_update(*args) for args in work]
            for c, (mn, ln, an) in zip(cs, done):
                m_s[c] = mn
                l_s[c] = ln
                acc[c] = an
            return carry

        lax.fori_loop(0, N_CLASS // BLOCKS_PER_TRIP, body, 0)

        cls = _class_order()
        for c in range(N_CLASS):
            o = acc[c] / l_s[c]
            for jc in range(ATTN_WIDTH // LANES):
                ost[jc, pl.ds(cls[c], AT_I, stride=N_CLASS), :] = o[:, jc * LANES:(jc + 1) * LANES]
        o_ref[...] = jnp.concatenate([ost[jc] for jc in range(ATTN_WIDTH // LANES)], axis=1).astype(BF16)


def _attn_bias_tables():
    cls = np.asarray(_class_order())

    def bias(diff):
        return np.where((diff >= 0) & (diff <= N_KEYS_BACK), 0.0, NEG).astype(np.float32)

    c, r = np.divmod(np.arange(128), 8)
    c2, r2 = np.divmod(np.arange(256), 16)
    d0 = 16 * (r[:, None] - r2[None, :] + 8) + cls[c][:, None] - cls[c2][None, :]
    rel0 = (r2 - 8).astype(np.int32)[None, :]
    a, r = np.divmod(np.arange(128), 32)
    a2, r2 = np.divmod(np.arange(256), 64)
    d1 = 4 * (r[:, None] - r2[None, :] + 32) + a[:, None] - a2[None, :]
    rel1 = (r2 - 32).astype(np.int32)[None, :]
    r = np.arange(128)
    r2 = np.arange(256)
    d2 = r[:, None] - r2[None, :] + 128
    rel2 = (r2 - 128).astype(np.int32)[None, :]
    return (jnp.asarray(bias(d0)), jnp.asarray(bias(d1)), jnp.asarray(bias(d2)),
            jnp.asarray(rel0), jnp.asarray(rel1), jnp.asarray(rel2))


def _attn_prompt(q_r, k_r, v_r):
    n_i = q_r.shape[1]
    s = n_i * N_CLASS
    nsteps = n_i // AT_I
    b0, b1, b2, r0, r1, r2 = _attn_bias_tables()
    blk = (N_CLASS, AT_I, ATTN_WIDTH)
    cur = lambda n, g: (0, n, g)
    prev = lambda n, g: (0, jnp.maximum(n - 1, 0), g)
    return pl.pallas_call(
        _attn_prompt_kernel,
        grid=(nsteps, N_DIL),
        in_specs=[
            pl.BlockSpec(blk, cur), pl.BlockSpec(blk, cur), pl.BlockSpec(blk, prev),
            pl.BlockSpec(blk, cur), pl.BlockSpec(blk, prev),
            _const_spec(b0.shape), _const_spec(b1.shape), _const_spec(b2.shape),
            _const_spec(r0.shape), _const_spec(r1.shape), _const_spec(r2.shape),
        ],
        out_specs=pl.BlockSpec((AT_I * N_CLASS, ATTN_WIDTH), lambda n, g: (n, 0)),
        out_shape=jax.ShapeDtypeStruct((s, ATTN_WIDTH), BF16),
        scratch_shapes=[
            pltpu.VMEM((N_CLASS, 2 * AT_I, ATTN_WIDTH), BF16),
            pltpu.VMEM((N_CLASS, 2 * AT_I, ATTN_WIDTH), BF16),
            pltpu.VMEM((N_CLASS, AT_I, ATTN_WIDTH), F32),
            pltpu.VMEM((N_CLASS, AT_I, ATTN_WIDTH), F32),
            pltpu.VMEM((N_CLASS, AT_I, ATTN_WIDTH), F32),
            pltpu.VMEM((ATTN_WIDTH // LANES, AT_I * N_CLASS, LANES), F32),
        ],
        compiler_params=pltpu.CompilerParams(
            dimension_semantics=("arbitrary", "arbitrary"), vmem_limit_bytes=VMEM_LIMIT_BYTES),
        name="attn_prompt",
    )(q_r, k_r, k_r, v_r, v_r, b0, b1, b2, r0, r1, r2)


def _proj_sample_kernel(n_b, n_t, x_ref, xb_ref, n1w_ref, w_ref, qw_ref, kw_ref, cos_ref, sina_ref, sinb_ref,
                        seg_ref, segt_ref, poolw_ref, pools_ref, state_ref,
                        ya_ref, gates_ref, q_ref, k_ref, v_ref, u_ref, kt_ref, vt_ref):
    xn = _rms(x_ref[...], n1w_ref[...]).astype(BF16)
    u = jnp.dot(xn, w_ref[:, C_U:C_U + POOL_WIDTH], preferred_element_type=F32)
    u_ref[...] = u

    def ext(j, lanes):
        if j < POOL_STATE:
            return state_ref[j, :, lanes]
        return u[(j - POOL_STATE) * n_b:(j - POOL_STATE + 1) * n_b, lanes]

    pooled = []
    for g, w in enumerate(POOL_WINDOWS):
        lanes = slice(g * POOL_GROUP_DIM, (g + 1) * POOL_GROUP_DIM)
        rows = []
        for i in range(n_t):
            s = ext(POOL_STATE + i, lanes)
            for m in range(1, w):
                s = s + ext(POOL_STATE + i - m, lanes)
            cnt = float(min(PAST_LEN + i + 1, w))
            rows.append(s / cnt - ext(POOL_STATE + i, lanes))
        pooled.append(jnp.concatenate(rows, axis=0))
    ya_ref[...] = _pool_mix_out(pooled, poolw_ref, pools_ref[...]).astype(BF16)

    gl = jnp.dot(xn, w_ref[:, C_G:C_G + N_BRANCH * D_MODEL], preferred_element_type=F32)
    gates_ref[...] = jax.nn.sigmoid(gl).astype(BF16)

    xnb = _rms(xb_ref[...], n1w_ref[...]).astype(BF16)
    cos, sina, sinb = cos_ref[...], sina_ref[...], sinb_ref[...]
    seg, segt = seg_ref[...], segt_ref[...]
    yq = jnp.dot(xnb, w_ref[:, C_Q:C_Q + QKV_WIDTH], preferred_element_type=F32)
    q_ref[...] = _head_norm_rope(yq, qw_ref[...], seg, segt, cos, sina, sinb)
    yk = jnp.dot(xnb, w_ref[:, C_K:C_K + QKV_WIDTH], preferred_element_type=F32)
    kr = _head_norm_rope(yk, kw_ref[...], seg, segt, cos, sina, sinb)
    k_ref[...] = kr
    kt_ref[...] = kr.T
    yv = jnp.dot(xnb, w_ref[:, C_V:C_V + QKV_WIDTH], preferred_element_type=F32)
    v_ref[...] = yv
    vt_ref[...] = yv.T


def _proj_sample(x, xb, n1w, w_bf, qw_row, kw_row, poolw_bf, pool_scale, state_t, n_b, n_t):
    rows = n_b * n_t
    pos = PAST_LEN + jnp.arange(rows, dtype=jnp.int32) % n_t
    cos, sina, sinb = _rope_tables(pos)
    seg, segt = _seg_mats()
    row = lambda a: a.reshape(1, -1)
    args = (x, xb, row(n1w), w_bf, row(qw_row), row(kw_row), cos, sina, sinb, seg, segt,
            poolw_bf, row(pool_scale), state_t)
    f32_qkv = jax.ShapeDtypeStruct((rows, QKV_WIDTH), F32)
    out_shape = [
        jax.ShapeDtypeStruct((rows, POOL_WIDTH), BF16),
        jax.ShapeDtypeStruct((rows, N_BRANCH * D_MODEL), BF16),
        f32_qkv, f32_qkv, f32_qkv,
        jax.ShapeDtypeStruct((rows, POOL_WIDTH), F32),
        jax.ShapeDtypeStruct((QKV_WIDTH, rows), F32),
        jax.ShapeDtypeStruct((QKV_WIDTH, rows), F32),
    ]
    return pl.pallas_call(
        functools.partial(_proj_sample_kernel, n_b, n_t),
        grid=(1,),
        in_specs=[_const_spec(a.shape) for a in args],
        out_specs=[_const_spec(o.shape) for o in out_shape],
        out_shape=out_shape,
        compiler_params=pltpu.CompilerParams(
            dimension_semantics=("arbitrary",), vmem_limit_bytes=VMEM_LIMIT_BYTES),
        name="proj_sample",
    )(*args)


HEADS_PER_STEP = 4


def _attn_roll_kernel(n_t, q_ref, kn_ref, vn_ref, newt_ref, c0_ref, c1_ref, c2_ref,
                      b0_ref, b1_ref, b2_ref, bn_ref, o_ref, o0_ref, o1_ref, o2_ref):
    caches = (c0_ref, c1_ref, c2_ref)
    outs = (o0_ref, o1_ref, o2_ref)
    biases = (b0_ref, b1_ref, b2_ref)
    nt_contract = (((1,), (1,)), ((), ()))
    col = pl.program_id(1) * n_t
    new_tile = pl.multiple_of((col // LANES) * LANES, LANES)
    new_shift = (LANES - n_t) - col % LANES
    for hh in range(HEADS_PER_STEP):
        scores, values = [], []
        for g in range(N_DIL):
            q = q_ref[hh, g].astype(BF16)
            kc = caches[g][0, hh].astype(BF16)
            scores.append(jnp.dot(q, kc, preferred_element_type=F32) + biases[g][...])
            kn = kn_ref[hh, g].astype(BF16)
            scores.append(lax.dot_general(q, kn, nt_contract, preferred_element_type=F32) + bn_ref[g])
            values.append(caches[g][1, hh].astype(BF16))
            values.append(vn_ref[hh, g].astype(BF16))
        m = functools.reduce(jnp.maximum, [jnp.max(s, axis=1, keepdims=True) for s in scores])
        l = jnp.zeros_like(m)
        o = jnp.zeros((n_t, HEAD_DIM), F32)
        for idx, (s, v) in enumerate(zip(scores, values)):
            p = jnp.exp(s - m)
            l = l + jnp.sum(p, axis=1, keepdims=True)
            pb = p.astype(BF16)
            if idx % 2 == 0:
                o = o + lax.dot_general(pb, v, nt_contract, preferred_element_type=F32)
            else:
                o = o + jnp.dot(pb, v, preferred_element_type=F32)
        o_ref[hh] = o / l

        for g in range(N_DIL):
            length = caches[g].shape[-1]
            lane = lax.broadcasted_iota(jnp.int32, (HEAD_DIM, LANES), 1)
            for kv in range(2):
                rolled = pltpu.roll(caches[g][kv, hh], length - n_t, 1)
                new = pltpu.roll(newt_ref[kv, g, hh, :, pl.ds(new_tile, LANES)], new_shift, 1)
                if length > LANES:
                    outs[g][kv, hh, :, 0:length - LANES] = rolled[:, 0:length - LANES]
                outs[g][kv, hh, :, length - LANES:length] = jnp.where(
                    lane >= LANES - n_t, new, rolled[:, length - LANES:length])


def _attn_roll_bias(n_t):
    i = np.arange(n_t)
    bias, bias_new = [], []
    for win, dil in DILATION_PAIRS:
        length = min(win, PAST_LEN)
        delta = length + i[:, None] - np.arange(length)[None, :]
        ok = (delta % dil == 0) & (delta // dil >= 1) & (delta // dil <= win // dil)
        bias.append(jnp.asarray(np.where(ok, 0.0, NEG).astype(np.float32)))
        dn = i[:, None] - i[None, :]
        okn = (dn >= 0) & (dn % dil == 0) & (dn // dil <= win // dil)
        bias_new.append(np.where(okn, 0.0, NEG).astype(np.float32))
    return bias, jnp.asarray(np.stack(bias_new))


def _attn_roll(q_h, kn_h, vn_h, new_t, caches_t):
    n_b, _, _, n_t, _ = q_h.shape
    bias, bias_new = _attn_roll_bias(n_t)
    hs = HEADS_PER_STEP
    qspec = pl.BlockSpec((None, hs, N_DIL, n_t, HEAD_DIM), lambda h, b: (b, h, 0, 0, 0))
    cspec = lambda c: pl.BlockSpec((None, 2, hs, HEAD_DIM, c.shape[-1]), lambda h, b: (b, 0, h, 0, 0))
    return pl.pallas_call(
        functools.partial(_attn_roll_kernel, n_t),
        grid=(N_HEADS // hs, n_b),
        in_specs=[qspec, qspec, qspec,
                  pl.BlockSpec((2, N_DIL, hs, HEAD_DIM, n_t * n_b), lambda h, b: (0, 0, h, 0, 0))]
                 + [cspec(c) for c in caches_t]
                 + [_const_spec(b.shape) for b in bias] + [_const_spec(bias_new.shape)],
        out_specs=[pl.BlockSpec((None, hs, n_t, HEAD_DIM), lambda h, b: (b, h, 0, 0))]
                  + [cspec(c) for c in caches_t],
        out_shape=[jax.ShapeDtypeStruct((n_b, N_HEADS, n_t, HEAD_DIM), F32)]
                  + [jax.ShapeDtypeStruct(c.shape, c.dtype) for c in caches_t],
        compiler_params=pltpu.CompilerParams(
            dimension_semantics=("arbitrary", "arbitrary"), vmem_limit_bytes=VMEM_LIMIT_BYTES),
        name="attn_roll",
    )(q_h, kn_h, vn_h, new_t, *caches_t, *bias, bias_new)


TM = 512
MOE_ROWS = 256
L_E0, L_E1, L_W0, L_W1, L_R0, L_R1 = 0, 1, 2, 3, 4, 5
L_GROUP = N_EXPERTS


def _lane_min_index(mask, lane):
    return jnp.min(jnp.where(mask, lane, float(LANES)), axis=1, keepdims=True)


def _merge_route_kernel(n_first, xa_ref, xb_ref, yaa_ref, yab_ref, yba_ref, ybb_ref, ga_ref, gb_ref,
                        wbp_ref, wba_ref, wo_ref, n2w_ref, wrh_ref, wrl_ref, br_ref, ltri_ref,
                        x1_ref, h_ref, route_ref, cnt_ref):
    n = pl.program_id(0)
    first = n < n_first
    pick = lambda a, b: jnp.where(first, a[...], b[...])
    pa = jnp.dot(pick(yaa_ref, yab_ref), wbp_ref[...], preferred_element_type=F32)
    pb = jnp.dot(pick(yba_ref, ybb_ref), wba_ref[...], preferred_element_type=F32)
    gts = pick(ga_ref, gb_ref)
    merged = gts[:, :D_MODEL].astype(F32) * pa + gts[:, D_MODEL:].astype(F32) * pb
    x1 = pick(xa_ref, xb_ref) + jnp.dot(merged.astype(BF16), wo_ref[...], preferred_element_type=F32)
    x1_ref[...] = x1
    h = _rms(x1, n2w_ref[...])
    h_ref[...] = h

    h_hi = h.astype(BF16)
    h_lo = (h - h_hi.astype(F32)).astype(BF16)
    logits = (jnp.dot(h_hi, wrh_ref[...], preferred_element_type=F32)
              + jnp.dot(h_hi, wrl_ref[...], preferred_element_type=F32)
              + jnp.dot(h_lo, wrh_ref[...], preferred_element_type=F32)) + br_ref[...]
    rows = logits.shape[0]
    lane = lax.broadcasted_iota(jnp.int32, (rows, LANES), 1).astype(F32)
    is_group = (lane >= L_GROUP) & (lane < L_GROUP + N_EXPERT_GROUPS)
    gl = jnp.where(is_group, logits, NEG)
    gmax = jnp.max(gl, axis=1, keepdims=True)
    g_w = 1.0 / jnp.sum(jnp.exp(gl - gmax), axis=1, keepdims=True)
    g_sel = _lane_min_index(gl == gmax, lane) - L_GROUP
    in_group = (lane >= g_sel * EXPERTS_PER_GROUP) & (lane < (g_sel + 1.0) * EXPERTS_PER_GROUP)
    el = jnp.where(in_group, logits, NEG)
    v0 = jnp.max(el, axis=1, keepdims=True)
    e0 = _lane_min_index(el == v0, lane)
    el2 = jnp.where(lane == e0, NEG, el)
    v1 = jnp.max(el2, axis=1, keepdims=True)
    e1 = _lane_min_index(el2 == v1, lane)
    t = jnp.exp(v1 - v0)
    w0 = g_w / (1.0 + t)
    w1 = g_w * t / (1.0 + t)

    hot0 = lane == e0
    hot1 = lane == e1
    onehot = jnp.where(hot0 | hot1, 1.0, 0.0)

    @pl.when(n == 0)
    def _():
        cnt_ref[...] = jnp.zeros(cnt_ref.shape, F32)

    before = jnp.dot(ltri_ref[...], onehot.astype(BF16), preferred_element_type=F32) + cnt_ref[...]
    r0 = jnp.sum(jnp.where(hot0, before, 0.0), axis=1, keepdims=True)
    r1 = jnp.sum(jnp.where(hot1, before, 0.0), axis=1, keepdims=True)
    cnt_ref[...] = cnt_ref[...] + jnp.sum(onehot, axis=0, keepdims=True)

    rec = jnp.zeros((rows, LANES), F32)
    for ln, val in ((L_E0, e0), (L_E1, e1), (L_W0, w0), (L_W1, w1), (L_R0, r0), (L_R1, r1)):
        rec = jnp.where(lane == ln, val, rec)
    route_ref[...] = rec


def _router_weights(w_rg, b_rg, w_re, b_re):
    w = jnp.zeros((D_MODEL, LANES), F32)
    w = w.at[:, :N_EXPERTS].set(w_re).at[:, L_GROUP:L_GROUP + N_EXPERT_GROUPS].set(w_rg)
    b = jnp.zeros((1, LANES), F32)
    b = b.at[0, :N_EXPERTS].set(b_re).at[0, L_GROUP:L_GROUP + N_EXPERT_GROUPS].set(b_rg)
    w_hi = w.astype(BF16)
    w_lo = (w - w_hi.astype(F32)).astype(BF16)
    return w_hi, w_lo, b


def _merge_route(xa, xb, yaa, yab, yba, ybb, ga, gb, wbp, wba, wo, n2w, wr_hi, wr_lo, br):
    n_first = xa.shape[0] // TM
    t_all = xa.shape[0] + xb.shape[0]
    ltri = jnp.asarray(np.tril(np.ones((TM, TM), np.float32), -1), BF16)
    ta = lambda w: pl.BlockSpec((TM, w), lambda n: (jnp.minimum(n, n_first - 1), 0))
    tb = lambda w: pl.BlockSpec((TM, w), lambda n: (jnp.maximum(n - n_first, 0), 0))
    tile = lambda w: pl.BlockSpec((TM, w), lambda n: (n, 0))
    return pl.pallas_call(
        functools.partial(_merge_route_kernel, n_first),
        grid=(t_all // TM,),
        in_specs=[
            ta(D_MODEL), tb(D_MODEL), ta(POOL_WIDTH), tb(POOL_WIDTH), ta(ATTN_WIDTH), tb(ATTN_WIDTH),
            ta(N_BRANCH * D_MODEL), tb(N_BRANCH * D_MODEL),
            _const_spec(wbp.shape), _const_spec(wba.shape), _const_spec(wo.shape),
            _const_spec((1, D_MODEL)),
            _const_spec(wr_hi.shape), _const_spec(wr_lo.shape), _const_spec(br.shape),
            _const_spec(ltri.shape),
        ],
        out_specs=[tile(D_MODEL), tile(D_MODEL), tile(LANES), _const_spec((1, LANES))],
        out_shape=[
            jax.ShapeDtypeStruct((t_all, D_MODEL), F32),
            jax.ShapeDtypeStruct((t_all, D_MODEL), F32),
            jax.ShapeDtypeStruct((t_all, LANES), F32),
            jax.ShapeDtypeStruct((1, LANES), F32),
        ],
        compiler_params=pltpu.CompilerParams(
            dimension_semantics=("arbitrary",), vmem_limit_bytes=VMEM_LIMIT_BYTES),
        name="merge_route",
    )(xa, xb, yaa, yab, yba, ybb, ga, gb, wbp, wba, wo, n2w.reshape(1, -1), wr_hi, wr_lo, br, ltri)


def _moe_blocks(n_tokens):
    return -(-(2 * n_tokens + N_EXPERTS * (MOE_ROWS - 1)) // MOE_ROWS)


def _plan_kernel(nb_pad, cnt_ref, route_ref, upper_ref, dest_ref, blk_ref, xs_ref, zbuf, sem):
    n = pl.program_id(0)
    cnt = cnt_ref[...]
    nblk = jnp.floor((cnt + (MOE_ROWS - 1)) * (1.0 / MOE_ROWS))
    nb8 = jnp.broadcast_to(nblk, (8, LANES)).astype(BF16)
    bstart = jnp.dot(nb8, upper_ref[...], preferred_element_type=F32)[0:1]
    bend = bstart + nblk
    pstart = bstart * MOE_ROWS

    rec = route_ref[...]
    rows = rec.shape[0]
    lane = lax.broadcasted_iota(jnp.int32, (rows, LANES), 1).astype(F32)
    col = lambda ln: jnp.sum(jnp.where(lane == ln, rec, 0.0), axis=1, keepdims=True)
    look = lambda e: jnp.sum(jnp.where(lane == e, pstart, 0.0), axis=1, keepdims=True)
    d0 = look(col(L_E0)) + col(L_R0)
    d1 = look(col(L_E1)) + col(L_R1)
    dest_ref[...] = jnp.where(lane == 0, d0, jnp.where(lane == 1, d1, 0.0)).astype(jnp.int32)

    @pl.when(n == 0)
    def _():
        b = lax.broadcasted_iota(jnp.int32, (nb_pad, LANES), 0).astype(F32)
        lane_b = lax.broadcasted_iota(jnp.int32, (nb_pad, LANES), 1)
        done = jnp.where((bend <= b) & (lane_b < N_EXPERTS), 1.0, 0.0)
        e_of_b = jnp.minimum(jnp.sum(done, axis=1, keepdims=True), float(N_EXPERTS - 1))
        total = jnp.max(jnp.where(lane_b < N_EXPERTS, bend, 0.0), axis=1, keepdims=True)
        blk_ref[...] = jnp.where(lane_b == 0, e_of_b, jnp.where(lane_b == 1, total, 0.0)).astype(jnp.int32)
        zbuf[...] = jnp.zeros(zbuf.shape, F32)

    slab = zbuf.shape[0]
    n_full, rem = divmod(xs_ref.shape[0], slab)
    full = lambda i: pltpu.make_async_copy(
        zbuf, xs_ref.at[pl.ds(pl.multiple_of(i * slab, MOE_ROWS), slab)], sem)
    tail = pltpu.make_async_copy(zbuf.at[pl.ds(0, max(rem, 1))],
                                 xs_ref.at[pl.ds(n_full * slab, max(rem, 1))], sem)

    @pl.when(n < n_full)
    def _():
        full(n).start()

    if rem:
        @pl.when(n == 0)
        def _():
            tail.start()

    @pl.when(n == pl.num_programs(0) - 1)
    def _():
        for _ in range(n_full):
            full(0).wait()
        if rem:
            tail.wait()


def _plan(cnt, route, n_rows):
    t = route.shape[0]
    nb = n_rows // MOE_ROWS
    nb_pad = -(-nb // 8) * 8
    slab_blocks = -(-nb // (t // TM))
    upper = jnp.asarray(np.triu(np.ones((LANES, LANES), np.float32), 1), BF16)
    return pl.pallas_call(
        functools.partial(_plan_kernel, nb_pad),
        grid=(t // TM,),
        in_specs=[_const_spec((1, LANES)), pl.BlockSpec((TM, LANES), lambda n: (n, 0)),
                  _const_spec((LANES, LANES))],
        out_specs=[pl.BlockSpec((TM, LANES), lambda n: (n, 0)), _const_spec((nb_pad, LANES)),
                   pl.BlockSpec(memory_space=pl.ANY)],
        out_shape=[jax.ShapeDtypeStruct((t, LANES), jnp.int32),
                   jax.ShapeDtypeStruct((nb_pad, LANES), jnp.int32),
                   jax.ShapeDtypeStruct((n_rows, D_MODEL), F32)],
        scratch_shapes=[pltpu.VMEM((slab_blocks * MOE_ROWS, D_MODEL), F32), pltpu.SemaphoreType.DMA],
        compiler_params=pltpu.CompilerParams(
            dimension_semantics=("arbitrary",), vmem_limit_bytes=VMEM_LIMIT_BYTES),
        name="moe_plan",
    )(cnt, route, upper)


def _dispatch_kernel(dest_ref, h_ref, xs_in, xs_out, sem):
    del xs_in
    rows = h_ref.shape[0]

    def issue(t, carry):
        for k in range(2):
            pltpu.make_async_copy(h_ref.at[pl.ds(t, 1)], xs_out.at[pl.ds(dest_ref[k, t], 1)], sem).start()
        return carry

    lax.fori_loop(0, rows, issue, 0, unroll=8)
    for k in range(2):
        pltpu.make_async_copy(h_ref, xs_out.at[pl.ds(0, rows)], sem).wait()


def _dispatch(dest_t, h, xs):
    t = h.shape[0]
    return pl.pallas_call(
        _dispatch_kernel,
        grid=(t // TM,),
        in_specs=[pl.BlockSpec((None, 8, TM), lambda n: (n, 0, 0), memory_space=pltpu.SMEM),
                  pl.BlockSpec((TM, D_MODEL), lambda n: (n, 0)),
                  pl.BlockSpec(memory_space=pl.ANY)],
        out_specs=pl.BlockSpec(memory_space=pl.ANY),
        out_shape=jax.ShapeDtypeStruct(xs.shape, xs.dtype),
        input_output_aliases={2: 0},
        scratch_shapes=[pltpu.SemaphoreType.DMA],
        compiler_params=pltpu.CompilerParams(
            dimension_semantics=("arbitrary",), vmem_limit_bytes=VMEM_LIMIT_BYTES),
        name="moe_dispatch",
    )(dest_t, h, xs)


def _expert_kernel(blk_ref, x_ref, wg_ref, wu_ref, wd_ref, y_ref, wg_bf, wu_bf, wd_bf):
    b = pl.program_id(0)
    used = b < blk_ref[1, 0]

    @pl.when(used)
    def _():
        prev = blk_ref[0, jnp.maximum(b - 1, 0)]

        @pl.when((b == 0) | (blk_ref[0, b] != prev))
        def _():
            wg_bf[...] = wg_ref[...].astype(BF16)
            wu_bf[...] = wu_ref[...].astype(BF16)
            wd_bf[...] = wd_ref[...].astype(BF16)

        x = x_ref[...].astype(BF16)
        gate = jnp.dot(x, wg_bf[...], preferred_element_type=F32)
        up = jnp.dot(x, wu_bf[...], preferred_element_type=F32)
        mid = (jax.nn.silu(gate) * up).astype(BF16)
        y_ref[...] = jnp.dot(mid, wd_bf[...], preferred_element_type=F32)

    @pl.when(jnp.logical_not(used))
    def _():
        y_ref[...] = jnp.zeros(y_ref.shape, F32)


def _experts(blk_t, xs, w_g, w_u, w_d):
    n_rows = xs.shape[0]
    nb = n_rows // MOE_ROWS
    last = lambda b, blk: jnp.minimum(b, blk[1, 0] - 1)
    grid_spec = pltpu.PrefetchScalarGridSpec(
        num_scalar_prefetch=1,
        grid=(nb,),
        in_specs=[
            pl.BlockSpec((MOE_ROWS, D_MODEL), lambda b, blk: (last(b, blk), 0)),
            pl.BlockSpec((None, D_MODEL, D_EXPERT), lambda b, blk: (blk[0, last(b, blk)], 0, 0)),
            pl.BlockSpec((None, D_MODEL, D_EXPERT), lambda b, blk: (blk[0, last(b, blk)], 0, 0)),
            pl.BlockSpec((None, D_EXPERT, D_MODEL), lambda b, blk: (blk[0, last(b, blk)], 0, 0)),
        ],
        out_specs=pl.BlockSpec((MOE_ROWS, D_MODEL), lambda b, blk: (b, 0)),
        scratch_shapes=[pltpu.VMEM((D_MODEL, D_EXPERT), BF16), pltpu.VMEM((D_MODEL, D_EXPERT), BF16),
                        pltpu.VMEM((D_EXPERT, D_MODEL), BF16)],
    )
    return pl.pallas_call(
        _expert_kernel,
        grid_spec=grid_spec,
        out_shape=jax.ShapeDtypeStruct((n_rows, D_MODEL), F32),
        compiler_params=pltpu.CompilerParams(
            dimension_semantics=("arbitrary",), vmem_limit_bytes=VMEM_LIMIT_BYTES),
        name="moe_experts",
    )(blk_t, xs, w_g, w_u, w_d)


def _combine_kernel(n_first, dest_ref, x1_ref, route_ref, ys_ref, oa_ref, ob_ref, gbuf, sem):
    n = pl.program_id(0)
    rows = x1_ref.shape[0]

    def issue(t, carry):
        for k in range(2):
            pltpu.make_async_copy(ys_ref.at[pl.ds(dest_ref[k, t], 1)], gbuf.at[k, pl.ds(t, 1)], sem).start()
        return carry

    lax.fori_loop(0, rows, issue, 0, unroll=8)
    for k in range(2):
        pltpu.make_async_copy(ys_ref.at[pl.ds(0, rows)], gbuf.at[k], sem).wait()
    rec = route_ref[...]
    lane = lax.broadcasted_iota(jnp.int32, rec.shape, 1)
    w0 = jnp.sum(jnp.where(lane == L_W0, rec, 0.0), axis=1, keepdims=True)
    w1 = jnp.sum(jnp.where(lane == L_W1, rec, 0.0), axis=1, keepdims=True)
    res = x1_ref[...] + (gbuf[0] * w0 + gbuf[1] * w1)

    @pl.when(n < n_first)
    def _():
        oa_ref[...] = res

    @pl.when(n >= n_first)
    def _():
        ob_ref[...] = res


def _combine(dest_t, x1, route, ys, t_first):
    t = x1.shape[0]
    n_first = t_first // TM
    return pl.pallas_call(
        functools.partial(_combine_kernel, n_first),
        grid=(t // TM,),
        in_specs=[pl.BlockSpec((None, 8, TM), lambda n: (n, 0, 0), memory_space=pltpu.SMEM),
                  pl.BlockSpec((TM, D_MODEL), lambda n: (n, 0)),
                  pl.BlockSpec((TM, LANES), lambda n: (n, 0)),
                  pl.BlockSpec(memory_space=pl.ANY)],
        out_specs=[pl.BlockSpec((TM, D_MODEL), lambda n: (jnp.minimum(n, n_first - 1), 0)),
                   pl.BlockSpec((TM, D_MODEL), lambda n: (jnp.maximum(n - n_first, 0), 0))],
        out_shape=[jax.ShapeDtypeStruct((t_first, D_MODEL), F32),
                   jax.ShapeDtypeStruct((t - t_first, D_MODEL), F32)],
        scratch_shapes=[pltpu.VMEM((2, TM, D_MODEL), F32), pltpu.SemaphoreType.DMA],
        compiler_params=pltpu.CompilerParams(
            dimension_semantics=("arbitrary",), vmem_limit_bytes=VMEM_LIMIT_BYTES),
        name="moe_combine",
    )(dest_t, x1, route, ys)


def kernel(x_prompt, x_sample, cache_kv_w128, cache_kv_w512, cache_kv_w2048, state_pool, norm1_w, w_in, q_norm_w, k_norm_w, pool_w, pool_scale, w_branch_pool, w_branch_attn, w_out, norm2_w, w_router_group, b_router_group, w_router_expert, b_router_expert, w_expert_gate, w_expert_up, w_expert_down):
    assert x_prompt.shape[0] == 1 and norm1_w.shape[0] == 1
    layer = 0
    s_p = x_prompt.shape[1]
    n_b, n_t, _ = x_sample.shape
    t_s = n_b * n_t
    t_all = s_p + t_s
    caches = (cache_kv_w128[layer], cache_kv_w512[layer], cache_kv_w2048[layer])

    w_bf = w_in[layer].astype(BF16)
    qw_row = jnp.tile(q_norm_w[layer][:, None, :], (1, N_HEADS, 1)).reshape(-1) * (HEAD_DIM ** -0.5)
    kw_row = jnp.tile(k_norm_w[layer][:, None, :], (1, N_HEADS, 1)).reshape(-1)
    poolw_bf = pool_w[layer].astype(BF16)
    wbp, wba, wo = (w.astype(BF16) for w in (w_branch_pool[layer], w_branch_attn[layer], w_out[layer]))
    wr_hi, wr_lo, br = _router_weights(w_router_group[layer], b_router_group[layer],
                                       w_router_expert[layer], b_router_expert[layer])

    ya_p, gates_p, q_r, k_r, v_r, kvt, ut = _proj_prompt(
        x_prompt[0], norm1_w[layer], w_bf, qw_row, kw_row, poolw_bf, pool_scale[layer])
    yb_p = _attn_prompt(q_r, k_r, v_r)

    xs_t = jnp.transpose(x_sample, (1, 0, 2)).reshape(t_s, D_MODEL)
    state_t = jnp.transpose(state_pool[layer], (1, 0, 2))
    ya_s, gates_s, q_s, k_s, v_s, u_s, kt_s, vt_s = _proj_sample(
        xs_t, x_sample.reshape(t_s, D_MODEL), norm1_w[layer], w_bf, qw_row, kw_row, poolw_bf,
        pool_scale[layer], state_t, n_b, n_t)
    split = lambda a: a.reshape(n_b, n_t, N_DIL, N_HEADS, HEAD_DIM)
    per_head = lambda a: jnp.transpose(split(a), (0, 3, 2, 1, 4))
    new_t = jnp.stack([kt_s, vt_s]).reshape(2, N_DIL, N_HEADS, HEAD_DIM, t_s)
    caches_t = [jnp.transpose(c, (0, 2, 3, 4, 1)) for c in caches]
    o_s, *kv_t = _attn_roll(per_head(q_s), per_head(k_s), per_head(v_s), new_t, caches_t)
    yb_s = jnp.transpose(o_s, (2, 0, 1, 3)).reshape(t_s, ATTN_WIDTH).astype(BF16)
    kv_s = [jnp.transpose(c, (0, 4, 1, 2, 3)) for c in kv_t]

    x1, h, route, cnt = _merge_route(x_prompt[0], xs_t, ya_p, ya_s, yb_p, yb_s, gates_p, gates_s,
                                     wbp, wba, wo, norm2_w[layer], wr_hi, wr_lo, br)

    n_rows = _moe_blocks(t_all) * MOE_ROWS
    dest, blk, xs = _plan(cnt, route, n_rows)
    dest_t = jnp.transpose(dest[:, :8].reshape(t_all // TM, TM, 8), (0, 2, 1))
    blk_t = jnp.transpose(blk[:, :2])
    xs = _dispatch(dest_t, h, xs)
    ys = _experts(blk_t, xs, w_expert_gate[layer], w_expert_up[layer], w_expert_down[layer])
    y_p, y_s = _combine(dest_t, x1, route, ys, s_p)

    y_prompt = y_p[None]
    y_sample = jnp.transpose(y_s.reshape(n_t, n_b, D_MODEL), (1, 0, 2))
    pool_prompt = ut[1:][None, None]
    u_new = jnp.transpose(u_s.reshape(n_t, n_b, POOL_WIDTH), (1, 0, 2))
    pool_sample = jnp.concatenate([state_pool[layer][:, n_t:], u_new], axis=1)[None]
    inv = np.argsort(np.asarray(_class_order()))
    kv_nat = jnp.transpose(kvt[inv], (1, 0, 2)).reshape(-1, 2, N_DIL, N_HEADS, HEAD_DIM)
    outs = [y_prompt, y_sample, pool_prompt, pool_sample]
    for g, (win, _) in enumerate(DILATION_PAIRS):
        keep = min(win, s_p)
        outs.append(kv_nat[kv_nat.shape[0] - keep:, :, g][None, None])
        outs.append(kv_s[g][None])
    return tuple(outs)
```

```python
import functools

import numpy as np
import jax
import jax.numpy as jnp
from jax import lax
from jax.experimental import pallas as pl
from jax.experimental.pallas import tpu as pltpu

F32 = jnp.float32
BF16 = jnp.bfloat16

D_MODEL = 1024
PAST_LEN = 8192
POOL_WIDTH = 512
POOL_WINDOWS = (2, 4, 8, 16)
POOL_GROUP_DIM = 128
POOL_STATE = 15
HEAD_DIM = 64
N_HEADS = 8
DILATION_PAIRS = ((128, 1), (512, 4), (2048, 16))
N_DIL = 3
ATTN_WIDTH = 512
QKV_WIDTH = 1536
ROT_DIM = 16
ROPE_THETA = 500000.0
N_BRANCH = 2
IN_COLS = POOL_WIDTH + 3 * QKV_WIDTH + N_BRANCH * D_MODEL
N_EXPERT_GROUPS = 4
EXPERTS_PER_GROUP = 8
N_EXPERTS = 32
D_EXPERT = 512
EPS = 1e-6

LANES = 128
VMEM_LIMIT_BYTES = 56 * 1024 * 1024

C_U = 0
C_Q = POOL_WIDTH
C_K = C_Q + QKV_WIDTH
C_V = C_K + QKV_WIDTH
C_G = C_V + QKV_WIDTH

N_CLASS = 16
TI = 256
ROWS_PER_CLASS = TI // N_CLASS
NEG = -1e30


def _class_order():
    return [(c // 4) + 4 * (c % 4) for c in range(N_CLASS)]


def _const_spec(shape, single_buffer=False):
    nd = len(shape)
    kw = {}
    if single_buffer:
        kw["pipeline_mode"] = pl.Buffered(1)
    return pl.BlockSpec(shape, lambda *_: (0,) * nd, **kw)


def _rms(x, w):
    ms = jnp.mean(x * x, axis=-1, keepdims=True)
    return x * lax.rsqrt(ms + EPS) * w


def _head_norm_rope(y, wrow, seg, segt, cos, sina, sinb):
    sq = (y * y).astype(BF16)
    ssum = jnp.dot(sq, seg, preferred_element_type=F32)
    r = lax.rsqrt(ssum * (1.0 / HEAD_DIM) + EPS)
    r_hi = r.astype(BF16)
    r_lo = (r - r_hi.astype(F32)).astype(BF16)
    rexp = jnp.dot(jnp.concatenate([r_hi, r_lo], axis=1), segt, preferred_element_type=F32)
    yn = y * rexp * wrow
    outs = []
    for j in range(QKV_WIDTH // LANES):
        c = yn[:, j * LANES:(j + 1) * LANES]
        outs.append(c * cos + pltpu.roll(c, LANES - ROT_DIM // 2, 1) * sina
                    + pltpu.roll(c, ROT_DIM // 2, 1) * sinb)
    return jnp.concatenate(outs, axis=1)


def _pool_mix_out(pooled, poolw_ref, pool_scale):
    outs = []
    for g in range(len(POOL_WINDOWS)):
        outs.append(jnp.dot(pooled[g].astype(BF16), poolw_ref[g], preferred_element_type=F32))
    return jnp.concatenate(outs, axis=1) * pool_scale


def _split3(x):
    hi = x.astype(BF16)
    r1 = x - hi.astype(F32)
    mid = r1.astype(BF16)
    lo = (r1 - mid.astype(F32)).astype(BF16)
    return hi, mid, lo


def _proj_prompt_kernel(tail_tiles, x_ref, n1w_ref, w_ref, qw_ref, kw_ref, cosr_ref, sinr_ref, cosn_ref,
                        sinn_ref, rotm_ref, perm_ref, permt_ref, seg_ref, segt_ref, poolw_ref, pools_ref,
                        ya_ref, gates_ref, q_ref, k_ref, v_ref, kvt_ref, ut_ref,
                        uext_ref):
    n = pl.program_id(0)
    xn = _rms(x_ref[...], n1w_ref[...]).astype(BF16)

    u = jnp.dot(xn, w_ref[:, C_U:C_U + POOL_WIDTH], preferred_element_type=F32)

    @pl.when(n == 0)
    def _():
        uext_ref[0:16, :] = jnp.zeros((16, POOL_WIDTH), F32)

    uext_ref[16:16 + TI, :] = u
    pos = n * TI + lax.broadcasted_iota(jnp.int32, (TI, 1), 0)
    pooled = []
    for g, w in enumerate(POOL_WINDOWS):
        lanes = slice(g * POOL_GROUP_DIM, (g + 1) * POOL_GROUP_DIM)
        s = uext_ref[16:16 + TI, lanes]
        for m in range(1, w):
            s = s + uext_ref[16 - m:16 - m + TI, lanes]
        cnt = jnp.minimum(pos + 1, w).astype(F32)
        pooled.append(s / cnt - u[:, lanes])
    ya_ref[...] = _pool_mix_out(pooled, poolw_ref, pools_ref[...]).astype(BF16)
    ut_ref[...] = u[TI - 16:, :]
    uext_ref[0:16, :] = uext_ref[TI:TI + 16, :]

    gl = jnp.dot(xn, w_ref[:, C_G:C_G + N_BRANCH * D_MODEL], preferred_element_type=F32)
    gates_ref[...] = jax.nn.sigmoid(gl).astype(BF16)

    xp = jnp.dot(perm_ref[...], xn, preferred_element_type=F32).astype(BF16)
    cn, sn = cosn_ref[pl.ds(n, 1), :], sinn_ref[pl.ds(n, 1), :]
    cr, sr = cosr_ref[...], sinr_ref[...]
    cos = cn * cr - sn * sr
    sin = sn * cr + cn * sr
    sina, sinb = sin * rotm_ref[0:1, :], sin * rotm_ref[1:2, :]
    seg, segt = seg_ref[...], segt_ref[...]
    shp = (N_CLASS, ROWS_PER_CLASS, QKV_WIDTH)
    yq = jnp.dot(xp, w_ref[:, C_Q:C_Q + QKV_WIDTH], preferred_element_type=F32)
    q_ref[...] = _head_norm_rope(yq, qw_ref[...], seg, segt, cos, sina, sinb).astype(BF16).reshape(shp)
    yk = jnp.dot(xp, w_ref[:, C_K:C_K + QKV_WIDTH], preferred_element_type=F32)
    kr = _head_norm_rope(yk, kw_ref[...], seg, segt, cos, sina, sinb)
    k_ref[...] = kr.astype(BF16).reshape(shp)
    yv = jnp.dot(xp, w_ref[:, C_V:C_V + QKV_WIDTH], preferred_element_type=F32)
    v_ref[...] = yv.astype(BF16).reshape(shp)

    @pl.when(n >= pl.num_programs(0) - tail_tiles)
    def _():
        def natural_t(val):
            nat = sum(jnp.dot(permt_ref[...], part, preferred_element_type=F32) for part in _split3(val))
            return nat.T
        kvt_ref[0:QKV_WIDTH, :] = natural_t(kr)
        kvt_ref[QKV_WIDTH:2 * QKV_WIDTH, :] = natural_t(yv)


def _rope_tables(pos):
    half = ROT_DIM // 2
    inv = ROPE_THETA ** (-(jnp.arange(half, dtype=F32) * 2.0 / ROT_DIM))
    ang = pos.astype(F32)[:, None] * inv[None, :]
    c, s = jnp.cos(ang), jnp.sin(ang)
    n = pos.shape[0]
    ones = jnp.ones((n, HEAD_DIM - ROT_DIM), F32)
    zeros = jnp.zeros((n, HEAD_DIM - ROT_DIM), F32)
    z8 = jnp.zeros((n, half), F32)
    cos64 = jnp.concatenate([c, c, ones], axis=1)
    sina64 = jnp.concatenate([-s, z8, zeros], axis=1)
    sinb64 = jnp.concatenate([z8, s, zeros], axis=1)
    rep = lambda t: jnp.concatenate([t, t], axis=1)
    return rep(cos64), rep(sina64), rep(sinb64)


def _rope_lane_freq():
    half = ROT_DIM // 2
    inv = ROPE_THETA ** (-(np.arange(half, dtype=np.float64) * 2.0 / ROT_DIM))
    d = np.arange(LANES) % HEAD_DIM
    freq = np.where(d < ROT_DIM, inv[d % half], 0.0)
    mask = np.zeros((8, LANES))
    mask[0] = np.where(d < half, -1.0, 0.0)
    mask[1] = np.where((d >= half) & (d < ROT_DIM), 1.0, 0.0)
    return freq, mask


def _seg_mats():
    h = np.arange(QKV_WIDTH) // HEAD_DIM
    seg = (h[:, None] == np.arange(LANES)[None, :]).astype(np.float32)
    return jnp.asarray(seg, BF16), jnp.asarray(np.concatenate([seg.T, seg.T], axis=0), BF16)


def _perm_mat():
    cls = _class_order()
    p = np.zeros((TI, TI), np.float32)
    for c in range(N_CLASS):
        for j in range(ROWS_PER_CLASS):
            p[c * ROWS_PER_CLASS + j, N_CLASS * j + cls[c]] = 1.0
    return p


def _proj_prompt(x, n1w, w_bf, qw_row, kw_row, poolw_bf, pool_scale):
    s = x.shape[0]
    nt = s // TI
    cls = np.asarray(_class_order())
    perm = _perm_mat()
    j = np.arange(ROWS_PER_CLASS)
    inner = (N_CLASS * j[None, :] + cls[:, None]).reshape(-1).astype(np.float64)
    freq, rot_mask = _rope_lane_freq()
    ang_r = inner[:, None] * freq[None, :]
    ang_n = (np.arange(nt, dtype=np.float64) * TI)[:, None] * freq[None, :]
    f32c = lambda a: jnp.asarray(a.astype(np.float32))
    cosr, sinr, cosn, sinn = f32c(np.cos(ang_r)), f32c(np.sin(ang_r)), f32c(np.cos(ang_n)), f32c(np.sin(ang_n))
    rotm = f32c(rot_mask)
    seg, segt = _seg_mats()
    n_i = s // N_CLASS
    tail_tokens = min(DILATION_PAIRS[-1][0], s)
    tail_tiles = tail_tokens // TI
    row = lambda a: a.reshape(1, -1)
    tile = lambda n: (n, 0)
    rblk = lambda n: (0, n, 0)
    tblk = lambda n: (0, jnp.maximum(n - (nt - tail_tiles), 0))
    outs = pl.pallas_call(
        functools.partial(_proj_prompt_kernel, tail_tiles),
        grid=(nt,),
        in_specs=[
            pl.BlockSpec((TI, D_MODEL), tile),
            _const_spec((1, D_MODEL)),
            _const_spec((D_MODEL, IN_COLS), single_buffer=True),
            _const_spec((1, QKV_WIDTH)),
            _const_spec((1, QKV_WIDTH)),
            _const_spec((TI, LANES)), _const_spec((TI, LANES)),
            _const_spec((nt, LANES)), _const_spec((nt, LANES)), _const_spec((8, LANES)),
            _const_spec((TI, TI)), _const_spec((TI, TI)),
            _const_spec((QKV_WIDTH, LANES)),
            _const_spec((2 * LANES, QKV_WIDTH)),
            _const_spec((len(POOL_WINDOWS), POOL_GROUP_DIM, POOL_GROUP_DIM)),
            _const_spec((1, POOL_WIDTH)),
        ],
        out_specs=[
            pl.BlockSpec((TI, POOL_WIDTH), tile),
            pl.BlockSpec((TI, N_BRANCH * D_MODEL), tile),
            pl.BlockSpec((N_CLASS, ROWS_PER_CLASS, QKV_WIDTH), rblk),
            pl.BlockSpec((N_CLASS, ROWS_PER_CLASS, QKV_WIDTH), rblk),
            pl.BlockSpec((N_CLASS, ROWS_PER_CLASS, QKV_WIDTH), rblk),
            pl.BlockSpec((2 * QKV_WIDTH, TI), tblk),
            _const_spec((16, POOL_WIDTH)),
        ],
        out_shape=[
            jax.ShapeDtypeStruct((s, POOL_WIDTH), BF16),
            jax.ShapeDtypeStruct((s, N_BRANCH * D_MODEL), BF16),
            jax.ShapeDtypeStruct((N_CLASS, n_i, QKV_WIDTH), BF16),
            jax.ShapeDtypeStruct((N_CLASS, n_i, QKV_WIDTH), BF16),
            jax.ShapeDtypeStruct((N_CLASS, n_i, QKV_WIDTH), BF16),
            jax.ShapeDtypeStruct((2 * QKV_WIDTH, tail_tokens), F32),
            jax.ShapeDtypeStruct((16, POOL_WIDTH), F32),
        ],
        scratch_shapes=[pltpu.VMEM((16 + TI, POOL_WIDTH), F32)],
        compiler_params=pltpu.CompilerParams(
            dimension_semantics=("arbitrary",), vmem_limit_bytes=VMEM_LIMIT_BYTES),
        name="proj_prompt",
    )(x, row(n1w), w_bf, row(qw_row), row(kw_row), cosr, sinr, cosn, sinn, rotm,
      jnp.asarray(perm, BF16), jnp.asarray(perm.T, BF16), seg, segt, poolw_bf, row(pool_scale))
    return outs


AT_I = 128
N_KEYS_BACK = 128
BLOCKS_PER_TRIP = 2


def _attn_update(q, k, v, bias, m_old, l_old, acc_old):
    nq, nk = q.shape[0], k.shape[0]
    lane = lax.broadcasted_iota(jnp.int32, (nq, LANES), 1)
    first = lane < HEAD_DIM
    bias2 = jnp.concatenate([bias, bias], axis=0)
    ones = jnp.ones((nk, LANES), BF16)
    m_out, l_out, a_out = [], [], []
    for hp in range(N_HEADS // 2):
        sl = slice(hp * LANES, (hp + 1) * LANES)
        qp, kp, vp = q[:, sl], k[:, sl], v[:, sl]
        mo, lo, ao = m_old[:, sl], l_old[:, sl], acc_old[:, sl]
        zero = jnp.zeros_like(qp)
        q2 = jnp.concatenate([jnp.where(first, qp, zero), jnp.where(first, zero, qp)], axis=0)
        s2 = lax.dot_general(q2, kp, (((1,), (1,)), ((), ())), preferred_element_type=F32) + bias2
        mo2 = jnp.concatenate([mo[:, 0:1], mo[:, HEAD_DIM:HEAD_DIM + 1]], axis=0)
        mn2 = jnp.maximum(mo2, jnp.max(s2, axis=1, keepdims=True))
        p2 = jnp.exp(s2 - mn2).astype(BF16)
        pv2 = jnp.dot(p2, jnp.concatenate([vp, ones], axis=1), preferred_element_type=F32)
        pv = jnp.where(first, pv2[:nq, :LANES], pv2[nq:, :LANES])
        lc = jnp.where(first, pv2[:nq, LANES:], pv2[nq:, LANES:])
        mn = jnp.where(first, mn2[:nq], mn2[nq:])
        alpha = jnp.exp(mo - mn)
        m_out.append(mn)
        l_out.append(alpha * lo + lc)
        a_out.append(alpha * ao + pv)
    cat = lambda xs: jnp.concatenate(xs, axis=1)
    return cat(m_out), cat(l_out), cat(a_out)


def _attn_prompt_kernel(q_ref, kc_ref, kp_ref, vc_ref, vp_ref, b0_ref, b1_ref, b2_ref,
                        r0_ref, r1_ref, r2_ref, o_ref,
                        kcat, vcat, acc, m_s, l_s, ost):
    n = pl.program_id(0)
    g = pl.program_id(1)
    i0 = n * AT_I
    kcat[:, 0:AT_I, :] = kp_ref[...]
    kcat[:, AT_I:2 * AT_I, :] = kc_ref[...]
    vcat[:, 0:AT_I, :] = vp_ref[...]
    vcat[:, AT_I:2 * AT_I, :] = vc_ref[...]

    def key_bias(b_ref, r_ref, base):
        return b_ref[...] + jnp.where(base + r_ref[...] >= 0, 0.0, NEG)

    @pl.when(g == 0)
    def _():
        m_s[...] = jnp.full(m_s.shape, NEG, F32)
        l_s[...] = jnp.zeros(l_s.shape, F32)
        acc[...] = jnp.zeros(acc.shape, F32)

        def body(s, carry):
            qo = pl.multiple_of(16 * s, 16)
            ko = pl.multiple_of(AT_I - 16 + 16 * s, 16)
            q16 = [q_ref[c, pl.ds(qo, 16), :] for c in range(N_CLASS)]
            k32 = [kcat[c, pl.ds(ko, 32), :] for c in range(N_CLASS)]
            v32 = [vcat[c, pl.ds(ko, 32), :] for c in range(N_CLASS)]
            cat = lambda f: jnp.concatenate([f(c) for c in range(N_CLASS)], axis=0)
            work = []
            for half in range(2):
                qh = pl.multiple_of(qo + 8 * half, 8)
                q = cat(lambda c: q16[c][8 * half:8 * half + 8])
                k = cat(lambda c: k32[c][8 + 8 * half:24 + 8 * half])
                v = cat(lambda c: v32[c][8 + 8 * half:24 + 8 * half])
                mo = cat(lambda c: m_s[c, pl.ds(qh, 8), :])
                lo = cat(lambda c: l_s[c, pl.ds(qh, 8), :])
                ao = cat(lambda c: acc[c, pl.ds(qh, 8), :])
                bias = key_bias(b0_ref, r0_ref, i0 + 16 * s + 8 * half)
                work.append((qh, (q, k, v, bias, mo, lo, ao)))
            done = [(qh, _attn_update(*args)) for qh, args in work]
            for qh, (mn, ln, an) in done:
                for c in range(N_CLASS):
                    m_s[c, pl.ds(qh, 8), :] = mn[8 * c:8 * c + 8]
                    l_s[c, pl.ds(qh, 8), :] = ln[8 * c:8 * c + 8]
                    acc[c, pl.ds(qh, 8), :] = an[8 * c:8 * c + 8]
            return carry

        lax.fori_loop(0, AT_I // 16, body, 0)

    @pl.when(g == 1)
    def _():
        def load(r4, s):
            qo = pl.multiple_of(32 * s, 32)
            ko = pl.multiple_of(AT_I - 32 + 32 * s, 32)
            cat = lambda f: jnp.concatenate([f(4 * r4 + a) for a in range(4)], axis=0)
            q = cat(lambda c: q_ref[c, pl.ds(qo, 32), :])
            k = cat(lambda c: kcat[c, pl.ds(ko, 64), :])
            v = cat(lambda c: vcat[c, pl.ds(ko, 64), :])
            mo = cat(lambda c: m_s[c, pl.ds(qo, 32), :])
            lo = cat(lambda c: l_s[c, pl.ds(qo, 32), :])
            ao = cat(lambda c: acc[c, pl.ds(qo, 32), :])
            bias = key_bias(b1_ref, r1_ref, i0 + 32 * s)
            return q, k, v, bias, mo, lo, ao

        def store(r4, s, mn, ln, an):
            qo = pl.multiple_of(32 * s, 32)
            for a in range(4):
                c = 4 * r4 + a
                m_s[c, pl.ds(qo, 32), :] = mn[32 * a:32 * a + 32]
                l_s[c, pl.ds(qo, 32), :] = ln[32 * a:32 * a + 32]
                acc[c, pl.ds(qo, 32), :] = an[32 * a:32 * a + 32]

        def body(idx, carry):
            per_class = 4 // BLOCKS_PER_TRIP
            where = [(idx // per_class, BLOCKS_PER_TRIP * (idx % per_class) + j) for j in range(BLOCKS_PER_TRIP)]
            work = [load(r4, s) for r4, s in where]
            done = [_attn_update(*args) for args in work]
            for (r4, s), res in zip(where, done):
                store(r4, s, *res)
            return carry

        lax.fori_loop(0, 16 // BLOCKS_PER_TRIP, body, 0)

    @pl.when(g == 2)
    def _():
        bias = key_bias(b2_ref, r2_ref, i0)

        def body(cc, carry):
            cs = [BLOCKS_PER_TRIP * cc + j for j in range(BLOCKS_PER_TRIP)]
            work = [(q_ref[c], kcat[c], vcat[c], bias, m_s[c], l_s[c], acc[c]) for c in cs]
            done = [_attn_update(*args) for args in work]
            for c, (mn, ln, an) in zip(cs, done):
                m_s[c] = mn
                l_s[c] = ln
                acc[c] = an
            return carry

        lax.fori_loop(0, N_CLASS // BLOCKS_PER_TRIP, body, 0)

        cls = _class_order()
        for c in range(N_CLASS):
            o = acc[c] / l_s[c]
            for jc in range(ATTN_WIDTH // LANES):
                ost[jc, pl.ds(cls[c], AT_I, stride=N_CLASS), :] = o[:, jc * LANES:(jc + 1) * LANES]
        o_ref[...] = jnp.concatenate([ost[jc] for jc in range(ATTN_WIDTH // LANES)], axis=1).astype(BF16)


def _attn_bias_tables():
    cls = np.asarray(_class_order())

    def bias(diff):
        return np.where((diff >= 0) & (diff <= N_KEYS_BACK), 0.0, NEG).astype(np.float32)

    c, r = np.divmod(np.arange(128), 8)
    c2, r2 = np.divmod(np.arange(256), 16)
    d0 = 16 * (r[:, None] - r2[None, :] + 8) + cls[c][:, None] - cls[c2][None, :]
    rel0 = (r2 - 8).astype(np.int32)[None, :]
    a, r = np.divmod(np.arange(128), 32)
    a2, r2 = np.divmod(np.arange(256), 64)
    d1 = 4 * (r[:, None] - r2[None, :] + 32) + a[:, None] - a2[None, :]
    rel1 = (r2 - 32).astype(np.int32)[None, :]
    r = np.arange(128)
    r2 = np.arange(256)
    d2 = r[:, None] - r2[None, :] + 128
    rel2 = (r2 - 128).astype(np.int32)[None, :]
    return (jnp.asarray(bias(d0)), jnp.asarray(bias(d1)), jnp.asarray(bias(d2)),
            jnp.asarray(rel0), jnp.asarray(rel1), jnp.asarray(rel2))


def _attn_prompt(q_r, k_r, v_r):
    n_i = q_r.shape[1]
    s = n_i * N_CLASS
    nsteps = n_i // AT_I
    b0, b1, b2, r0, r1, r2 = _attn_bias_tables()
    blk = (N_CLASS, AT_I, ATTN_WIDTH)
    cur = lambda n, g: (0, n, g)
    prev = lambda n, g: (0, jnp.maximum(n - 1, 0), g)
    return pl.pallas_call(
        _attn_prompt_kernel,
        grid=(nsteps, N_DIL),
        in_specs=[
            pl.BlockSpec(blk, cur), pl.BlockSpec(blk, cur), pl.BlockSpec(blk, prev),
            pl.BlockSpec(blk, cur), pl.BlockSpec(blk, prev),
            _const_spec(b0.shape), _const_spec(b1.shape), _const_spec(b2.shape),
            _const_spec(r0.shape), _const_spec(r1.shape), _const_spec(r2.shape),
        ],
        out_specs=pl.BlockSpec((AT_I * N_CLASS, ATTN_WIDTH), lambda n, g: (n, 0)),
        out_shape=jax.ShapeDtypeStruct((s, ATTN_WIDTH), BF16),
        scratch_shapes=[
            pltpu.VMEM((N_CLASS, 2 * AT_I, ATTN_WIDTH), BF16),
            pltpu.VMEM((N_CLASS, 2 * AT_I, ATTN_WIDTH), BF16),
            pltpu.VMEM((N_CLASS, AT_I, ATTN_WIDTH), F32),
            pltpu.VMEM((N_CLASS, AT_I, ATTN_WIDTH), F32),
            pltpu.VMEM((N_CLASS, AT_I, ATTN_WIDTH), F32),
            pltpu.VMEM((ATTN_WIDTH // LANES, AT_I * N_CLASS, LANES), F32),
        ],
        compiler_params=pltpu.CompilerParams(
            dimension_semantics=("arbitrary", "arbitrary"), vmem_limit_bytes=VMEM_LIMIT_BYTES),
        name="attn_prompt",
    )(q_r, k_r, k_r, v_r, v_r, b0, b1, b2, r0, r1, r2)


def _proj_sample_kernel(n_b, n_t, x_ref, xb_ref, n1w_ref, w_ref, qw_ref, kw_ref, cos_ref, sina_ref, sinb_ref,
                        seg_ref, segt_ref, poolw_ref, pools_ref, state_ref,
                        ya_ref, gates_ref, q_ref, k_ref, v_ref, u_ref, kt_ref, vt_ref):
    xn = _rms(x_ref[...], n1w_ref[...]).astype(BF16)
    u = jnp.dot(xn, w_ref[:, C_U:C_U + POOL_WIDTH], preferred_element_type=F32)
    u_ref[...] = u

    def ext(j, lanes):
        if j < POOL_STATE:
            return state_ref[j, :, lanes]
        return u[(j - POOL_STATE) * n_b:(j - POOL_STATE + 1) * n_b, lanes]

    pooled = []
    for g, w in enumerate(POOL_WINDOWS):
        lanes = slice(g * POOL_GROUP_DIM, (g + 1) * POOL_GROUP_DIM)
        rows = []
        for i in range(n_t):
            s = ext(POOL_STATE + i, lanes)
            for m in range(1, w):
                s = s + ext(POOL_STATE + i - m, lanes)
            cnt = float(min(PAST_LEN + i + 1, w))
            rows.append(s / cnt - ext(POOL_STATE + i, lanes))
        pooled.append(jnp.concatenate(rows, axis=0))
    ya_ref[...] = _pool_mix_out(pooled, poolw_ref, pools_ref[...]).astype(BF16)

    gl = jnp.dot(xn, w_ref[:, C_G:C_G + N_BRANCH * D_MODEL], preferred_element_type=F32)
    gates_ref[...] = jax.nn.sigmoid(gl).astype(BF16)

    xnb = _rms(xb_ref[...], n1w_ref[...]).astype(BF16)
    cos, sina, sinb = cos_ref[...], sina_ref[...], sinb_ref[...]
    seg, segt = seg_ref[...], segt_ref[...]
    yq = jnp.dot(xnb, w_ref[:, C_Q:C_Q + QKV_WIDTH], preferred_element_type=F32)
    q_ref[...] = _head_norm_rope(yq, qw_ref[...], seg, segt, cos, sina, sinb)
    yk = jnp.dot(xnb, w_ref[:, C_K:C_K + QKV_WIDTH], preferred_element_type=F32)
    kr = _head_norm_rope(yk, kw_ref[...], seg, segt, cos, sina, sinb)
    k_ref[...] = kr
    kt_ref[...] = kr.T
    yv = jnp.dot(xnb, w_ref[:, C_V:C_V + QKV_WIDTH], preferred_element_type=F32)
    v_ref[...] = yv
    vt_ref[...] = yv.T


def _proj_sample(x, xb, n1w, w_bf, qw_row, kw_row, poolw_bf, pool_scale, state_t, n_b, n_t):
    rows = n_b * n_t
    pos = PAST_LEN + jnp.arange(rows, dtype=jnp.int32) % n_t
    cos, sina, sinb = _rope_tables(pos)
    seg, segt = _seg_mats()
    row = lambda a: a.reshape(1, -1)
    args = (x, xb, row(n1w), w_bf, row(qw_row), row(kw_row), cos, sina, sinb, seg, segt,
            poolw_bf, row(pool_scale), state_t)
    f32_qkv = jax.ShapeDtypeStruct((rows, QKV_WIDTH), F32)
    out_shape = [
        jax.ShapeDtypeStruct((rows, POOL_WIDTH), BF16),
        jax.ShapeDtypeStruct((rows, N_BRANCH * D_MODEL), BF16),
        f32_qkv, f32_qkv, f32_qkv,
        jax.ShapeDtypeStruct((rows, POOL_WIDTH), F32),
        jax.ShapeDtypeStruct((QKV_WIDTH, rows), F32),
        jax.ShapeDtypeStruct((QKV_WIDTH, rows), F32),
    ]
    return pl.pallas_call(
        functools.partial(_proj_sample_kernel, n_b, n_t),
        grid=(1,),
        in_specs=[_const_spec(a.shape) for a in args],
        out_specs=[_const_spec(o.shape) for o in out_shape],
        out_shape=out_shape,
        compiler_params=pltpu.CompilerParams(
            dimension_semantics=("arbitrary",), vmem_limit_bytes=VMEM_LIMIT_BYTES),
        name="proj_sample",
    )(*args)


HEADS_PER_STEP = 4


def _attn_roll_kernel(n_t, q_ref, kn_ref, vn_ref, newt_ref, c0_ref, c1_ref, c2_ref,
                      b0_ref, b1_ref, b2_ref, bn_ref, o_ref, o0_ref, o1_ref, o2_ref):
    caches = (c0_ref, c1_ref, c2_ref)
    outs = (o0_ref, o1_ref, o2_ref)
    biases = (b0_ref, b1_ref, b2_ref)
    nt_contract = (((1,), (1,)), ((), ()))
    col = pl.program_id(1) * n_t
    new_tile = pl.multiple_of((col // LANES) * LANES, LANES)
    new_shift = (LANES - n_t) - col % LANES
    for hh in range(HEADS_PER_STEP):
        scores, values = [], []
        for g in range(N_DIL):
            q = q_ref[hh, g].astype(BF16)
            kc = caches[g][0, hh].astype(BF16)
            scores.append(jnp.dot(q, kc, preferred_element_type=F32) + biases[g][...])
            kn = kn_ref[hh, g].astype(BF16)
            scores.append(lax.dot_general(q, kn, nt_contract, preferred_element_type=F32) + bn_ref[g])
            values.append(caches[g][1, hh].astype(BF16))
            values.append(vn_ref[hh, g].astype(BF16))
        m = functools.reduce(jnp.maximum, [jnp.max(s, axis=1, keepdims=True) for s in scores])
        l = jnp.zeros_like(m)
        o = jnp.zeros((n_t, HEAD_DIM), F32)
        for idx, (s, v) in enumerate(zip(scores, values)):
            p = jnp.exp(s - m)
            l = l + jnp.sum(p, axis=1, keepdims=True)
            pb = p.astype(BF16)
            if idx % 2 == 0:
                o = o + lax.dot_general(pb, v, nt_contract, preferred_element_type=F32)
            else:
                o = o + jnp.dot(pb, v, preferred_element_type=F32)
        o_ref[hh] = o / l

        for g in range(N_DIL):
            length = caches[g].shape[-1]
            lane = lax.broadcasted_iota(jnp.int32, (HEAD_DIM, LANES), 1)
            for kv in range(2):
                rolled = pltpu.roll(caches[g][kv, hh], length - n_t, 1)
                new = pltpu.roll(newt_ref[kv, g, hh, :, pl.ds(new_tile, LANES)], new_shift, 1)
                if length > LANES:
                    outs[g][kv, hh, :, 0:length - LANES] = rolled[:, 0:length - LANES]
                outs[g][kv, hh, :, length - LANES:length] = jnp.where(
                    lane >= LANES - n_t, new, rolled[:, length - LANES:length])


def _attn_roll_bias(n_t):
    i = np.arange(n_t)
    bias, bias_new = [], []
    for win, dil in DILATION_PAIRS:
        length = min(win, PAST_LEN)
        delta = length + i[:, None] - np.arange(length)[None, :]
        ok = (delta % dil == 0) & (delta // dil >= 1) & (delta // dil <= win // dil)
        bias.append(jnp.asarray(np.where(ok, 0.0, NEG).astype(np.float32)))
        dn = i[:, None] - i[None, :]
        okn = (dn >= 0) & (dn % dil == 0) & (dn // dil <= win // dil)
        bias_new.append(np.where(okn, 0.0, NEG).astype(np.float32))
    return bias, jnp.asarray(np.stack(bias_new))


def _attn_roll(q_h, kn_h, vn_h, new_t, caches_t):
    n_b, _, _, n_t, _ = q_h.shape
    bias, bias_new = _attn_roll_bias(n_t)
    hs = HEADS_PER_STEP
    qspec = pl.BlockSpec((None, hs, N_DIL, n_t, HEAD_DIM), lambda h, b: (b, h, 0, 0, 0))
    cspec = lambda c: pl.BlockSpec((None, 2, hs, HEAD_DIM, c.shape[-1]), lambda h, b: (b, 0, h, 0, 0))
    return pl.pallas_call(
        functools.partial(_attn_roll_kernel, n_t),
        grid=(N_HEADS // hs, n_b),
        in_specs=[qspec, qspec, qspec,
                  pl.BlockSpec((2, N_DIL, hs, HEAD_DIM, n_t * n_b), lambda h, b: (0, 0, h, 0, 0))]
                 + [cspec(c) for c in caches_t]
                 + [_const_spec(b.shape) for b in bias] + [_const_spec(bias_new.shape)],
        out_specs=[pl.BlockSpec((None, hs, n_t, HEAD_DIM), lambda h, b: (b, h, 0, 0))]
                  + [cspec(c) for c in caches_t],
        out_shape=[jax.ShapeDtypeStruct((n_b, N_HEADS, n_t, HEAD_DIM), F32)]
                  + [jax.ShapeDtypeStruct(c.shape, c.dtype) for c in caches_t],
        compiler_params=pltpu.CompilerParams(
            dimension_semantics=("arbitrary", "arbitrary"), vmem_limit_bytes=VMEM_LIMIT_BYTES),
        name="attn_roll",
    )(q_h, kn_h, vn_h, new_t, *caches_t, *bias, bias_new)


TM = 512
MOE_ROWS = 256
L_E0, L_E1, L_W0, L_W1, L_R0, L_R1 = 0, 1, 2, 3, 4, 5
L_GROUP = N_EXPERTS


def _lane_min_index(mask, lane):
    return jnp.min(jnp.where(mask, lane, float(LANES)), axis=1, keepdims=True)


def _merge_route_kernel(n_first, xa_ref, xb_ref, yaa_ref, yab_ref, yba_ref, ybb_ref, ga_ref, gb_ref,
                        wbp_ref, wba_ref, wo_ref, n2w_ref, wrh_ref, wrl_ref, br_ref, ltri_ref,
                        x1_ref, h_ref, route_ref, cnt_ref):
    n = pl.program_id(0)
    first = n < n_first
    pick = lambda a, b: jnp.where(first, a[...], b[...])
    pa = jnp.dot(pick(yaa_ref, yab_ref), wbp_ref[...], preferred_element_type=F32)
    pb = jnp.dot(pick(yba_ref, ybb_ref), wba_ref[...], preferred_element_type=F32)
    gts = pick(ga_ref, gb_ref)
    merged = gts[:, :D_MODEL].astype(F32) * pa + gts[:, D_MODEL:].astype(F32) * pb
    x1 = pick(xa_ref, xb_ref) + jnp.dot(merged.astype(BF16), wo_ref[...], preferred_element_type=F32)
    x1_ref[...] = x1
    h = _rms(x1, n2w_ref[...])
    h_ref[...] = h

    h_hi = h.astype(BF16)
    h_lo = (h - h_hi.astype(F32)).astype(BF16)
    logits = (jnp.dot(h_hi, wrh_ref[...], preferred_element_type=F32)
              + jnp.dot(h_hi, wrl_ref[...], preferred_element_type=F32)
              + jnp.dot(h_lo, wrh_ref[...], preferred_element_type=F32)) + br_ref[...]
    rows = logits.shape[0]
    lane = lax.broadcasted_iota(jnp.int32, (rows, LANES), 1).astype(F32)
    is_group = (lane >= L_GROUP) & (lane < L_GROUP + N_EXPERT_GROUPS)
    gl = jnp.where(is_group, logits, NEG)
    gmax = jnp.max(gl, axis=1, keepdims=True)
    g_w = 1.0 / jnp.sum(jnp.exp(gl - gmax), axis=1, keepdims=True)
    g_sel = _lane_min_index(gl == gmax, lane) - L_GROUP
    in_group = (lane >= g_sel * EXPERTS_PER_GROUP) & (lane < (g_sel + 1.0) * EXPERTS_PER_GROUP)
    el = jnp.where(in_group, logits, NEG)
    v0 = jnp.max(el, axis=1, keepdims=True)
    e0 = _lane_min_index(el == v0, lane)
    el2 = jnp.where(lane == e0, NEG, el)
    v1 = jnp.max(el2, axis=1, keepdims=True)
    e1 = _lane_min_index(el2 == v1, lane)
    t = jnp.exp(v1 - v0)
    w0 = g_w / (1.0 + t)
    w1 = g_w * t / (1.0 + t)

    hot0 = lane == e0
    hot1 = lane == e1
    onehot = jnp.where(hot0 | hot1, 1.0, 0.0)

    @pl.when(n == 0)
    def _():
        cnt_ref[...] = jnp.zeros(cnt_ref.shape, F32)

    before = jnp.dot(ltri_ref[...], onehot.astype(BF16), preferred_element_type=F32) + cnt_ref[...]
    r0 = jnp.sum(jnp.where(hot0, before, 0.0), axis=1, keepdims=True)
    r1 = jnp.sum(jnp.where(hot1, before, 0.0), axis=1, keepdims=True)
    cnt_ref[...] = cnt_ref[...] + jnp.sum(onehot, axis=0, keepdims=True)

    rec = jnp.zeros((rows, LANES), F32)
    for ln, val in ((L_E0, e0), (L_E1, e1), (L_W0, w0), (L_W1, w1), (L_R0, r0), (L_R1, r1)):
        rec = jnp.where(lane == ln, val, rec)
    route_ref[...] = rec


def _router_weights(w_rg, b_rg, w_re, b_re):
    w = jnp.zeros((D_MODEL, LANES), F32)
    w = w.at[:, :N_EXPERTS].set(w_re).at[:, L_GROUP:L_GROUP + N_EXPERT_GROUPS].set(w_rg)
    b = jnp.zeros((1, LANES), F32)
    b = b.at[0, :N_EXPERTS].set(b_re).at[0, L_GROUP:L_GROUP + N_EXPERT_GROUPS].set(b_rg)
    w_hi = w.astype(BF16)
    w_lo = (w - w_hi.astype(F32)).astype(BF16)
    return w_hi, w_lo, b


def _merge_route(xa, xb, yaa, yab, yba, ybb, ga, gb, wbp, wba, wo, n2w, wr_hi, wr_lo, br):
    n_first = xa.shape[0] // TM
    t_all = xa.shape[0] + xb.shape[0]
    ltri = jnp.asarray(np.tril(np.ones((TM, TM), np.float32), -1), BF16)
    ta = lambda w: pl.BlockSpec((TM, w), lambda n: (jnp.minimum(n, n_first - 1), 0))
    tb = lambda w: pl.BlockSpec((TM, w), lambda n: (jnp.maximum(n - n_first, 0), 0))
    tile = lambda w: pl.BlockSpec((TM, w), lambda n: (n, 0))
    return pl.pallas_call(
        functools.partial(_merge_route_kernel, n_first),
        grid=(t_all // TM,),
        in_specs=[
            ta(D_MODEL), tb(D_MODEL), ta(POOL_WIDTH), tb(POOL_WIDTH), ta(ATTN_WIDTH), tb(ATTN_WIDTH),
            ta(N_BRANCH * D_MODEL), tb(N_BRANCH * D_MODEL),
            _const_spec(wbp.shape), _const_spec(wba.shape), _const_spec(wo.shape),
            _const_spec((1, D_MODEL)),
            _const_spec(wr_hi.shape), _const_spec(wr_lo.shape), _const_spec(br.shape),
            _const_spec(ltri.shape),
        ],
        out_specs=[tile(D_MODEL), tile(D_MODEL), tile(LANES), _const_spec((1, LANES))],
        out_shape=[
            jax.ShapeDtypeStruct((t_all, D_MODEL), F32),
            jax.ShapeDtypeStruct((t_all, D_MODEL), F32),
            jax.ShapeDtypeStruct((t_all, LANES), F32),
            jax.ShapeDtypeStruct((1, LANES), F32),
        ],
        compiler_params=pltpu.CompilerParams(
            dimension_semantics=("arbitrary",), vmem_limit_bytes=VMEM_LIMIT_BYTES),
        name="merge_route",
    )(xa, xb, yaa, yab, yba, ybb, ga, gb, wbp, wba, wo, n2w.reshape(1, -1), wr_hi, wr_lo, br, ltri)


def _moe_blocks(n_tokens):
    return -(-(2 * n_tokens + N_EXPERTS * (MOE_ROWS - 1)) // MOE_ROWS)


def _plan_kernel(nb_pad, cnt_ref, route_ref, upper_ref, dest_ref, blk_ref, xs_ref, zbuf, sem):
    n = pl.program_id(0)
    cnt = cnt_ref[...]
    nblk = jnp.floor((cnt + (MOE_ROWS - 1)) * (1.0 / MOE_ROWS))
    nb8 = jnp.broadcast_to(nblk, (8, LANES)).astype(BF16)
    bstart = jnp.dot(nb8, upper_ref[...], preferred_element_type=F32)[0:1]
    bend = bstart + nblk
    pstart = bstart * MOE_ROWS

    rec = route_ref[...]
    rows = rec.shape[0]
    lane = lax.broadcasted_iota(jnp.int32, (rows, LANES), 1).astype(F32)
    col = lambda ln: jnp.sum(jnp.where(lane == ln, rec, 0.0), axis=1, keepdims=True)
    look = lambda e: jnp.sum(jnp.where(lane == e, pstart, 0.0), axis=1, keepdims=True)
    d0 = look(col(L_E0)) + col(L_R0)
    d1 = look(col(L_E1)) + col(L_R1)
    dest_ref[...] = jnp.where(lane == 0, d0, jnp.where(lane == 1, d1, 0.0)).astype(jnp.int32)

    @pl.when(n == 0)
    def _():
        b = lax.broadcasted_iota(jnp.int32, (nb_pad, LANES), 0).astype(F32)
        lane_b = lax.broadcasted_iota(jnp.int32, (nb_pad, LANES), 1)
        done = jnp.where((bend <= b) & (lane_b < N_EXPERTS), 1.0, 0.0)
        e_of_b = jnp.minimum(jnp.sum(done, axis=1, keepdims=True), float(N_EXPERTS - 1))
        total = jnp.max(jnp.where(lane_b < N_EXPERTS, bend, 0.0), axis=1, keepdims=True)
        blk_ref[...] = jnp.where(lane_b == 0, e_of_b, jnp.where(lane_b == 1, total, 0.0)).astype(jnp.int32)
        zbuf[...] = jnp.zeros(zbuf.shape, F32)

    slab = zbuf.shape[0]
    n_full, rem = divmod(xs_ref.shape[0], slab)
    full = lambda i: pltpu.make_async_copy(
        zbuf, xs_ref.at[pl.ds(pl.multiple_of(i * slab, MOE_ROWS), slab)], sem)
    tail = pltpu.make_async_copy(zbuf.at[pl.ds(0, max(rem, 1))],
                                 xs_ref.at[pl.ds(n_full * slab, max(rem, 1))], sem)

    @pl.when(n < n_full)
    def _():
        full(n).start()

    if rem:
        @pl.when(n == 0)
        def _():
            tail.start()

    @pl.when(n == pl.num_programs(0) - 1)
    def _():
        for _ in range(n_full):
            full(0).wait()
        if rem:
            tail.wait()


def _plan(cnt, route, n_rows):
    t = route.shape[0]
    nb = n_rows // MOE_ROWS
    nb_pad = -(-nb // 8) * 8
    slab_blocks = -(-nb // (t // TM))
    upper = jnp.asarray(np.triu(np.ones((LANES, LANES), np.float32), 1), BF16)
    return pl.pallas_call(
        functools.partial(_plan_kernel, nb_pad),
        grid=(t // TM,),
        in_specs=[_const_spec((1, LANES)), pl.BlockSpec((TM, LANES), lambda n: (n, 0)),
                  _const_spec((LANES, LANES))],
        out_specs=[pl.BlockSpec((TM, LANES), lambda n: (n, 0)), _const_spec((nb_pad, LANES)),
                   pl.BlockSpec(memory_space=pl.ANY)],
        out_shape=[jax.ShapeDtypeStruct((t, LANES), jnp.int32),
                   jax.ShapeDtypeStruct((nb_pad, LANES), jnp.int32),
                   jax.ShapeDtypeStruct((n_rows, D_MODEL), F32)],
        scratch_shapes=[pltpu.VMEM((slab_blocks * MOE_ROWS, D_MODEL), F32), pltpu.SemaphoreType.DMA],
        compiler_params=pltpu.CompilerParams(
            dimension_semantics=("arbitrary",), vmem_limit_bytes=VMEM_LIMIT_BYTES),
        name="moe_plan",
    )(cnt, route, upper)


def _dispatch_kernel(dest_ref, h_ref, xs_in, xs_out, sem):
    del xs_in
    rows = h_ref.shape[0]

    def issue(t, carry):
        for k in range(2):
            pltpu.make_async_copy(h_ref.at[pl.ds(t, 1)], xs_out.at[pl.ds(dest_ref[k, t], 1)], sem).start(priority=k)
        return carry

    lax.fori_loop(0, rows, issue, 0, unroll=8)
    for k in range(2):
        pltpu.make_async_copy(h_ref, xs_out.at[pl.ds(0, rows)], sem).wait()


def _dispatch(dest_t, h, xs):
    t = h.shape[0]
    return pl.pallas_call(
        _dispatch_kernel,
        grid=(t // TM,),
        in_specs=[pl.BlockSpec((None, 8, TM), lambda n: (n, 0, 0), memory_space=pltpu.SMEM),
                  pl.BlockSpec((TM, D_MODEL), lambda n: (n, 0)),
                  pl.BlockSpec(memory_space=pl.ANY)],
        out_specs=pl.BlockSpec(memory_space=pl.ANY),
        out_shape=jax.ShapeDtypeStruct(xs.shape, xs.dtype),
        input_output_aliases={2: 0},
        scratch_shapes=[pltpu.SemaphoreType.DMA],
        compiler_params=pltpu.CompilerParams(
            dimension_semantics=("arbitrary",), vmem_limit_bytes=VMEM_LIMIT_BYTES),
        name="moe_dispatch",
    )(dest_t, h, xs)


def _expert_kernel(blk_ref, x_ref, wg_ref, wu_ref, wd_ref, y_ref, wg_bf, wu_bf, wd_bf):
    b = pl.program_id(0)
    used = b < blk_ref[1, 0]

    @pl.when(used)
    def _():
        prev = blk_ref[0, jnp.maximum(b - 1, 0)]

        @pl.when((b == 0) | (blk_ref[0, b] != prev))
        def _():
            wg_bf[...] = wg_ref[...].astype(BF16)
            wu_bf[...] = wu_ref[...].astype(BF16)
            wd_bf[...] = wd_ref[...].astype(BF16)

        x = x_ref[...].astype(BF16)
        gate = jnp.dot(x, wg_bf[...], preferred_element_type=F32)
        up = jnp.dot(x, wu_bf[...], preferred_element_type=F32)
        mid = (jax.nn.silu(gate) * up).astype(BF16)
        y_ref[...] = jnp.dot(mid, wd_bf[...], preferred_element_type=F32)

    @pl.when(jnp.logical_not(used))
    def _():
        y_ref[...] = jnp.zeros(y_ref.shape, F32)


def _experts(blk_t, xs, w_g, w_u, w_d):
    n_rows = xs.shape[0]
    nb = n_rows // MOE_ROWS
    last = lambda b, blk: jnp.minimum(b, blk[1, 0] - 1)
    grid_spec = pltpu.PrefetchScalarGridSpec(
        num_scalar_prefetch=1,
        grid=(nb,),
        in_specs=[
            pl.BlockSpec((MOE_ROWS, D_MODEL), lambda b, blk: (last(b, blk), 0)),
            pl.BlockSpec((None, D_MODEL, D_EXPERT), lambda b, blk: (blk[0, last(b, blk)], 0, 0)),
            pl.BlockSpec((None, D_MODEL, D_EXPERT), lambda b, blk: (blk[0, last(b, blk)], 0, 0)),
            pl.BlockSpec((None, D_EXPERT, D_MODEL), lambda b, blk: (blk[0, last(b, blk)], 0, 0)),
        ],
        out_specs=pl.BlockSpec((MOE_ROWS, D_MODEL), lambda b, blk: (b, 0)),
        scratch_shapes=[pltpu.VMEM((D_MODEL, D_EXPERT), BF16), pltpu.VMEM((D_MODEL, D_EXPERT), BF16),
                        pltpu.VMEM((D_EXPERT, D_MODEL), BF16)],
    )
    return pl.pallas_call(
        _expert_kernel,
        grid_spec=grid_spec,
        out_shape=jax.ShapeDtypeStruct((n_rows, D_MODEL), F32),
        compiler_params=pltpu.CompilerParams(
            dimension_semantics=("arbitrary",), vmem_limit_bytes=VMEM_LIMIT_BYTES),
        name="moe_experts",
    )(blk_t, xs, w_g, w_u, w_d)


def _combine_kernel(n_first, dest_ref, x1_ref, route_ref, ys_ref, oa_ref, ob_ref, gbuf, sem):
    n = pl.program_id(0)
    rows = x1_ref.shape[0]

    def issue(t, carry):
        for k in range(2):
            pltpu.make_async_copy(ys_ref.at[pl.ds(dest_ref[k, t], 1)], gbuf.at[k, pl.ds(t, 1)], sem).start(priority=k)
        return carry

    lax.fori_loop(0, rows, issue, 0, unroll=8)
    for k in range(2):
        pltpu.make_async_copy(ys_ref.at[pl.ds(0, rows)], gbuf.at[k], sem).wait()
    rec = route_ref[...]
    lane = lax.broadcasted_iota(jnp.int32, rec.shape, 1)
    w0 = jnp.sum(jnp.where(lane == L_W0, rec, 0.0), axis=1, keepdims=True)
    w1 = jnp.sum(jnp.where(lane == L_W1, rec, 0.0), axis=1, keepdims=True)
    res = x1_ref[...] + (gbuf[0] * w0 + gbuf[1] * w1)

    @pl.when(n < n_first)
    def _():
        oa_ref[...] = res

    @pl.when(n >= n_first)
    def _():
        ob_ref[...] = res


def _combine(dest_t, x1, route, ys, t_first):
    t = x1.shape[0]
    n_first = t_first // TM
    return pl.pallas_call(
        functools.partial(_combine_kernel, n_first),
        grid=(t // TM,),
        in_specs=[pl.BlockSpec((None, 8, TM), lambda n: (n, 0, 0), memory_space=pltpu.SMEM),
                  pl.BlockSpec((TM, D_MODEL), lambda n: (n, 0)),
                  pl.BlockSpec((TM, LANES), lambda n: (n, 0)),
                  pl.BlockSpec(memory_space=pl.ANY)],
        out_specs=[pl.BlockSpec((TM, D_MODEL), lambda n: (jnp.minimum(n, n_first - 1), 0)),
                   pl.BlockSpec((TM, D_MODEL), lambda n: (jnp.maximum(n - n_first, 0), 0))],
        out_shape=[jax.ShapeDtypeStruct((t_first, D_MODEL), F32),
                   jax.ShapeDtypeStruct((t - t_first, D_MODEL), F32)],
        scratch_shapes=[pltpu.VMEM((2, TM, D_MODEL), F32), pltpu.SemaphoreType.DMA],
        compiler_params=pltpu.CompilerParams(
            dimension_semantics=("arbitrary",), vmem_limit_bytes=VMEM_LIMIT_BYTES),
        name="moe_combine",
    )(dest_t, x1, route, ys)


def kernel(x_prompt, x_sample, cache_kv_w128, cache_kv_w512, cache_kv_w2048, state_pool, norm1_w, w_in, q_norm_w, k_norm_w, pool_w, pool_scale, w_branch_pool, w_branch_attn, w_out, norm2_w, w_router_group, b_router_group, w_router_expert, b_router_expert, w_expert_gate, w_expert_up, w_expert_down):
    assert x_prompt.shape[0] == 1 and norm1_w.shape[0] == 1
    layer = 0
    s_p = x_prompt.shape[1]
    n_b, n_t, _ = x_sample.shape
    t_s = n_b * n_t
    t_all = s_p + t_s
    caches = (cache_kv_w128[layer], cache_kv_w512[layer], cache_kv_w2048[layer])

    w_bf = w_in[layer].astype(BF16)
    qw_row = jnp.tile(q_norm_w[layer][:, None, :], (1, N_HEADS, 1)).reshape(-1) * (HEAD_DIM ** -0.5)
    kw_row = jnp.tile(k_norm_w[layer][:, None, :], (1, N_HEADS, 1)).reshape(-1)
    poolw_bf = pool_w[layer].astype(BF16)
    wbp, wba, wo = (w.astype(BF16) for w in (w_branch_pool[layer], w_branch_attn[layer], w_out[layer]))
    wr_hi, wr_lo, br = _router_weights(w_router_group[layer], b_router_group[layer],
                                       w_router_expert[layer], b_router_expert[layer])

    ya_p, gates_p, q_r, k_r, v_r, kvt, ut = _proj_prompt(
        x_prompt[0], norm1_w[layer], w_bf, qw_row, kw_row, poolw_bf, pool_scale[layer])
    yb_p = _attn_prompt(q_r, k_r, v_r)

    xs_t = jnp.transpose(x_sample, (1, 0, 2)).reshape(t_s, D_MODEL)
    state_t = jnp.transpose(state_pool[layer], (1, 0, 2))
    ya_s, gates_s, q_s, k_s, v_s, u_s, kt_s, vt_s = _proj_sample(
        xs_t, x_sample.reshape(t_s, D_MODEL), norm1_w[layer], w_bf, qw_row, kw_row, poolw_bf,
        pool_scale[layer], state_t, n_b, n_t)
    split = lambda a: a.reshape(n_b, n_t, N_DIL, N_HEADS, HEAD_DIM)
    per_head = lambda a: jnp.transpose(split(a), (0, 3, 2, 1, 4))
    new_t = jnp.stack([kt_s, vt_s]).reshape(2, N_DIL, N_HEADS, HEAD_DIM, t_s)
    caches_t = [jnp.transpose(c, (0, 2, 3, 4, 1)) for c in caches]
    o_s, *kv_t = _attn_roll(per_head(q_s), per_head(k_s), per_head(v_s), new_t, caches_t)
    yb_s = jnp.transpose(o_s, (2, 0, 1, 3)).reshape(t_s, ATTN_WIDTH).astype(BF16)
    kv_s = [jnp.transpose(c, (0, 4, 1, 2, 3)) for c in kv_t]

    x1, h, route, cnt = _merge_route(x_prompt[0], xs_t, ya_p, ya_s, yb_p, yb_s, gates_p, gates_s,
                                     wbp, wba, wo, norm2_w[layer], wr_hi, wr_lo, br)

    n_rows = _moe_blocks(t_all) * MOE_ROWS
    dest, blk, xs = _plan(cnt, route, n_rows)
    dest_t = jnp.transpose(dest[:, :8].reshape(t_all // TM, TM, 8), (0, 2, 1))
    blk_t = jnp.transpose(blk[:, :2])
    xs = _dispatch(dest_t, h, xs)
    ys = _experts(blk_t, xs, w_expert_gate[layer], w_expert_up[layer], w_expert_down[layer])
    y_p, y_s = _combine(dest_t, x1, route, ys, s_p)

    y_prompt = y_p[None]
    y_sample = jnp.transpose(y_s.reshape(n_t, n_b, D_MODEL), (1, 0, 2))
    pool_prompt = ut[1:][None, None]
    u_new = jnp.transpose(u_s.reshape(n_t, n_b, POOL_WIDTH), (1, 0, 2))
    pool_sample = jnp.concatenate([state_pool[layer][:, n_t:], u_new], axis=1)[None]
    kv_t = kvt.reshape(2, N_DIL, N_HEADS, HEAD_DIM, kvt.shape[1])
    outs = [y_prompt, y_sample, pool_prompt, pool_sample]
    for g, (win, _) in enumerate(DILATION_PAIRS):
        keep = min(win, s_p)
        kv_g = kv_t[:, g, :, :, kv_t.shape[-1] - keep:]
        outs.append(jnp.transpose(kv_g, (3, 0, 1, 2))[None, None])
        outs.append(kv_s[g][None])
    return tuple(outs)
```

```python
import functools

import numpy as np
import jax
import jax.numpy as jnp
from jax import lax
from jax.experimental import pallas as pl
from jax.experimental.pallas import tpu as pltpu

F32 = jnp.float32
BF16 = jnp.bfloat16

D_MODEL = 1024
PAST_LEN = 8192
POOL_WIDTH = 512
POOL_WINDOWS = (2, 4, 8, 16)
POOL_GROUP_DIM = 128
POOL_STATE = 15
HEAD_DIM = 64
N_HEADS = 8
DILATION_PAIRS = ((128, 1), (512, 4), (2048, 16))
N_DIL = 3
ATTN_WIDTH = 512
QKV_WIDTH = 1536
ROT_DIM = 16
ROPE_THETA = 500000.0
N_BRANCH = 2
IN_COLS = POOL_WIDTH + 3 * QKV_WIDTH + N_BRANCH * D_MODEL
N_EXPERT_GROUPS = 4
EXPERTS_PER_GROUP = 8
N_EXPERTS = 32
D_EXPERT = 512
EPS = 1e-6

LANES = 128
VMEM_LIMIT_BYTES = 56 * 1024 * 1024

C_U = 0
C_Q = POOL_WIDTH
C_K = C_Q + QKV_WIDTH
C_V = C_K + QKV_WIDTH
C_G = C_V + QKV_WIDTH

N_CLASS = 16
TI = 256
ROWS_PER_CLASS = TI // N_CLASS
NEG = -1e30


def _class_order():
    return [(c // 4) + 4 * (c % 4) for c in range(N_CLASS)]


def _const_spec(shape, single_buffer=False):
    nd = len(shape)
    kw = {}
    if single_buffer:
        kw["pipeline_mode"] = pl.Buffered(1)
    return pl.BlockSpec(shape, lambda *_: (0,) * nd, **kw)


def _rms(x, w):
    ms = jnp.mean(x * x, axis=-1, keepdims=True)
    return x * lax.rsqrt(ms + EPS) * w


def _head_norm_rope(y, wrow, seg, segt, cos, sina, sinb):
    sq = (y * y).astype(BF16)
    ssum = jnp.dot(sq, seg, preferred_element_type=F32)
    r = lax.rsqrt(ssum * (1.0 / HEAD_DIM) + EPS)
    r_hi = r.astype(BF16)
    r_lo = (r - r_hi.astype(F32)).astype(BF16)
    rexp = jnp.dot(jnp.concatenate([r_hi, r_lo], axis=1), segt, preferred_element_type=F32)
    yn = y * rexp * wrow
    outs = []
    for j in range(QKV_WIDTH // LANES):
        c = yn[:, j * LANES:(j + 1) * LANES]
        outs.append(c * cos + pltpu.roll(c, LANES - ROT_DIM // 2, 1) * sina
                    + pltpu.roll(c, ROT_DIM // 2, 1) * sinb)
    return jnp.concatenate(outs, axis=1)


def _pool_mix_out(pooled, poolw_ref, pool_scale):
    outs = []
    for g in range(len(POOL_WINDOWS)):
        outs.append(jnp.dot(pooled[g].astype(BF16), poolw_ref[g], preferred_element_type=F32))
    return jnp.concatenate(outs, axis=1) * pool_scale


def _split3(x):
    hi = x.astype(BF16)
    r1 = x - hi.astype(F32)
    mid = r1.astype(BF16)
    lo = (r1 - mid.astype(F32)).astype(BF16)
    return hi, mid, lo


def _proj_prompt_kernel(tail_tiles, x_ref, n1w_ref, w_ref, qw_ref, kw_ref, cosr_ref, sinr_ref, cosn_ref,
                        sinn_ref, rotm_ref, perm_ref, permt_ref, seg_ref, segt_ref, poolw_ref, pools_ref,
                        ya_ref, gates_ref, q_ref, k_ref, v_ref, kvt_ref, ut_ref,
                        uext_ref):
    n = pl.program_id(0)
    xn = _rms(x_ref[...], n1w_ref[...]).astype(BF16)

    u = jnp.dot(xn, w_ref[:, C_U:C_U + POOL_WIDTH], preferred_element_type=F32)

    @pl.when(n == 0)
    def _():
        uext_ref[0:16, :] = jnp.zeros((16, POOL_WIDTH), F32)

    uext_ref[16:16 + TI, :] = u
    pos = n * TI + lax.broadcasted_iota(jnp.int32, (TI, 1), 0)
    pooled = []
    for g, w in enumerate(POOL_WINDOWS):
        lanes = slice(g * POOL_GROUP_DIM, (g + 1) * POOL_GROUP_DIM)
        s = uext_ref[16:16 + TI, lanes]
        for m in range(1, w):
            s = s + uext_ref[16 - m:16 - m + TI, lanes]
        cnt = jnp.minimum(pos + 1, w).astype(F32)
        pooled.append(s / cnt - u[:, lanes])
    ya_ref[...] = _pool_mix_out(pooled, poolw_ref, pools_ref[...]).astype(BF16)
    ut_ref[...] = u[TI - 16:, :]
    uext_ref[0:16, :] = uext_ref[TI:TI + 16, :]

    gl = jnp.dot(xn, w_ref[:, C_G:C_G + N_BRANCH * D_MODEL], preferred_element_type=F32)
    gates_ref[...] = jax.nn.sigmoid(gl).astype(BF16)

    xp = jnp.dot(perm_ref[...], xn, preferred_element_type=F32).astype(BF16)
    cn, sn = cosn_ref[pl.ds(n, 1), :], sinn_ref[pl.ds(n, 1), :]
    cr, sr = cosr_ref[...], sinr_ref[...]
    cos = cn * cr - sn * sr
    sin = sn * cr + cn * sr
    sina, sinb = sin * rotm_ref[0:1, :], sin * rotm_ref[1:2, :]
    seg, segt = seg_ref[...], segt_ref[...]
    shp = (N_CLASS, ROWS_PER_CLASS, QKV_WIDTH)
    yq = jnp.dot(xp, w_ref[:, C_Q:C_Q + QKV_WIDTH], preferred_element_type=F32)
    q_ref[...] = _head_norm_rope(yq, qw_ref[...], seg, segt, cos, sina, sinb).astype(BF16).reshape(shp)
    yk = jnp.dot(xp, w_ref[:, C_K:C_K + QKV_WIDTH], preferred_element_type=F32)
    kr = _head_norm_rope(yk, kw_ref[...], seg, segt, cos, sina, sinb)
    k_ref[...] = kr.astype(BF16).reshape(shp)
    yv = jnp.dot(xp, w_ref[:, C_V:C_V + QKV_WIDTH], preferred_element_type=F32)
    v_ref[...] = yv.astype(BF16).reshape(shp)

    @pl.when(n >= pl.num_programs(0) - tail_tiles)
    def _():
        def natural_t(val):
            nat = sum(jnp.dot(permt_ref[...], part, preferred_element_type=F32) for part in _split3(val))
            return nat.T
        kvt_ref[0:QKV_WIDTH, :] = natural_t(kr)
        kvt_ref[QKV_WIDTH:2 * QKV_WIDTH, :] = natural_t(yv)


def _rope_tables(pos):
    half = ROT_DIM // 2
    inv = ROPE_THETA ** (-(jnp.arange(half, dtype=F32) * 2.0 / ROT_DIM))
    ang = pos.astype(F32)[:, None] * inv[None, :]
    c, s = jnp.cos(ang), jnp.sin(ang)
    n = pos.shape[0]
    ones = jnp.ones((n, HEAD_DIM - ROT_DIM), F32)
    zeros = jnp.zeros((n, HEAD_DIM - ROT_DIM), F32)
    z8 = jnp.zeros((n, half), F32)
    cos64 = jnp.concatenate([c, c, ones], axis=1)
    sina64 = jnp.concatenate([-s, z8, zeros], axis=1)
    sinb64 = jnp.concatenate([z8, s, zeros], axis=1)
    rep = lambda t: jnp.concatenate([t, t], axis=1)
    return rep(cos64), rep(sina64), rep(sinb64)


def _rope_lane_freq():
    half = ROT_DIM // 2
    inv = ROPE_THETA ** (-(np.arange(half, dtype=np.float64) * 2.0 / ROT_DIM))
    d = np.arange(LANES) % HEAD_DIM
    freq = np.where(d < ROT_DIM, inv[d % half], 0.0)
    mask = np.zeros((8, LANES))
    mask[0] = np.where(d < half, -1.0, 0.0)
    mask[1] = np.where((d >= half) & (d < ROT_DIM), 1.0, 0.0)
    return freq, mask


def _seg_mats():
    h = np.arange(QKV_WIDTH) // HEAD_DIM
    seg = (h[:, None] == np.arange(LANES)[None, :]).astype(np.float32)
    return jnp.asarray(seg, BF16), jnp.asarray(np.concatenate([seg.T, seg.T], axis=0), BF16)


def _perm_mat():
    cls = _class_order()
    p = np.zeros((TI, TI), np.float32)
    for c in range(N_CLASS):
        for j in range(ROWS_PER_CLASS):
            p[c * ROWS_PER_CLASS + j, N_CLASS * j + cls[c]] = 1.0
    return p


def _proj_prompt(x, n1w, w_bf, qw_row, kw_row, poolw_bf, pool_scale):
    s = x.shape[0]
    nt = s // TI
    cls = np.asarray(_class_order())
    perm = _perm_mat()
    j = np.arange(ROWS_PER_CLASS)
    inner = (N_CLASS * j[None, :] + cls[:, None]).reshape(-1).astype(np.float64)
    freq, rot_mask = _rope_lane_freq()
    ang_r = inner[:, None] * freq[None, :]
    ang_n = (np.arange(nt, dtype=np.float64) * TI)[:, None] * freq[None, :]
    f32c = lambda a: jnp.asarray(a.astype(np.float32))
    cosr, sinr, cosn, sinn = f32c(np.cos(ang_r)), f32c(np.sin(ang_r)), f32c(np.cos(ang_n)), f32c(np.sin(ang_n))
    rotm = f32c(rot_mask)
    seg, segt = _seg_mats()
    n_i = s // N_CLASS
    tail_tokens = min(DILATION_PAIRS[-1][0], s)
    tail_tiles = tail_tokens // TI
    row = lambda a: a.reshape(1, -1)
    tile = lambda n: (n, 0)
    rblk = lambda n: (0, n, 0)
    tblk = lambda n: (0, jnp.maximum(n - (nt - tail_tiles), 0))
    outs = pl.pallas_call(
        functools.partial(_proj_prompt_kernel, tail_tiles),
        grid=(nt,),
        in_specs=[
            pl.BlockSpec((TI, D_MODEL), tile),
            _const_spec((1, D_MODEL)),
            _const_spec((D_MODEL, IN_COLS), single_buffer=True),
            _const_spec((1, QKV_WIDTH)),
            _const_spec((1, QKV_WIDTH)),
            _const_spec((TI, LANES)), _const_spec((TI, LANES)),
            _const_spec((nt, LANES)), _const_spec((nt, LANES)), _const_spec((8, LANES)),
            _const_spec((TI, TI)), _const_spec((TI, TI)),
            _const_spec((QKV_WIDTH, LANES)),
            _const_spec((2 * LANES, QKV_WIDTH)),
            _const_spec((len(POOL_WINDOWS), POOL_GROUP_DIM, POOL_GROUP_DIM)),
            _const_spec((1, POOL_WIDTH)),
        ],
        out_specs=[
            pl.BlockSpec((TI, POOL_WIDTH), tile),
            pl.BlockSpec((TI, N_BRANCH * D_MODEL), tile),
            pl.BlockSpec((N_CLASS, ROWS_PER_CLASS, QKV_WIDTH), rblk),
            pl.BlockSpec((N_CLASS, ROWS_PER_CLASS, QKV_WIDTH), rblk),
            pl.BlockSpec((N_CLASS, ROWS_PER_CLASS, QKV_WIDTH), rblk),
            pl.BlockSpec((2 * QKV_WIDTH, TI), tblk),
            _const_spec((16, POOL_WIDTH)),
        ],
        out_shape=[
            jax.ShapeDtypeStruct((s, POOL_WIDTH), BF16),
            jax.ShapeDtypeStruct((s, N_BRANCH * D_MODEL), BF16),
            jax.ShapeDtypeStruct((N_CLASS, n_i, QKV_WIDTH), BF16),
            jax.ShapeDtypeStruct((N_CLASS, n_i, QKV_WIDTH), BF16),
            jax.ShapeDtypeStruct((N_CLASS, n_i, QKV_WIDTH), BF16),
            jax.ShapeDtypeStruct((2 * QKV_WIDTH, tail_tokens), F32),
            jax.ShapeDtypeStruct((16, POOL_WIDTH), F32),
        ],
        scratch_shapes=[pltpu.VMEM((16 + TI, POOL_WIDTH), F32)],
        compiler_params=pltpu.CompilerParams(
            dimension_semantics=("arbitrary",), vmem_limit_bytes=VMEM_LIMIT_BYTES),
        name="proj_prompt",
    )(x, row(n1w), w_bf, row(qw_row), row(kw_row), cosr, sinr, cosn, sinn, rotm,
      jnp.asarray(perm, BF16), jnp.asarray(perm.T, BF16), seg, segt, poolw_bf, row(pool_scale))
    return outs


AT_I = 128
N_KEYS_BACK = 128
BLOCKS_PER_TRIP = 2


def _attn_update(q, k, v, bias, m_old, l_old, acc_old):
    nq, nk = q.shape[0], k.shape[0]
    lane = lax.broadcasted_iota(jnp.int32, (nq, LANES), 1)
    first = lane < HEAD_DIM
    bias2 = jnp.concatenate([bias, bias], axis=0)
    ones = jnp.ones((nk, LANES), BF16)
    m_out, l_out, a_out = [], [], []
    for hp in range(N_HEADS // 2):
        sl = slice(hp * LANES, (hp + 1) * LANES)
        qp, kp, vp = q[:, sl], k[:, sl], v[:, sl]
        mo, lo, ao = m_old[:, sl], l_old[:, sl], acc_old[:, sl]
        zero = jnp.zeros_like(qp)
        q2 = jnp.concatenate([jnp.where(first, qp, zero), jnp.where(first, zero, qp)], axis=0)
        s2 = lax.dot_general(q2, kp, (((1,), (1,)), ((), ())), preferred_element_type=F32) + bias2
        mo2 = jnp.concatenate([mo[:, 0:1], mo[:, HEAD_DIM:HEAD_DIM + 1]], axis=0)
        mn2 = jnp.maximum(mo2, jnp.max(s2, axis=1, keepdims=True))
        p2 = jnp.exp(s2 - mn2).astype(BF16)
        pv2 = jnp.dot(p2, jnp.concatenate([vp, ones], axis=1), preferred_element_type=F32)
        pv = jnp.where(first, pv2[:nq, :LANES], pv2[nq:, :LANES])
        lc = jnp.where(first, pv2[:nq, LANES:], pv2[nq:, LANES:])
        mn = jnp.where(first, mn2[:nq], mn2[nq:])
        alpha = jnp.exp(mo - mn)
        m_out.append(mn)
        l_out.append(alpha * lo + lc)
        a_out.append(alpha * ao + pv)
    cat = lambda xs: jnp.concatenate(xs, axis=1)
    return cat(m_out), cat(l_out), cat(a_out)


def _attn_prompt_kernel(q_ref, kc_ref, kp_ref, vc_ref, vp_ref, b0_ref, b1_ref, b2_ref,
                        r0_ref, r1_ref, r2_ref, o_ref,
                        kcat, vcat, acc, m_s, l_s, ost):
    n = pl.program_id(0)
    g = pl.program_id(1)
    i0 = n * AT_I
    kcat[:, 0:AT_I, :] = kp_ref[...]
    kcat[:, AT_I:2 * AT_I, :] = kc_ref[...]
    vcat[:, 0:AT_I, :] = vp_ref[...]
    vcat[:, AT_I:2 * AT_I, :] = vc_ref[...]

    def key_bias(b_ref, r_ref, base):
        return b_ref[...] + jnp.where(base + r_ref[...] >= 0, 0.0, NEG)

    @pl.when(g == 0)
    def _():
        m_s[...] = jnp.full(m_s.shape, NEG, F32)
        l_s[...] = jnp.zeros(l_s.shape, F32)
        acc[...] = jnp.zeros(acc.shape, F32)

        def body(s, carry):
            qo = pl.multiple_of(16 * s, 16)
            ko = pl.multiple_of(AT_I - 16 + 16 * s, 16)
            q16 = [q_ref[c, pl.ds(qo, 16), :] for c in range(N_CLASS)]
            k32 = [kcat[c, pl.ds(ko, 32), :] for c in range(N_CLASS)]
            v32 = [vcat[c, pl.ds(ko, 32), :] for c in range(N_CLASS)]
            cat = lambda f: jnp.concatenate([f(c) for c in range(N_CLASS)], axis=0)
            work = []
            for half in range(2):
                qh = pl.multiple_of(qo + 8 * half, 8)
                q = cat(lambda c: q16[c][8 * half:8 * half + 8])
                k = cat(lambda c: k32[c][8 + 8 * half:24 + 8 * half])
                v = cat(lambda c: v32[c][8 + 8 * half:24 + 8 * half])
                mo = cat(lambda c: m_s[c, pl.ds(qh, 8), :])
                lo = cat(lambda c: l_s[c, pl.ds(qh, 8), :])
                ao = cat(lambda c: acc[c, pl.ds(qh, 8), :])
                bias = key_bias(b0_ref, r0_ref, i0 + 16 * s + 8 * half)
                work.append((qh, (q, k, v, bias, mo, lo, ao)))
            done = [(qh, _attn_update(*args)) for qh, args in work]
            for qh, (mn, ln, an) in done:
                for c in range(N_CLASS):
                    m_s[c, pl.ds(qh, 8), :] = mn[8 * c:8 * c + 8]
                    l_s[c, pl.ds(qh, 8), :] = ln[8 * c:8 * c + 8]
                    acc[c, pl.ds(qh, 8), :] = an[8 * c:8 * c + 8]
            return carry

        lax.fori_loop(0, AT_I // 16, body, 0)

    @pl.when(g == 1)
    def _():
        def load(r4, s):
            qo = pl.multiple_of(32 * s, 32)
            ko = pl.multiple_of(AT_I - 32 + 32 * s, 32)
            cat = lambda f: jnp.concatenate([f(4 * r4 + a) for a in range(4)], axis=0)
            q = cat(lambda c: q_ref[c, pl.ds(qo, 32), :])
            k = cat(lambda c: kcat[c, pl.ds(ko, 64), :])
            v = cat(lambda c: vcat[c, pl.ds(ko, 64), :])
            mo = cat(lambda c: m_s[c, pl.ds(qo, 32), :])
            lo = cat(lambda c: l_s[c, pl.ds(qo, 32), :])
            ao = cat(lambda c: acc[c, pl.ds(qo, 32), :])
            bias = key_bias(b1_ref, r1_ref, i0 + 32 * s)
            return q, k, v, bias, mo, lo, ao

        def store(r4, s, mn, ln, an):
            qo = pl.multiple_of(32 * s, 32)
            for a in range(4):
                c = 4 * r4 + a
                m_s[c, pl.ds(qo, 32), :] = mn[32 * a:32 * a + 32]
                l_s[c, pl.ds(qo, 32), :] = ln[32 * a:32 * a + 32]
                acc[c, pl.ds(qo, 32), :] = an[32 * a:32 * a + 32]

        def body(idx, carry):
            per_class = 4 // BLOCKS_PER_TRIP
            where = [(idx // per_class, BLOCKS_PER_TRIP * (idx % per_class) + j) for j in range(BLOCKS_PER_TRIP)]
            work = [load(r4, s) for r4, s in where]
            done = [_attn_update(*args) for args in work]
            for (r4, s), res in zip(where, done):
                store(r4, s, *res)
            return carry

        lax.fori_loop(0, 16 // BLOCKS_PER_TRIP, body, 0)

    @pl.when(g == 2)
    def _():
        bias = key_bias(b2_ref, r2_ref, i0)

        def body(cc, carry):
            cs = [BLOCKS_PER_TRIP * cc + j for j in range(BLOCKS_PER_TRIP)]
            work = [(q_ref[c], kcat[c], vcat[c], bias, m_s[c], l_s[c], acc[c]) for c in cs]
            done = [_attn_update(*args) for args in work]
            for c, (mn, ln, an) in zip(cs, done):
                m_s[c] = mn
                l_s[c] = ln
                acc[c] = an
            return carry

        lax.fori_loop(0, N_CLASS // BLOCKS_PER_TRIP, body, 0)

        cls = _class_order()
        for c in range(N_CLASS):
            o = acc[c] / l_s[c]
            for jc in range(ATTN_WIDTH // LANES):
                ost[jc, pl.ds(cls[c], AT_I, stride=N_CLASS), :] = o[:, jc * LANES:(jc + 1) * LANES]
        o_ref[...] = jnp.concatenate([ost[jc] for jc in range(ATTN_WIDTH // LANES)], axis=1).astype(BF16)


def _attn_bias_tables():
    cls = np.asarray(_class_order())

    def bias(diff):
        return np.where((diff >= 0) & (diff <= N_KEYS_BACK), 0.0, NEG).astype(np.float32)

    c, r = np.divmod(np.arange(128), 8)
    c2, r2 = np.divmod(np.arange(256), 16)
    d0 = 16 * (r[:, None] - r2[None, :] + 8) + cls[c][:, None] - cls[c2][None, :]
    rel0 = (r2 - 8).astype(np.int32)[None, :]
    a, r = np.divmod(np.arange(128), 32)
    a2, r2 = np.divmod(np.arange(256), 64)
    d1 = 4 * (r[:, None] - r2[None, :] + 32) + a[:, None] - a2[None, :]
    rel1 = (r2 - 32).astype(np.int32)[None, :]
    r = np.arange(128)
    r2 = np.arange(256)
    d2 = r[:, None] - r2[None, :] + 128
    rel2 = (r2 - 128).astype(np.int32)[None, :]
    return (jnp.asarray(bias(d0)), jnp.asarray(bias(d1)), jnp.asarray(bias(d2)),
            jnp.asarray(rel0), jnp.asarray(rel1), jnp.asarray(rel2))


def _attn_prompt(q_r, k_r, v_r):
    n_i = q_r.shape[1]
    s = n_i * N_CLASS
    nsteps = n_i // AT_I
    b0, b1, b2, r0, r1, r2 = _attn_bias_tables()
    blk = (N_CLASS, AT_I, ATTN_WIDTH)
    cur = lambda n, g: (0, n, g)
    prev = lambda n, g: (0, jnp.maximum(n - 1, 0), g)
    return pl.pallas_call(
        _attn_prompt_kernel,
        grid=(nsteps, N_DIL),
        in_specs=[
            pl.BlockSpec(blk, cur), pl.BlockSpec(blk, cur), pl.BlockSpec(blk, prev),
            pl.BlockSpec(blk, cur), pl.BlockSpec(blk, prev),
            _const_spec(b0.shape), _const_spec(b1.shape), _const_spec(b2.shape),
            _const_spec(r0.shape), _const_spec(r1.shape), _const_spec(r2.shape),
        ],
        out_specs=pl.BlockSpec((AT_I * N_CLASS, ATTN_WIDTH), lambda n, g: (n, 0)),
        out_shape=jax.ShapeDtypeStruct((s, ATTN_WIDTH), BF16),
        scratch_shapes=[
            pltpu.VMEM((N_CLASS, 2 * AT_I, ATTN_WIDTH), BF16),
            pltpu.VMEM((N_CLASS, 2 * AT_I, ATTN_WIDTH), BF16),
            pltpu.VMEM((N_CLASS, AT_I, ATTN_WIDTH), F32),
            pltpu.VMEM((N_CLASS, AT_I, ATTN_WIDTH), F32),
            pltpu.VMEM((N_CLASS, AT_I, ATTN_WIDTH), F32),
            pltpu.VMEM((ATTN_WIDTH // LANES, AT_I * N_CLASS, LANES), F32),
        ],
        compiler_params=pltpu.CompilerParams(
            dimension_semantics=("arbitrary", "arbitrary"), vmem_limit_bytes=VMEM_LIMIT_BYTES),
        name="attn_prompt",
    )(q_r, k_r, k_r, v_r, v_r, b0, b1, b2, r0, r1, r2)


def _proj_sample_kernel(n_b, n_t, x_ref, xb_ref, n1w_ref, w_ref, qw_ref, kw_ref, cos_ref, sina_ref, sinb_ref,
                        seg_ref, segt_ref, poolw_ref, pools_ref, state_ref,
                        ya_ref, gates_ref, q_ref, k_ref, v_ref, u_ref, kt_ref, vt_ref):
    xn = _rms(x_ref[...], n1w_ref[...]).astype(BF16)
    u = jnp.dot(xn, w_ref[:, C_U:C_U + POOL_WIDTH], preferred_element_type=F32)
    u_ref[...] = u

    def ext(j, lanes):
        if j < POOL_STATE:
            return state_ref[j, :, lanes]
        return u[(j - POOL_STATE) * n_b:(j - POOL_STATE + 1) * n_b, lanes]

    pooled = []
    for g, w in enumerate(POOL_WINDOWS):
        lanes = slice(g * POOL_GROUP_DIM, (g + 1) * POOL_GROUP_DIM)
        rows = []
        for i in range(n_t):
            s = ext(POOL_STATE + i, lanes)
            for m in range(1, w):
                s = s + ext(POOL_STATE + i - m, lanes)
            cnt = float(min(PAST_LEN + i + 1, w))
            rows.append(s / cnt - ext(POOL_STATE + i, lanes))
        pooled.append(jnp.concatenate(rows, axis=0))
    ya_ref[...] = _pool_mix_out(pooled, poolw_ref, pools_ref[...]).astype(BF16)

    gl = jnp.dot(xn, w_ref[:, C_G:C_G + N_BRANCH * D_MODEL], preferred_element_type=F32)
    gates_ref[...] = jax.nn.sigmoid(gl).astype(BF16)

    xnb = _rms(xb_ref[...], n1w_ref[...]).astype(BF16)
    cos, sina, sinb = cos_ref[...], sina_ref[...], sinb_ref[...]
    seg, segt = seg_ref[...], segt_ref[...]
    yq = jnp.dot(xnb, w_ref[:, C_Q:C_Q + QKV_WIDTH], preferred_element_type=F32)
    q_ref[...] = _head_norm_rope(yq, qw_ref[...], seg, segt, cos, sina, sinb)
    yk = jnp.dot(xnb, w_ref[:, C_K:C_K + QKV_WIDTH], preferred_element_type=F32)
    kr = _head_norm_rope(yk, kw_ref[...], seg, segt, cos, sina, sinb)
    k_ref[...] = kr
    kt_ref[...] = kr.T
    yv = jnp.dot(xnb, w_ref[:, C_V:C_V + QKV_WIDTH], preferred_element_type=F32)
    v_ref[...] = yv
    vt_ref[...] = yv.T


def _proj_sample(x, xb, n1w, w_bf, qw_row, kw_row, poolw_bf, pool_scale, state_t, n_b, n_t):
    rows = n_b * n_t
    pos = PAST_LEN + jnp.arange(rows, dtype=jnp.int32) % n_t
    cos, sina, sinb = _rope_tables(pos)
    seg, segt = _seg_mats()
    row = lambda a: a.reshape(1, -1)
    args = (x, xb, row(n1w), w_bf, row(qw_row), row(kw_row), cos, sina, sinb, seg, segt,
            poolw_bf, row(pool_scale), state_t)
    f32_qkv = jax.ShapeDtypeStruct((rows, QKV_WIDTH), F32)
    out_shape = [
        jax.ShapeDtypeStruct((rows, POOL_WIDTH), BF16),
        jax.ShapeDtypeStruct((rows, N_BRANCH * D_MODEL), BF16),
        f32_qkv, f32_qkv, f32_qkv,
        jax.ShapeDtypeStruct((rows, POOL_WIDTH), F32),
        jax.ShapeDtypeStruct((QKV_WIDTH, rows), F32),
        jax.ShapeDtypeStruct((QKV_WIDTH, rows), F32),
    ]
    return pl.pallas_call(
        functools.partial(_proj_sample_kernel, n_b, n_t),
        grid=(1,),
        in_specs=[_const_spec(a.shape) for a in args],
        out_specs=[_const_spec(o.shape) for o in out_shape],
        out_shape=out_shape,
        compiler_params=pltpu.CompilerParams(
            dimension_semantics=("arbitrary",), vmem_limit_bytes=VMEM_LIMIT_BYTES),
        name="proj_sample",
    )(*args)


HEADS_PER_STEP = 8


def _attn_roll_kernel(n_t, q_ref, kn_ref, vn_ref, newt_ref, c0_ref, c1_ref, c2_ref,
                      b0_ref, b1_ref, b2_ref, bn_ref, o_ref, o0_ref, o1_ref, o2_ref):
    caches = (c0_ref, c1_ref, c2_ref)
    outs = (o0_ref, o1_ref, o2_ref)
    biases = (b0_ref, b1_ref, b2_ref)
    nt_contract = (((1,), (1,)), ((), ()))
    col = pl.program_id(1) * n_t
    new_tile = pl.multiple_of((col // LANES) * LANES, LANES)
    new_shift = (LANES - n_t) - col % LANES
    for hh in range(HEADS_PER_STEP):
        scores, values = [], []
        for g in range(N_DIL):
            q = q_ref[hh, g].astype(BF16)
            kc = caches[g][0, hh].astype(BF16)
            scores.append(jnp.dot(q, kc, preferred_element_type=F32) + biases[g][...])
            kn = kn_ref[hh, g].astype(BF16)
            scores.append(lax.dot_general(q, kn, nt_contract, preferred_element_type=F32) + bn_ref[g])
            values.append(caches[g][1, hh].astype(BF16))
            values.append(vn_ref[hh, g].astype(BF16))
        m = functools.reduce(jnp.maximum, [jnp.max(s, axis=1, keepdims=True) for s in scores])
        l = jnp.zeros_like(m)
        o = jnp.zeros((n_t, HEAD_DIM), F32)
        for idx, (s, v) in enumerate(zip(scores, values)):
            p = jnp.exp(s - m)
            l = l + jnp.sum(p, axis=1, keepdims=True)
            pb = p.astype(BF16)
            if idx % 2 == 0:
                o = o + lax.dot_general(pb, v, nt_contract, preferred_element_type=F32)
            else:
                o = o + jnp.dot(pb, v, preferred_element_type=F32)
        o_ref[hh] = o / l

        for g in range(N_DIL):
            length = caches[g].shape[-1]
            lane = lax.broadcasted_iota(jnp.int32, (HEAD_DIM, LANES), 1)
            for kv in range(2):
                rolled = pltpu.roll(caches[g][kv, hh], length - n_t, 1)
                new = pltpu.roll(newt_ref[kv, g, hh, :, pl.ds(new_tile, LANES)], new_shift, 1)
                if length > LANES:
                    outs[g][kv, hh, :, 0:length - LANES] = rolled[:, 0:length - LANES]
                outs[g][kv, hh, :, length - LANES:length] = jnp.where(
                    lane >= LANES - n_t, new, rolled[:, length - LANES:length])


def _attn_roll_bias(n_t):
    i = np.arange(n_t)
    bias, bias_new = [], []
    for win, dil in DILATION_PAIRS:
        length = min(win, PAST_LEN)
        delta = length + i[:, None] - np.arange(length)[None, :]
        ok = (delta % dil == 0) & (delta // dil >= 1) & (delta // dil <= win // dil)
        bias.append(jnp.asarray(np.where(ok, 0.0, NEG).astype(np.float32)))
        dn = i[:, None] - i[None, :]
        okn = (dn >= 0) & (dn % dil == 0) & (dn // dil <= win // dil)
        bias_new.append(np.where(okn, 0.0, NEG).astype(np.float32))
    return bias, jnp.asarray(np.stack(bias_new))


def _attn_roll(q_h, kn_h, vn_h, new_t, caches_t):
    n_b, _, _, n_t, _ = q_h.shape
    bias, bias_new = _attn_roll_bias(n_t)
    hs = HEADS_PER_STEP
    qspec = pl.BlockSpec((None, hs, N_DIL, n_t, HEAD_DIM), lambda h, b: (b, h, 0, 0, 0))
    cspec = lambda c: pl.BlockSpec((None, 2, hs, HEAD_DIM, c.shape[-1]), lambda h, b: (b, 0, h, 0, 0))
    return pl.pallas_call(
        functools.partial(_attn_roll_kernel, n_t),
        grid=(N_HEADS // hs, n_b),
        in_specs=[qspec, qspec, qspec,
                  pl.BlockSpec((2, N_DIL, hs, HEAD_DIM, n_t * n_b), lambda h, b: (0, 0, h, 0, 0))]
                 + [cspec(c) for c in caches_t]
                 + [_const_spec(b.shape) for b in bias] + [_const_spec(bias_new.shape)],
        out_specs=[pl.BlockSpec((None, hs, n_t, HEAD_DIM), lambda h, b: (b, h, 0, 0))]
                  + [cspec(c) for c in caches_t],
        out_shape=[jax.ShapeDtypeStruct((n_b, N_HEADS, n_t, HEAD_DIM), F32)]
                  + [jax.ShapeDtypeStruct(c.shape, c.dtype) for c in caches_t],
        compiler_params=pltpu.CompilerParams(
            dimension_semantics=("arbitrary", "arbitrary"), vmem_limit_bytes=VMEM_LIMIT_BYTES),
        name="attn_roll",
    )(q_h, kn_h, vn_h, new_t, *caches_t, *bias, bias_new)


TM = 512
MOE_ROWS = 256
L_E0, L_E1, L_W0, L_W1, L_R0, L_R1 = 0, 1, 2, 3, 4, 5
L_GROUP = N_EXPERTS


def _lane_min_index(mask, lane):
    return jnp.min(jnp.where(mask, lane, float(LANES)), axis=1, keepdims=True)


def _zero_fill_step(n, xs_ref, zbuf, sem):
    slab = zbuf.shape[0]
    n_full, rem = divmod(xs_ref.shape[0], slab)
    full = lambda i: pltpu.make_async_copy(
        zbuf, xs_ref.at[pl.ds(pl.multiple_of(i * slab, MOE_ROWS), slab)], sem)
    tail = pltpu.make_async_copy(zbuf.at[pl.ds(0, max(rem, 1))],
                                 xs_ref.at[pl.ds(n_full * slab, max(rem, 1))], sem)

    @pl.when(n == 0)
    def _():
        zbuf[...] = jnp.zeros(zbuf.shape, F32)
        if rem:
            tail.start()

    @pl.when(n < n_full)
    def _():
        full(n).start()

    @pl.when(n == pl.num_programs(0) - 1)
    def _():
        for _ in range(n_full):
            full(0).wait()
        if rem:
            tail.wait()


def _merge_route_kernel(n_first, xa_ref, xb_ref, yaa_ref, yab_ref, yba_ref, ybb_ref, ga_ref, gb_ref,
                        wbp_ref, wba_ref, wo_ref, n2w_ref, wrh_ref, wrl_ref, br_ref, ltri_ref,
                        x1_ref, h_ref, route_ref, cnt_ref, xs_ref, zbuf, sem):
    n = pl.program_id(0)
    _zero_fill_step(n, xs_ref, zbuf, sem)
    first = n < n_first
    pick = lambda a, b: jnp.where(first, a[...], b[...])
    pa = jnp.dot(pick(yaa_ref, yab_ref), wbp_ref[...], preferred_element_type=F32)
    pb = jnp.dot(pick(yba_ref, ybb_ref), wba_ref[...], preferred_element_type=F32)
    gts = pick(ga_ref, gb_ref)
    merged = gts[:, :D_MODEL].astype(F32) * pa + gts[:, D_MODEL:].astype(F32) * pb
    x1 = pick(xa_ref, xb_ref) + jnp.dot(merged.astype(BF16), wo_ref[...], preferred_element_type=F32)
    x1_ref[...] = x1
    h = _rms(x1, n2w_ref[...])
    h_ref[...] = h

    h_hi = h.astype(BF16)
    h_lo = (h - h_hi.astype(F32)).astype(BF16)
    logits = (jnp.dot(h_hi, wrh_ref[...], preferred_element_type=F32)
              + jnp.dot(h_hi, wrl_ref[...], preferred_element_type=F32)
              + jnp.dot(h_lo, wrh_ref[...], preferred_element_type=F32)) + br_ref[...]
    rows = logits.shape[0]
    lane = lax.broadcasted_iota(jnp.int32, (rows, LANES), 1).astype(F32)
    is_group = (lane >= L_GROUP) & (lane < L_GROUP + N_EXPERT_GROUPS)
    gl = jnp.where(is_group, logits, NEG)
    gmax = jnp.max(gl, axis=1, keepdims=True)
    g_w = 1.0 / jnp.sum(jnp.exp(gl - gmax), axis=1, keepdims=True)
    g_sel = _lane_min_index(gl == gmax, lane) - L_GROUP
    in_group = (lane >= g_sel * EXPERTS_PER_GROUP) & (lane < (g_sel + 1.0) * EXPERTS_PER_GROUP)
    el = jnp.where(in_group, logits, NEG)
    v0 = jnp.max(el, axis=1, keepdims=True)
    e0 = _lane_min_index(el == v0, lane)
    el2 = jnp.where(lane == e0, NEG, el)
    v1 = jnp.max(el2, axis=1, keepdims=True)
    e1 = _lane_min_index(el2 == v1, lane)
    t = jnp.exp(v1 - v0)
    w0 = g_w / (1.0 + t)
    w1 = g_w * t / (1.0 + t)

    hot0 = lane == e0
    hot1 = lane == e1
    onehot = jnp.where(hot0 | hot1, 1.0, 0.0)

    @pl.when(n == 0)
    def _():
        cnt_ref[...] = jnp.zeros(cnt_ref.shape, F32)

    before = jnp.dot(ltri_ref[...], onehot.astype(BF16), preferred_element_type=F32) + cnt_ref[...]
    r0 = jnp.sum(jnp.where(hot0, before, 0.0), axis=1, keepdims=True)
    r1 = jnp.sum(jnp.where(hot1, before, 0.0), axis=1, keepdims=True)
    cnt_ref[...] = cnt_ref[...] + jnp.sum(onehot, axis=0, keepdims=True)

    rec = jnp.zeros((rows, LANES), F32)
    for ln, val in ((L_E0, e0), (L_E1, e1), (L_W0, w0), (L_W1, w1), (L_R0, r0), (L_R1, r1)):
        rec = jnp.where(lane == ln, val, rec)
    route_ref[...] = rec


def _router_weights(w_rg, b_rg, w_re, b_re):
    w = jnp.zeros((D_MODEL, LANES), F32)
    w = w.at[:, :N_EXPERTS].set(w_re).at[:, L_GROUP:L_GROUP + N_EXPERT_GROUPS].set(w_rg)
    b = jnp.zeros((1, LANES), F32)
    b = b.at[0, :N_EXPERTS].set(b_re).at[0, L_GROUP:L_GROUP + N_EXPERT_GROUPS].set(b_rg)
    w_hi = w.astype(BF16)
    w_lo = (w - w_hi.astype(F32)).astype(BF16)
    return w_hi, w_lo, b


def _merge_route(xa, xb, yaa, yab, yba, ybb, ga, gb, wbp, wba, wo, n2w, wr_hi, wr_lo, br):
    n_first = xa.shape[0] // TM
    t_all = xa.shape[0] + xb.shape[0]
    n_rows = _moe_blocks(t_all) * MOE_ROWS
    slab_rows = -(-n_rows // MOE_ROWS // (t_all // TM)) * MOE_ROWS
    assert n_rows // slab_rows <= t_all // TM
    ltri = jnp.asarray(np.tril(np.ones((TM, TM), np.float32), -1), BF16)
    ta = lambda w: pl.BlockSpec((TM, w), lambda n: (jnp.minimum(n, n_first - 1), 0))
    tb = lambda w: pl.BlockSpec((TM, w), lambda n: (jnp.maximum(n - n_first, 0), 0))
    tile = lambda w: pl.BlockSpec((TM, w), lambda n: (n, 0))
    return pl.pallas_call(
        functools.partial(_merge_route_kernel, n_first),
        grid=(t_all // TM,),
        in_specs=[
            ta(D_MODEL), tb(D_MODEL), ta(POOL_WIDTH), tb(POOL_WIDTH), ta(ATTN_WIDTH), tb(ATTN_WIDTH),
            ta(N_BRANCH * D_MODEL), tb(N_BRANCH * D_MODEL),
            _const_spec(wbp.shape), _const_spec(wba.shape), _const_spec(wo.shape),
            _const_spec((1, D_MODEL)),
            _const_spec(wr_hi.shape), _const_spec(wr_lo.shape), _const_spec(br.shape),
            _const_spec(ltri.shape),
        ],
        out_specs=[tile(D_MODEL), tile(D_MODEL), tile(LANES), _const_spec((1, LANES)),
                   pl.BlockSpec(memory_space=pl.ANY)],
        out_shape=[
            jax.ShapeDtypeStruct((t_all, D_MODEL), F32),
            jax.ShapeDtypeStruct((t_all, D_MODEL), F32),
            jax.ShapeDtypeStruct((t_all, LANES), F32),
            jax.ShapeDtypeStruct((1, LANES), F32),
            jax.ShapeDtypeStruct((n_rows, D_MODEL), F32),
        ],
        scratch_shapes=[pltpu.VMEM((slab_rows, D_MODEL), F32), pltpu.SemaphoreType.DMA],
        compiler_params=pltpu.CompilerParams(
            dimension_semantics=("arbitrary",), vmem_limit_bytes=VMEM_LIMIT_BYTES),
        name="merge_route",
    )(xa, xb, yaa, yab, yba, ybb, ga, gb, wbp, wba, wo, n2w.reshape(1, -1), wr_hi, wr_lo, br, ltri)


def _moe_blocks(n_tokens):
    return -(-(2 * n_tokens + N_EXPERTS * (MOE_ROWS - 1)) // MOE_ROWS)


def _plan_kernel(nb_pad, cnt_ref, route_ref, upper_ref, dest_ref, blk_ref):
    n = pl.program_id(0)
    cnt = cnt_ref[...]
    nblk = jnp.floor((cnt + (MOE_ROWS - 1)) * (1.0 / MOE_ROWS))
    nb8 = jnp.broadcast_to(nblk, (8, LANES)).astype(BF16)
    bstart = jnp.dot(nb8, upper_ref[...], preferred_element_type=F32)[0:1]
    bend = bstart + nblk
    pstart = bstart * MOE_ROWS

    rec = route_ref[...]
    rows = rec.shape[0]
    lane = lax.broadcasted_iota(jnp.int32, (rows, LANES), 1).astype(F32)
    col = lambda ln: jnp.sum(jnp.where(lane == ln, rec, 0.0), axis=1, keepdims=True)
    look = lambda e: jnp.sum(jnp.where(lane == e, pstart, 0.0), axis=1, keepdims=True)
    d0 = look(col(L_E0)) + col(L_R0)
    d1 = look(col(L_E1)) + col(L_R1)
    dest_ref[...] = jnp.where(lane == 0, d0, jnp.where(lane == 1, d1, 0.0)).astype(jnp.int32)

    @pl.when(n == 0)
    def _():
        b = lax.broadcasted_iota(jnp.int32, (nb_pad, LANES), 0).astype(F32)
        lane_b = lax.broadcasted_iota(jnp.int32, (nb_pad, LANES), 1)
        done = jnp.where((bend <= b) & (lane_b < N_EXPERTS), 1.0, 0.0)
        e_of_b = jnp.minimum(jnp.sum(done, axis=1, keepdims=True), float(N_EXPERTS - 1))
        total = jnp.max(jnp.where(lane_b < N_EXPERTS, bend, 0.0), axis=1, keepdims=True)
        blk_ref[...] = jnp.where(lane_b == 0, e_of_b, jnp.where(lane_b == 1, total, 0.0)).astype(jnp.int32)


def _plan(cnt, route, n_rows):
    t = route.shape[0]
    nb = n_rows // MOE_ROWS
    nb_pad = -(-nb // 8) * 8
    upper = jnp.asarray(np.triu(np.ones((LANES, LANES), np.float32), 1), BF16)
    return pl.pallas_call(
        functools.partial(_plan_kernel, nb_pad),
        grid=(t // TM,),
        in_specs=[_const_spec((1, LANES)), pl.BlockSpec((TM, LANES), lambda n: (n, 0)),
                  _const_spec((LANES, LANES))],
        out_specs=[pl.BlockSpec((TM, LANES), lambda n: (n, 0)), _const_spec((nb_pad, LANES))],
        out_shape=[jax.ShapeDtypeStruct((t, LANES), jnp.int32),
                   jax.ShapeDtypeStruct((nb_pad, LANES), jnp.int32)],
        compiler_params=pltpu.CompilerParams(
            dimension_semantics=("arbitrary",), vmem_limit_bytes=VMEM_LIMIT_BYTES),
        name="moe_plan",
    )(cnt, route, upper)


def _dispatch_kernel(dest_ref, h_ref, xs_in, xs_out, sem):
    del xs_in
    rows = h_ref.shape[0]

    def issue(t, carry):
        for k in range(2):
            pltpu.make_async_copy(h_ref.at[pl.ds(t, 1)], xs_out.at[pl.ds(dest_ref[k, t], 1)], sem).start(priority=k)
        return carry

    lax.fori_loop(0, rows, issue, 0, unroll=8)
    for k in range(2):
        pltpu.make_async_copy(h_ref, xs_out.at[pl.ds(0, rows)], sem).wait()


def _dispatch(dest_t, h, xs):
    t = h.shape[0]
    return pl.pallas_call(
        _dispatch_kernel,
        grid=(t // TM,),
        in_specs=[pl.BlockSpec((None, 8, TM), lambda n: (n, 0, 0), memory_space=pltpu.SMEM),
                  pl.BlockSpec((TM, D_MODEL), lambda n: (n, 0)),
                  pl.BlockSpec(memory_space=pl.ANY)],
        out_specs=pl.BlockSpec(memory_space=pl.ANY),
        out_shape=jax.ShapeDtypeStruct(xs.shape, xs.dtype),
        input_output_aliases={2: 0},
        scratch_shapes=[pltpu.SemaphoreType.DMA],
        compiler_params=pltpu.CompilerParams(
            dimension_semantics=("arbitrary",), vmem_limit_bytes=VMEM_LIMIT_BYTES),
        name="moe_dispatch",
    )(dest_t, h, xs)


def _expert_kernel(blk_ref, x_ref, wg_ref, wu_ref, wd_ref, y_ref, wg_bf, wu_bf, wd_bf):
    b = pl.program_id(0)
    used = b < blk_ref[1, 0]

    @pl.when(used)
    def _():
        prev = blk_ref[0, jnp.maximum(b - 1, 0)]

        @pl.when((b == 0) | (blk_ref[0, b] != prev))
        def _():
            wg_bf[...] = wg_ref[...].astype(BF16)
            wu_bf[...] = wu_ref[...].astype(BF16)
            wd_bf[...] = wd_ref[...].astype(BF16)

        x = x_ref[...].astype(BF16)
        gate = jnp.dot(x, wg_bf[...], preferred_element_type=F32)
        up = jnp.dot(x, wu_bf[...], preferred_element_type=F32)
        mid = (jax.nn.silu(gate) * up).astype(BF16)
        y_ref[...] = jnp.dot(mid, wd_bf[...], preferred_element_type=F32)

    @pl.when(jnp.logical_not(used))
    def _():
        y_ref[...] = jnp.zeros(y_ref.shape, F32)


def _experts(blk_t, xs, w_g, w_u, w_d):
    n_rows = xs.shape[0]
    nb = n_rows // MOE_ROWS
    last = lambda b, blk: jnp.minimum(b, blk[1, 0] - 1)
    grid_spec = pltpu.PrefetchScalarGridSpec(
        num_scalar_prefetch=1,
        grid=(nb,),
        in_specs=[
            pl.BlockSpec((MOE_ROWS, D_MODEL), lambda b, blk: (last(b, blk), 0)),
            pl.BlockSpec((None, D_MODEL, D_EXPERT), lambda b, blk: (blk[0, last(b, blk)], 0, 0)),
            pl.BlockSpec((None, D_MODEL, D_EXPERT), lambda b, blk: (blk[0, last(b, blk)], 0, 0)),
            pl.BlockSpec((None, D_EXPERT, D_MODEL), lambda b, blk: (blk[0, last(b, blk)], 0, 0)),
        ],
        out_specs=pl.BlockSpec((MOE_ROWS, D_MODEL), lambda b, blk: (b, 0)),
        scratch_shapes=[pltpu.VMEM((D_MODEL, D_EXPERT), BF16), pltpu.VMEM((D_MODEL, D_EXPERT), BF16),
                        pltpu.VMEM((D_EXPERT, D_MODEL), BF16)],
    )
    return pl.pallas_call(
        _expert_kernel,
        grid_spec=grid_spec,
        out_shape=jax.ShapeDtypeStruct((n_rows, D_MODEL), F32),
        compiler_params=pltpu.CompilerParams(
            dimension_semantics=("arbitrary",), vmem_limit_bytes=VMEM_LIMIT_BYTES),
        name="moe_experts",
    )(blk_t, xs, w_g, w_u, w_d)


def _combine_kernel(n_first, dest_ref, x1_ref, route_ref, ys_ref, oa_ref, ob_ref, gbuf, sem):
    n = pl.program_id(0)
    rows = x1_ref.shape[0]

    def issue(t, carry):
        for k in range(2):
            pltpu.make_async_copy(ys_ref.at[pl.ds(dest_ref[k, t], 1)], gbuf.at[k, pl.ds(t, 1)], sem).start(priority=k)
        return carry

    lax.fori_loop(0, rows, issue, 0, unroll=8)
    for k in range(2):
        pltpu.make_async_copy(ys_ref.at[pl.ds(0, rows)], gbuf.at[k], sem).wait()
    rec = route_ref[...]
    lane = lax.broadcasted_iota(jnp.int32, rec.shape, 1)
    w0 = jnp.sum(jnp.where(lane == L_W0, rec, 0.0), axis=1, keepdims=True)
    w1 = jnp.sum(jnp.where(lane == L_W1, rec, 0.0), axis=1, keepdims=True)
    res = x1_ref[...] + (gbuf[0] * w0 + gbuf[1] * w1)

    @pl.when(n < n_first)
    def _():
        oa_ref[...] = res

    @pl.when(n >= n_first)
    def _():
        ob_ref[...] = res


def _combine(dest_t, x1, route, ys, t_first):
    t = x1.shape[0]
    n_first = t_first // TM
    return pl.pallas_call(
        functools.partial(_combine_kernel, n_first),
        grid=(t // TM,),
        in_specs=[pl.BlockSpec((None, 8, TM), lambda n: (n, 0, 0), memory_space=pltpu.SMEM),
                  pl.BlockSpec((TM, D_MODEL), lambda n: (n, 0)),
                  pl.BlockSpec((TM, LANES), lambda n: (n, 0)),
                  pl.BlockSpec(memory_space=pl.ANY)],
        out_specs=[pl.BlockSpec((TM, D_MODEL), lambda n: (jnp.minimum(n, n_first - 1), 0)),
                   pl.BlockSpec((TM, D_MODEL), lambda n: (jnp.maximum(n - n_first, 0), 0))],
        out_shape=[jax.ShapeDtypeStruct((t_first, D_MODEL), F32),
                   jax.ShapeDtypeStruct((t - t_first, D_MODEL), F32)],
        scratch_shapes=[pltpu.VMEM((2, TM, D_MODEL), F32), pltpu.SemaphoreType.DMA],
        compiler_params=pltpu.CompilerParams(
            dimension_semantics=("arbitrary",), vmem_limit_bytes=VMEM_LIMIT_BYTES),
        name="moe_combine",
    )(dest_t, x1, route, ys)


def kernel(x_prompt, x_sample, cache_kv_w128, cache_kv_w512, cache_kv_w2048, state_pool, norm1_w, w_in, q_norm_w, k_norm_w, pool_w, pool_scale, w_branch_pool, w_branch_attn, w_out, norm2_w, w_router_group, b_router_group, w_router_expert, b_router_expert, w_expert_gate, w_expert_up, w_expert_down):
    assert x_prompt.shape[0] == 1 and norm1_w.shape[0] == 1
    layer = 0
    s_p = x_prompt.shape[1]
    n_b, n_t, _ = x_sample.shape
    t_s = n_b * n_t
    t_all = s_p + t_s
    caches = (cache_kv_w128[layer], cache_kv_w512[layer], cache_kv_w2048[layer])

    w_bf = w_in[layer].astype(BF16)
    qw_row = jnp.tile(q_norm_w[layer][:, None, :], (1, N_HEADS, 1)).reshape(-1) * (HEAD_DIM ** -0.5)
    kw_row = jnp.tile(k_norm_w[layer][:, None, :], (1, N_HEADS, 1)).reshape(-1)
    poolw_bf = pool_w[layer].astype(BF16)
    wbp, wba, wo = (w.astype(BF16) for w in (w_branch_pool[layer], w_branch_attn[layer], w_out[layer]))
    wr_hi, wr_lo, br = _router_weights(w_router_group[layer], b_router_group[layer],
                                       w_router_expert[layer], b_router_expert[layer])

    ya_p, gates_p, q_r, k_r, v_r, kvt, ut = _proj_prompt(
        x_prompt[0], norm1_w[layer], w_bf, qw_row, kw_row, poolw_bf, pool_scale[layer])
    yb_p = _attn_prompt(q_r, k_r, v_r)

    xs_t = jnp.transpose(x_sample, (1, 0, 2)).reshape(t_s, D_MODEL)
    state_t = jnp.transpose(state_pool[layer], (1, 0, 2))
    ya_s, gates_s, q_s, k_s, v_s, u_s, kt_s, vt_s = _proj_sample(
        xs_t, x_sample.reshape(t_s, D_MODEL), norm1_w[layer], w_bf, qw_row, kw_row, poolw_bf,
        pool_scale[layer], state_t, n_b, n_t)
    split = lambda a: a.reshape(n_b, n_t, N_DIL, N_HEADS, HEAD_DIM)
    per_head = lambda a: jnp.transpose(split(a), (0, 3, 2, 1, 4))
    new_t = jnp.stack([kt_s, vt_s]).reshape(2, N_DIL, N_HEADS, HEAD_DIM, t_s)
    caches_t = [jnp.transpose(c, (0, 2, 3, 4, 1)) for c in caches]
    o_s, *kv_t = _attn_roll(per_head(q_s), per_head(k_s), per_head(v_s), new_t, caches_t)
    yb_s = jnp.transpose(o_s, (2, 0, 1, 3)).reshape(t_s, ATTN_WIDTH).astype(BF16)
    kv_s = [jnp.transpose(c, (0, 4, 1, 2, 3)) for c in kv_t]

    x1, h, route, cnt, xs = _merge_route(x_prompt[0], xs_t, ya_p, ya_s, yb_p, yb_s, gates_p, gates_s,
                                     wbp, wba, wo, norm2_w[layer], wr_hi, wr_lo, br)

    n_rows = _moe_blocks(t_all) * MOE_ROWS
    dest, blk = _plan(cnt, route, n_rows)
    dest_t = jnp.transpose(dest[:, :8].reshape(t_all // TM, TM, 8), (0, 2, 1))
    blk_t = jnp.transpose(blk[:, :2])
    xs = _dispatch(dest_t, h, xs)
    ys = _experts(blk_t, xs, w_expert_gate[layer], w_expert_up[layer], w_expert_down[layer])
    y_p, y_s = _combine(dest_t, x1, route, ys, s_p)

    y_prompt = y_p[None]
    y_sample = jnp.transpose(y_s.reshape(n_t, n_b, D_MODEL), (1, 0, 2))
    pool_prompt = ut[1:][None, None]
    u_new = jnp.transpose(u_s.reshape(n_t, n_b, POOL_WIDTH), (1, 0, 2))
    pool_sample = jnp.concatenate([state_pool[layer][:, n_t:], u_new], axis=1)[None]
    kv_t = kvt.reshape(2, N_DIL, N_HEADS, HEAD_DIM, kvt.shape[1])
    outs = [y_prompt, y_sample, pool_prompt, pool_sample]
    for g, (win, _) in enumerate(DILATION_PAIRS):
        keep = min(win, s_p)
        kv_g = kv_t[:, g, :, :, kv_t.shape[-1] - keep:]
        outs.append(jnp.transpose(kv_g, (3, 0, 1, 2))[None, None])
        outs.append(kv_s[g][None])
    return tuple(outs)
```

```python
import functools

import numpy as np
import jax
import jax.numpy as jnp
from jax import lax
from jax.experimental import pallas as pl
from jax.experimental.pallas import tpu as pltpu

F32 = jnp.float32
BF16 = jnp.bfloat16

D_MODEL = 1024
PAST_LEN = 8192
POOL_WIDTH = 512
POOL_WINDOWS = (2, 4, 8, 16)
POOL_GROUP_DIM = 128
POOL_STATE = 15
HEAD_DIM = 64
N_HEADS = 8
DILATION_PAIRS = ((128, 1), (512, 4), (2048, 16))
N_DIL = 3
ATTN_WIDTH = 512
QKV_WIDTH = 1536
ROT_DIM = 16
ROPE_THETA = 500000.0
N_BRANCH = 2
IN_COLS = POOL_WIDTH + 3 * QKV_WIDTH + N_BRANCH * D_MODEL
N_EXPERT_GROUPS = 4
EXPERTS_PER_GROUP = 8
N_EXPERTS = 32
D_EXPERT = 512
EPS = 1e-6

LANES = 128
VMEM_LIMIT_BYTES = 56 * 1024 * 1024
ATTN_VMEM_LIMIT_BYTES = 60 * 1024 * 1024

C_U = 0
C_Q = POOL_WIDTH
C_K = C_Q + QKV_WIDTH
C_V = C_K + QKV_WIDTH
C_G = C_V + QKV_WIDTH

N_CLASS = 16
TI = 256
ROWS_PER_CLASS = TI // N_CLASS
NEG = -1e30
LOG2_E = 1.4426950408889634


def _class_order():
    return [(c // 4) + 4 * (c % 4) for c in range(N_CLASS)]


def _const_spec(shape, single_buffer=False):
    nd = len(shape)
    kw = {}
    if single_buffer:
        kw["pipeline_mode"] = pl.Buffered(1)
    return pl.BlockSpec(shape, lambda *_: (0,) * nd, **kw)


def _rms(x, w):
    ms = jnp.mean(x * x, axis=-1, keepdims=True)
    return x * lax.rsqrt(ms + EPS) * w


def _head_norm_rope(y, wrow, seg, segt, cos, sina, sinb):
    sq = (y * y).astype(BF16)
    ssum = jnp.dot(sq, seg, preferred_element_type=F32)
    r = lax.rsqrt(ssum * (1.0 / HEAD_DIM) + EPS)
    r_hi = r.astype(BF16)
    r_lo = (r - r_hi.astype(F32)).astype(BF16)
    rexp = jnp.dot(jnp.concatenate([r_hi, r_lo], axis=1), segt, preferred_element_type=F32)
    yn = y * rexp * wrow
    outs = []
    for j in range(QKV_WIDTH // LANES):
        c = yn[:, j * LANES:(j + 1) * LANES]
        outs.append(c * cos + pltpu.roll(c, LANES - ROT_DIM // 2, 1) * sina
                    + pltpu.roll(c, ROT_DIM // 2, 1) * sinb)
    return jnp.concatenate(outs, axis=1)


def _pool_mix_out(pooled, poolw_ref, pool_scale):
    outs = []
    for g in range(len(POOL_WINDOWS)):
        outs.append(jnp.dot(pooled[g].astype(BF16), poolw_ref[g], preferred_element_type=F32))
    return jnp.concatenate(outs, axis=1) * pool_scale


def _split3(x):
    hi = x.astype(BF16)
    r1 = x - hi.astype(F32)
    mid = r1.astype(BF16)
    lo = (r1 - mid.astype(F32)).astype(BF16)
    return hi, mid, lo


def _proj_prompt_kernel(tail_tiles, x_ref, n1w_ref, w_ref, qw_ref, kw_ref, cosr_ref, sinr_ref, cosn_ref,
                        sinn_ref, rotm_ref, perm_ref, permt_ref, seg_ref, segt_ref, poolw_ref, pools_ref,
                        ya_ref, gates_ref, q_ref, k_ref, v_ref, kvt_ref, ut_ref,
                        uext_ref):
    n = pl.program_id(0)
    xn = _rms(x_ref[...], n1w_ref[...]).astype(BF16)

    u = jnp.dot(xn, w_ref[:, C_U:C_U + POOL_WIDTH], preferred_element_type=F32)

    @pl.when(n == 0)
    def _():
        uext_ref[0:16, :] = jnp.zeros((16, POOL_WIDTH), F32)

    uext_ref[16:16 + TI, :] = u
    pos = n * TI + lax.broadcasted_iota(jnp.int32, (TI, 1), 0)
    pooled = []
    for g, w in enumerate(POOL_WINDOWS):
        lanes = slice(g * POOL_GROUP_DIM, (g + 1) * POOL_GROUP_DIM)
        s = uext_ref[16:16 + TI, lanes]
        for m in range(1, w):
            s = s + uext_ref[16 - m:16 - m + TI, lanes]
        cnt = jnp.minimum(pos + 1, w).astype(F32)
        pooled.append(s / cnt - u[:, lanes])
    ya_ref[...] = _pool_mix_out(pooled, poolw_ref, pools_ref[...]).astype(BF16)
    ut_ref[...] = u[TI - 16:, :]
    uext_ref[0:16, :] = uext_ref[TI:TI + 16, :]

    gl = jnp.dot(xn, w_ref[:, C_G:C_G + N_BRANCH * D_MODEL], preferred_element_type=F32)
    gates_ref[...] = jax.nn.sigmoid(gl).astype(BF16)

    xp = jnp.dot(perm_ref[...], xn, preferred_element_type=F32).astype(BF16)
    cn, sn = cosn_ref[pl.ds(n, 1), :], sinn_ref[pl.ds(n, 1), :]
    cr, sr = cosr_ref[...], sinr_ref[...]
    cos = cn * cr - sn * sr
    sin = sn * cr + cn * sr
    sina, sinb = sin * rotm_ref[0:1, :], sin * rotm_ref[1:2, :]
    seg, segt = seg_ref[...], segt_ref[...]
    shp = (N_CLASS, ROWS_PER_CLASS, QKV_WIDTH)
    yq = jnp.dot(xp, w_ref[:, C_Q:C_Q + QKV_WIDTH], preferred_element_type=F32)
    q_ref[...] = _head_norm_rope(yq, qw_ref[...], seg, segt, cos, sina, sinb).astype(BF16).reshape(shp)
    yk = jnp.dot(xp, w_ref[:, C_K:C_K + QKV_WIDTH], preferred_element_type=F32)
    kr = _head_norm_rope(yk, kw_ref[...], seg, segt, cos, sina, sinb)
    k_ref[...] = kr.astype(BF16).reshape(shp)
    yv = jnp.dot(xp, w_ref[:, C_V:C_V + QKV_WIDTH], preferred_element_type=F32)
    v_ref[...] = yv.astype(BF16).reshape(shp)

    @pl.when(n >= pl.num_programs(0) - tail_tiles)
    def _():
        def natural_t(val):
            nat = sum(jnp.dot(permt_ref[...], part, preferred_element_type=F32) for part in _split3(val))
            return nat.T
        kvt_ref[0:QKV_WIDTH, :] = natural_t(kr)
        kvt_ref[QKV_WIDTH:2 * QKV_WIDTH, :] = natural_t(yv)


def _rope_tables(pos):
    half = ROT_DIM // 2
    inv = ROPE_THETA ** (-(jnp.arange(half, dtype=F32) * 2.0 / ROT_DIM))
    ang = pos.astype(F32)[:, None] * inv[None, :]
    c, s = jnp.cos(ang), jnp.sin(ang)
    n = pos.shape[0]
    ones = jnp.ones((n, HEAD_DIM - ROT_DIM), F32)
    zeros = jnp.zeros((n, HEAD_DIM - ROT_DIM), F32)
    z8 = jnp.zeros((n, half), F32)
    cos64 = jnp.concatenate([c, c, ones], axis=1)
    sina64 = jnp.concatenate([-s, z8, zeros], axis=1)
    sinb64 = jnp.concatenate([z8, s, zeros], axis=1)
    rep = lambda t: jnp.concatenate([t, t], axis=1)
    return rep(cos64), rep(sina64), rep(sinb64)


def _rope_lane_freq():
    half = ROT_DIM // 2
    inv = ROPE_THETA ** (-(np.arange(half, dtype=np.float64) * 2.0 / ROT_DIM))
    d = np.arange(LANES) % HEAD_DIM
    freq = np.where(d < ROT_DIM, inv[d % half], 0.0)
    mask = np.zeros((8, LANES))
    mask[0] = np.where(d < half, -1.0, 0.0)
    mask[1] = np.where((d >= half) & (d < ROT_DIM), 1.0, 0.0)
    return freq, mask


def _seg_mats():
    h = np.arange(QKV_WIDTH) // HEAD_DIM
    seg = (h[:, None] == np.arange(LANES)[None, :]).astype(np.float32)
    return jnp.asarray(seg, BF16), jnp.asarray(np.concatenate([seg.T, seg.T], axis=0), BF16)


def _perm_mat():
    cls = _class_order()
    p = np.zeros((TI, TI), np.float32)
    for c in range(N_CLASS):
        for j in range(ROWS_PER_CLASS):
            p[c * ROWS_PER_CLASS + j, N_CLASS * j + cls[c]] = 1.0
    return p


def _proj_prompt(x, n1w, w_bf, qw_row, kw_row, poolw_bf, pool_scale):
    s = x.shape[0]
    nt = s // TI
    cls = np.asarray(_class_order())
    perm = _perm_mat()
    j = np.arange(ROWS_PER_CLASS)
    inner = (N_CLASS * j[None, :] + cls[:, None]).reshape(-1).astype(np.float64)
    freq, rot_mask = _rope_lane_freq()
    ang_r = inner[:, None] * freq[None, :]
    ang_n = (np.arange(nt, dtype=np.float64) * TI)[:, None] * freq[None, :]
    f32c = lambda a: jnp.asarray(a.astype(np.float32))
    cosr, sinr, cosn, sinn = f32c(np.cos(ang_r)), f32c(np.sin(ang_r)), f32c(np.cos(ang_n)), f32c(np.sin(ang_n))
    rotm = f32c(rot_mask)
    seg, segt = _seg_mats()
    n_i = s // N_CLASS
    tail_tokens = min(DILATION_PAIRS[-1][0], s)
    tail_tiles = tail_tokens // TI
    row = lambda a: a.reshape(1, -1)
    tile = lambda n: (n, 0)
    rblk = lambda n: (0, n, 0)
    tblk = lambda n: (0, jnp.maximum(n - (nt - tail_tiles), 0))
    outs = pl.pallas_call(
        functools.partial(_proj_prompt_kernel, tail_tiles),
        grid=(nt,),
        in_specs=[
            pl.BlockSpec((TI, D_MODEL), tile),
            _const_spec((1, D_MODEL)),
            _const_spec((D_MODEL, IN_COLS), single_buffer=True),
            _const_spec((1, QKV_WIDTH)),
            _const_spec((1, QKV_WIDTH)),
            _const_spec((TI, LANES)), _const_spec((TI, LANES)),
            _const_spec((nt, LANES)), _const_spec((nt, LANES)), _const_spec((8, LANES)),
            _const_spec((TI, TI)), _const_spec((TI, TI)),
            _const_spec((QKV_WIDTH, LANES)),
            _const_spec((2 * LANES, QKV_WIDTH)),
            _const_spec((len(POOL_WINDOWS), POOL_GROUP_DIM, POOL_GROUP_DIM)),
            _const_spec((1, POOL_WIDTH)),
        ],
        out_specs=[
            pl.BlockSpec((TI, POOL_WIDTH), tile),
            pl.BlockSpec((TI, N_BRANCH * D_MODEL), tile),
            pl.BlockSpec((N_CLASS, ROWS_PER_CLASS, QKV_WIDTH), rblk),
            pl.BlockSpec((N_CLASS, ROWS_PER_CLASS, QKV_WIDTH), rblk),
            pl.BlockSpec((N_CLASS, ROWS_PER_CLASS, QKV_WIDTH), rblk),
            pl.BlockSpec((2 * QKV_WIDTH, TI), tblk),
            _const_spec((16, POOL_WIDTH)),
        ],
        out_shape=[
            jax.ShapeDtypeStruct((s, POOL_WIDTH), BF16),
            jax.ShapeDtypeStruct((s, N_BRANCH * D_MODEL), BF16),
            jax.ShapeDtypeStruct((N_CLASS, n_i, QKV_WIDTH), BF16),
            jax.ShapeDtypeStruct((N_CLASS, n_i, QKV_WIDTH), BF16),
            jax.ShapeDtypeStruct((N_CLASS, n_i, QKV_WIDTH), BF16),
            jax.ShapeDtypeStruct((2 * QKV_WIDTH, tail_tokens), F32),
            jax.ShapeDtypeStruct((16, POOL_WIDTH), F32),
        ],
        scratch_shapes=[pltpu.VMEM((16 + TI, POOL_WIDTH), F32)],
        compiler_params=pltpu.CompilerParams(
            dimension_semantics=("arbitrary",), vmem_limit_bytes=VMEM_LIMIT_BYTES),
        name="proj_prompt",
    )(x, row(n1w), w_bf, row(qw_row), row(kw_row), cosr, sinr, cosn, sinn, rotm,
      jnp.asarray(perm, BF16), jnp.asarray(perm.T, BF16), seg, segt, poolw_bf, row(pool_scale))
    return outs


AT_I = 128
N_KEYS_BACK = 128
BLOCKS_PER_TRIP = 2


def _attn_update(q, k, v, bias, m_old, l_old, acc_old):
    nq, nk = q.shape[0], k.shape[0]
    assert nk % LANES == 0
    lane = lax.broadcasted_iota(jnp.int32, (nq, LANES), 1)
    first = lane < HEAD_DIM
    bias2 = jnp.concatenate([bias, bias], axis=0)
    ones = jnp.ones((nk, LANES), BF16)
    m_out, l_out, a_out = [], [], []
    for hp in range(N_HEADS // 2):
        sl = slice(hp * LANES, (hp + 1) * LANES)
        qp, kp, vp = q[:, sl], k[:, sl], v[:, sl]
        lo, ao = l_old[:, sl], acc_old[:, sl]
        mo2 = jnp.concatenate([m_old[:, (2 * hp) * LANES:(2 * hp + 1) * LANES],
                               m_old[:, (2 * hp + 1) * LANES:(2 * hp + 2) * LANES]], axis=0)
        zero = jnp.zeros_like(qp)
        q2 = jnp.concatenate([jnp.where(first, qp, zero), jnp.where(first, zero, qp)], axis=0)
        s2 = lax.dot_general(q2, kp, (((1,), (1,)), ((), ())), preferred_element_type=F32) + bias2
        mn2 = jnp.maximum(mo2, jnp.max(s2, axis=1, keepdims=True))
        p2 = jnp.exp2(s2 - jnp.concatenate([mn2] * (nk // LANES), axis=1)).astype(BF16)
        pv2 = jnp.dot(p2, jnp.concatenate([vp, ones], axis=1), preferred_element_type=F32)
        pv = jnp.where(first, pv2[:nq, :LANES], pv2[nq:, :LANES])
        lc = jnp.where(first, pv2[:nq, LANES:], pv2[nq:, LANES:])
        alpha2 = jnp.exp2(mo2 - mn2)
        alpha = jnp.where(first, alpha2[:nq], alpha2[nq:])
        m_out += [mn2[:nq], mn2[nq:]]
        l_out.append(alpha * lo + lc)
        a_out.append(alpha * ao + pv)
    cat = lambda xs: jnp.concatenate(xs, axis=1)
    return cat(m_out), cat(l_out), cat(a_out)


def _attn_prompt_kernel(q_ref, kc_ref, kp_ref, vc_ref, vp_ref, b0_ref, b1_ref, b2_ref,
                        r0_ref, r1_ref, r2_ref, o_ref,
                        kcat, vcat, acc, m_s, l_s, ost):
    n = pl.program_id(0)
    g = pl.program_id(1)
    i0 = n * AT_I
    kcat[:, 0:AT_I, :] = kp_ref[...]
    kcat[:, AT_I:2 * AT_I, :] = kc_ref[...]
    vcat[:, 0:AT_I, :] = vp_ref[...]
    vcat[:, AT_I:2 * AT_I, :] = vc_ref[...]

    def key_bias(b_ref, r_ref, base):
        return b_ref[...] + jnp.where(base + r_ref[...] >= 0, 0.0, NEG)

    @pl.when(g == 0)
    def _():
        m_s[...] = jnp.full(m_s.shape, NEG, F32)
        l_s[...] = jnp.zeros(l_s.shape, F32)
        acc[...] = jnp.zeros(acc.shape, F32)

        def body(s, carry):
            qo = pl.multiple_of(16 * s, 16)
            ko = pl.multiple_of(AT_I - 16 + 16 * s, 16)
            q16 = [q_ref[c, pl.ds(qo, 16), :] for c in range(N_CLASS)]
            k32 = [kcat[c, pl.ds(ko, 32), :] for c in range(N_CLASS)]
            v32 = [vcat[c, pl.ds(ko, 32), :] for c in range(N_CLASS)]
            cat = lambda f: jnp.concatenate([f(c) for c in range(N_CLASS)], axis=0)
            work = []
            for half in range(2):
                qh = pl.multiple_of(qo + 8 * half, 8)
                q = cat(lambda c: q16[c][8 * half:8 * half + 8])
                k = cat(lambda c: k32[c][8 + 8 * half:24 + 8 * half])
                v = cat(lambda c: v32[c][8 + 8 * half:24 + 8 * half])
                mo = cat(lambda c: m_s[c, pl.ds(qh, 8), :])
                lo = cat(lambda c: l_s[c, pl.ds(qh, 8), :])
                ao = cat(lambda c: acc[c, pl.ds(qh, 8), :])
                bias = key_bias(b0_ref, r0_ref, i0 + 16 * s + 8 * half)
                work.append((qh, (q, k, v, bias, mo, lo, ao)))
            done = [(qh, _attn_update(*args)) for qh, args in work]
            for qh, (mn, ln, an) in done:
                for c in range(N_CLASS):
                    m_s[c, pl.ds(qh, 8), :] = mn[8 * c:8 * c + 8]
                    l_s[c, pl.ds(qh, 8), :] = ln[8 * c:8 * c + 8]
                    acc[c, pl.ds(qh, 8), :] = an[8 * c:8 * c + 8]
            return carry

        lax.fori_loop(0, AT_I // 16, body, 0)

    @pl.when(g == 1)
    def _():
        def load(r4, s):
            qo = pl.multiple_of(32 * s, 32)
            ko = pl.multiple_of(AT_I - 32 + 32 * s, 32)
            cat = lambda f: jnp.concatenate([f(4 * r4 + a) for a in range(4)], axis=0)
            q = cat(lambda c: q_ref[c, pl.ds(qo, 32), :])
            k = cat(lambda c: kcat[c, pl.ds(ko, 64), :])
            v = cat(lambda c: vcat[c, pl.ds(ko, 64), :])
            mo = cat(lambda c: m_s[c, pl.ds(qo, 32), :])
            lo = cat(lambda c: l_s[c, pl.ds(qo, 32), :])
            ao = cat(lambda c: acc[c, pl.ds(qo, 32), :])
            bias = key_bias(b1_ref, r1_ref, i0 + 32 * s)
            return q, k, v, bias, mo, lo, ao

        def store(r4, s, mn, ln, an):
            qo = pl.multiple_of(32 * s, 32)
            for a in range(4):
                c = 4 * r4 + a
                m_s[c, pl.ds(qo, 32), :] = mn[32 * a:32 * a + 32]
                l_s[c, pl.ds(qo, 32), :] = ln[32 * a:32 * a + 32]
                acc[c, pl.ds(qo, 32), :] = an[32 * a:32 * a + 32]

        def body(idx, carry):
            per_class = 4 // BLOCKS_PER_TRIP
            where = [(idx // per_class, BLOCKS_PER_TRIP * (idx % per_class) + j) for j in range(BLOCKS_PER_TRIP)]
            work = [load(r4, s) for r4, s in where]
            done = [_attn_update(*args) for args in work]
            for (r4, s), res in zip(where, done):
                store(r4, s, *res)
            return carry

        lax.fori_loop(0, 16 // BLOCKS_PER_TRIP, body, 0)

    @pl.when(g == 2)
    def _():
        bias = key_bias(b2_ref, r2_ref, i0)

        def body(cc, carry):
            cs = [BLOCKS_PER_TRIP * cc + j for j in range(BLOCKS_PER_TRIP)]
            work = [(q_ref[c], kcat[c], vcat[c], bias, m_s[c], l_s[c], acc[c]) for c in cs]
            done = [_attn_update(*args) for args in work]
            for c, (mn, ln, an) in zip(cs, done):
                m_s[c] = mn
                l_s[c] = ln
                acc[c] = an
            return carry

        lax.fori_loop(0, N_CLASS // BLOCKS_PER_TRIP, body, 0)

        cls = _class_order()
        for c in range(N_CLASS):
            o = acc[c] / l_s[c]
            for jc in range(ATTN_WIDTH // LANES):
                ost[jc, pl.ds(cls[c], AT_I, stride=N_CLASS), :] = o[:, jc * LANES:(jc + 1) * LANES]
        o_ref[...] = jnp.concatenate([ost[jc] for jc in range(ATTN_WIDTH // LANES)], axis=1).astype(BF16)


def _attn_bias_tables():
    cls = np.asarray(_class_order())

    def bias(diff):
        return np.where((diff >= 0) & (diff <= N_KEYS_BACK), 0.0, NEG).astype(np.float32)

    c, r = np.divmod(np.arange(128), 8)
    c2, r2 = np.divmod(np.arange(256), 16)
    d0 = 16 * (r[:, None] - r2[None, :] + 8) + cls[c][:, None] - cls[c2][None, :]
    rel0 = (r2 - 8).astype(np.int32)[None, :]
    a, r = np.divmod(np.arange(128), 32)
    a2, r2 = np.divmod(np.arange(256), 64)
    d1 = 4 * (r[:, None] - r2[None, :] + 32) + a[:, None] - a2[None, :]
    rel1 = (r2 - 32).astype(np.int32)[None, :]
    r = np.arange(128)
    r2 = np.arange(256)
    d2 = r[:, None] - r2[None, :] + 128
    rel2 = (r2 - 128).astype(np.int32)[None, :]
    return (jnp.asarray(bias(d0)), jnp.asarray(bias(d1)), jnp.asarray(bias(d2)),
            jnp.asarray(rel0), jnp.asarray(rel1), jnp.asarray(rel2))


def _attn_prompt(q_r, k_r, v_r):
    n_i = q_r.shape[1]
    s = n_i * N_CLASS
    nsteps = n_i // AT_I
    b0, b1, b2, r0, r1, r2 = _attn_bias_tables()
    blk = (N_CLASS, AT_I, ATTN_WIDTH)
    cur = lambda n, g: (0, n, g)
    prev = lambda n, g: (0, jnp.maximum(n - 1, 0), g)
    return pl.pallas_call(
        _attn_prompt_kernel,
        grid=(nsteps, N_DIL),
        in_specs=[
            pl.BlockSpec(blk, cur), pl.BlockSpec(blk, cur), pl.BlockSpec(blk, prev),
            pl.BlockSpec(blk, cur), pl.BlockSpec(blk, prev),
            _const_spec(b0.shape), _const_spec(b1.shape), _const_spec(b2.shape),
            _const_spec(r0.shape), _const_spec(r1.shape), _const_spec(r2.shape),
        ],
        out_specs=pl.BlockSpec((AT_I * N_CLASS, ATTN_WIDTH), lambda n, g: (n, 0)),
        out_shape=jax.ShapeDtypeStruct((s, ATTN_WIDTH), BF16),
        scratch_shapes=[
            pltpu.VMEM((N_CLASS, 2 * AT_I, ATTN_WIDTH), BF16),
            pltpu.VMEM((N_CLASS, 2 * AT_I, ATTN_WIDTH), BF16),
            pltpu.VMEM((N_CLASS, AT_I, ATTN_WIDTH), F32),
            pltpu.VMEM((N_CLASS, AT_I, N_HEADS * LANES), F32),
            pltpu.VMEM((N_CLASS, AT_I, ATTN_WIDTH), F32),
            pltpu.VMEM((ATTN_WIDTH // LANES, AT_I * N_CLASS, LANES), F32),
        ],
        compiler_params=pltpu.CompilerParams(
            dimension_semantics=("arbitrary", "arbitrary"), vmem_limit_bytes=ATTN_VMEM_LIMIT_BYTES),
        name="attn_prompt",
    )(q_r, k_r, k_r, v_r, v_r, b0, b1, b2, r0, r1, r2)


def _proj_sample_kernel(n_b, n_t, x_ref, xb_ref, n1w_ref, w_ref, qw_ref, kw_ref, cos_ref, sina_ref, sinb_ref,
                        seg_ref, segt_ref, poolw_ref, pools_ref, state_ref,
                        ya_ref, gates_ref, q_ref, k_ref, v_ref, u_ref, kt_ref, vt_ref):
    xn = _rms(x_ref[...], n1w_ref[...]).astype(BF16)
    u = jnp.dot(xn, w_ref[:, C_U:C_U + POOL_WIDTH], preferred_element_type=F32)
    u_ref[...] = u

    def ext(j, lanes):
        if j < POOL_STATE:
            return state_ref[j, :, lanes]
        return u[(j - POOL_STATE) * n_b:(j - POOL_STATE + 1) * n_b, lanes]

    pooled = []
    for g, w in enumerate(POOL_WINDOWS):
        lanes = slice(g * POOL_GROUP_DIM, (g + 1) * POOL_GROUP_DIM)
        rows = []
        for i in range(n_t):
            s = ext(POOL_STATE + i, lanes)
            for m in range(1, w):
                s = s + ext(POOL_STATE + i - m, lanes)
            cnt = float(min(PAST_LEN + i + 1, w))
            rows.append(s / cnt - ext(POOL_STATE + i, lanes))
        pooled.append(jnp.concatenate(rows, axis=0))
    ya_ref[...] = _pool_mix_out(pooled, poolw_ref, pools_ref[...]).astype(BF16)

    gl = jnp.dot(xn, w_ref[:, C_G:C_G + N_BRANCH * D_MODEL], preferred_element_type=F32)
    gates_ref[...] = jax.nn.sigmoid(gl).astype(BF16)

    xnb = _rms(xb_ref[...], n1w_ref[...]).astype(BF16)
    cos, sina, sinb = cos_ref[...], sina_ref[...], sinb_ref[...]
    seg, segt = seg_ref[...], segt_ref[...]
    yq = jnp.dot(xnb, w_ref[:, C_Q:C_Q + QKV_WIDTH], preferred_element_type=F32)
    q_ref[...] = _head_norm_rope(yq, qw_ref[...], seg, segt, cos, sina, sinb)
    yk = jnp.dot(xnb, w_ref[:, C_K:C_K + QKV_WIDTH], preferred_element_type=F32)
    kr = _head_norm_rope(yk, kw_ref[...], seg, segt, cos, sina, sinb)
    k_ref[...] = kr
    kt_ref[...] = kr.T
    yv = jnp.dot(xnb, w_ref[:, C_V:C_V + QKV_WIDTH], preferred_element_type=F32)
    v_ref[...] = yv
    vt_ref[...] = yv.T


def _proj_sample(x, xb, n1w, w_bf, qw_row, kw_row, poolw_bf, pool_scale, state_t, n_b, n_t):
    rows = n_b * n_t
    pos = PAST_LEN + jnp.arange(rows, dtype=jnp.int32) % n_t
    cos, sina, sinb = _rope_tables(pos)
    seg, segt = _seg_mats()
    row = lambda a: a.reshape(1, -1)
    args = (x, xb, row(n1w), w_bf, row(qw_row), row(kw_row), cos, sina, sinb, seg, segt,
            poolw_bf, row(pool_scale), state_t)
    f32_qkv = jax.ShapeDtypeStruct((rows, QKV_WIDTH), F32)
    out_shape = [
        jax.ShapeDtypeStruct((rows, POOL_WIDTH), BF16),
        jax.ShapeDtypeStruct((rows, N_BRANCH * D_MODEL), BF16),
        f32_qkv, f32_qkv, f32_qkv,
        jax.ShapeDtypeStruct((rows, POOL_WIDTH), F32),
        jax.ShapeDtypeStruct((QKV_WIDTH, rows), F32),
        jax.ShapeDtypeStruct((QKV_WIDTH, rows), F32),
    ]
    return pl.pallas_call(
        functools.partial(_proj_sample_kernel, n_b, n_t),
        grid=(1,),
        in_specs=[_const_spec(a.shape) for a in args],
        out_specs=[_const_spec(o.shape) for o in out_shape],
        out_shape=out_shape,
        compiler_params=pltpu.CompilerParams(
            dimension_semantics=("arbitrary",), vmem_limit_bytes=VMEM_LIMIT_BYTES),
        name="proj_sample",
    )(*args)


HEADS_PER_STEP = 8


def _attn_roll_kernel(n_t, q_ref, kn_ref, vn_ref, newt_ref, c0_ref, c1_ref, c2_ref,
                      b0_ref, b1_ref, b2_ref, bn_ref, o_ref, o0_ref, o1_ref, o2_ref):
    caches = (c0_ref, c1_ref, c2_ref)
    outs = (o0_ref, o1_ref, o2_ref)
    biases = (b0_ref, b1_ref, b2_ref)
    nt_contract = (((1,), (1,)), ((), ()))
    col = pl.program_id(1) * n_t
    new_tile = pl.multiple_of((col // LANES) * LANES, LANES)
    new_shift = (LANES - n_t) - col % LANES
    for hh in range(HEADS_PER_STEP):
        scores, values = [], []
        for g in range(N_DIL):
            q = q_ref[hh, g].astype(BF16)
            kc = caches[g][0, hh].astype(BF16)
            scores.append(jnp.dot(q, kc, preferred_element_type=F32) + biases[g][...])
            kn = kn_ref[hh, g].astype(BF16)
            scores.append(lax.dot_general(q, kn, nt_contract, preferred_element_type=F32) + bn_ref[g])
            values.append(caches[g][1, hh].astype(BF16))
            values.append(vn_ref[hh, g].astype(BF16))
        m = functools.reduce(jnp.maximum, [jnp.max(s, axis=1, keepdims=True) for s in scores])
        l = jnp.zeros_like(m)
        o = jnp.zeros((n_t, HEAD_DIM), F32)
        for idx, (s, v) in enumerate(zip(scores, values)):
            p = jnp.exp(s - m)
            l = l + jnp.sum(p, axis=1, keepdims=True)
            pb = p.astype(BF16)
            if idx % 2 == 0:
                o = o + lax.dot_general(pb, v, nt_contract, preferred_element_type=F32)
            else:
                o = o + jnp.dot(pb, v, preferred_element_type=F32)
        o_ref[hh] = o / l

        for g in range(N_DIL):
            length = caches[g].shape[-1]
            lane = lax.broadcasted_iota(jnp.int32, (HEAD_DIM, LANES), 1)
            for kv in range(2):
                rolled = pltpu.roll(caches[g][kv, hh], length - n_t, 1)
                new = pltpu.roll(newt_ref[kv, g, hh, :, pl.ds(new_tile, LANES)], new_shift, 1)
                if length > LANES:
                    outs[g][kv, hh, :, 0:length - LANES] = rolled[:, 0:length - LANES]
                outs[g][kv, hh, :, length - LANES:length] = jnp.where(
                    lane >= LANES - n_t, new, rolled[:, length - LANES:length])


def _attn_roll_bias(n_t):
    i = np.arange(n_t)
    bias, bias_new = [], []
    for win, dil in DILATION_PAIRS:
        length = min(win, PAST_LEN)
        delta = length + i[:, None] - np.arange(length)[None, :]
        ok = (delta % dil == 0) & (delta // dil >= 1) & (delta // dil <= win // dil)
        bias.append(jnp.asarray(np.where(ok, 0.0, NEG).astype(np.float32)))
        dn = i[:, None] - i[None, :]
        okn = (dn >= 0) & (dn % dil == 0) & (dn // dil <= win // dil)
        bias_new.append(np.where(okn, 0.0, NEG).astype(np.float32))
    return bias, jnp.asarray(np.stack(bias_new))


def _attn_roll(q_h, kn_h, vn_h, new_t, caches_t):
    n_b, _, _, n_t, _ = q_h.shape
    bias, bias_new = _attn_roll_bias(n_t)
    hs = HEADS_PER_STEP
    qspec = pl.BlockSpec((None, hs, N_DIL, n_t, HEAD_DIM), lambda h, b: (b, h, 0, 0, 0))
    cspec = lambda c: pl.BlockSpec((None, 2, hs, HEAD_DIM, c.shape[-1]), lambda h, b: (b, 0, h, 0, 0))
    return pl.pallas_call(
        functools.partial(_attn_roll_kernel, n_t),
        grid=(N_HEADS // hs, n_b),
        in_specs=[qspec, qspec, qspec,
                  pl.BlockSpec((2, N_DIL, hs, HEAD_DIM, n_t * n_b), lambda h, b: (0, 0, h, 0, 0))]
                 + [cspec(c) for c in caches_t]
                 + [_const_spec(b.shape) for b in bias] + [_const_spec(bias_new.shape)],
        out_specs=[pl.BlockSpec((None, hs, n_t, HEAD_DIM), lambda h, b: (b, h, 0, 0))]
                  + [cspec(c) for c in caches_t],
        out_shape=[jax.ShapeDtypeStruct((n_b, N_HEADS, n_t, HEAD_DIM), F32)]
                  + [jax.ShapeDtypeStruct(c.shape, c.dtype) for c in caches_t],
        compiler_params=pltpu.CompilerParams(
            dimension_semantics=("arbitrary", "arbitrary"), vmem_limit_bytes=VMEM_LIMIT_BYTES),
        name="attn_roll",
    )(q_h, kn_h, vn_h, new_t, *caches_t, *bias, bias_new)


TM = 512
MOE_ROWS = 256
L_E0, L_E1, L_W0, L_W1, L_R0, L_R1 = 0, 1, 2, 3, 4, 5
L_GROUP = N_EXPERTS


def _lane_min_index(mask, lane):
    return jnp.min(jnp.where(mask, lane, float(LANES)), axis=1, keepdims=True)


def _zero_fill_step(n, xs_ref, zbuf, sem):
    slab = zbuf.shape[0]
    n_full, rem = divmod(xs_ref.shape[0], slab)
    full = lambda i: pltpu.make_async_copy(
        zbuf, xs_ref.at[pl.ds(pl.multiple_of(i * slab, MOE_ROWS), slab)], sem)
    tail = pltpu.make_async_copy(zbuf.at[pl.ds(0, max(rem, 1))],
                                 xs_ref.at[pl.ds(n_full * slab, max(rem, 1))], sem)

    @pl.when(n == 0)
    def _():
        zbuf[...] = jnp.zeros(zbuf.shape, F32)
        if rem:
            tail.start()

    @pl.when(n < n_full)
    def _():
        full(n).start()

    @pl.when(n == pl.num_programs(0) - 1)
    def _():
        for _ in range(n_full):
            full(0).wait()
        if rem:
            tail.wait()


def _merge_route_kernel(n_first, xa_ref, xb_ref, yaa_ref, yab_ref, yba_ref, ybb_ref, ga_ref, gb_ref,
                        wbp_ref, wba_ref, wo_ref, n2w_ref, wrh_ref, wrl_ref, br_ref, ltri_ref,
                        x1_ref, h_ref, route_ref, cnt_ref, xs_ref, zbuf, sem):
    n = pl.program_id(0)
    _zero_fill_step(n, xs_ref, zbuf, sem)
    first = n < n_first
    pick = lambda a, b: jnp.where(first, a[...], b[...])
    pa = jnp.dot(pick(yaa_ref, yab_ref), wbp_ref[...], preferred_element_type=F32)
    pb = jnp.dot(pick(yba_ref, ybb_ref), wba_ref[...], preferred_element_type=F32)
    gts = pick(ga_ref, gb_ref)
    merged = gts[:, :D_MODEL].astype(F32) * pa + gts[:, D_MODEL:].astype(F32) * pb
    x1 = pick(xa_ref, xb_ref) + jnp.dot(merged.astype(BF16), wo_ref[...], preferred_element_type=F32)
    x1_ref[...] = x1
    h = _rms(x1, n2w_ref[...])
    h_ref[...] = h

    h_hi = h.astype(BF16)
    h_lo = (h - h_hi.astype(F32)).astype(BF16)
    logits = (jnp.dot(h_hi, wrh_ref[...], preferred_element_type=F32)
              + jnp.dot(h_hi, wrl_ref[...], preferred_element_type=F32)
              + jnp.dot(h_lo, wrh_ref[...], preferred_element_type=F32)) + br_ref[...]
    rows = logits.shape[0]
    lane = lax.broadcasted_iota(jnp.int32, (rows, LANES), 1).astype(F32)
    is_group = (lane >= L_GROUP) & (lane < L_GROUP + N_EXPERT_GROUPS)
    gl = jnp.where(is_group, logits, NEG)
    gmax = jnp.max(gl, axis=1, keepdims=True)
    g_w = 1.0 / jnp.sum(jnp.exp(gl - gmax), axis=1, keepdims=True)
    g_sel = _lane_min_index(gl == gmax, lane) - L_GROUP
    in_group = (lane >= g_sel * EXPERTS_PER_GROUP) & (lane < (g_sel + 1.0) * EXPERTS_PER_GROUP)
    el = jnp.where(in_group, logits, NEG)
    v0 = jnp.max(el, axis=1, keepdims=True)
    e0 = _lane_min_index(el == v0, lane)
    el2 = jnp.where(lane == e0, NEG, el)
    v1 = jnp.max(el2, axis=1, keepdims=True)
    e1 = _lane_min_index(el2 == v1, lane)
    t = jnp.exp(v1 - v0)
    w0 = g_w / (1.0 + t)
    w1 = g_w * t / (1.0 + t)

    hot0 = lane == e0
    hot1 = lane == e1
    onehot = jnp.where(hot0 | hot1, 1.0, 0.0)

    @pl.when(n == 0)
    def _():
        cnt_ref[...] = jnp.zeros(cnt_ref.shape, F32)

    before = jnp.dot(ltri_ref[...], onehot.astype(BF16), preferred_element_type=F32) + cnt_ref[...]
    r0 = jnp.sum(jnp.where(hot0, before, 0.0), axis=1, keepdims=True)
    r1 = jnp.sum(jnp.where(hot1, before, 0.0), axis=1, keepdims=True)
    cnt_ref[...] = cnt_ref[...] + jnp.sum(onehot, axis=0, keepdims=True)

    rec = jnp.zeros((rows, LANES), F32)
    for ln, val in ((L_E0, e0), (L_E1, e1), (L_W0, w0), (L_W1, w1), (L_R0, r0), (L_R1, r1)):
        rec = jnp.where(lane == ln, val, rec)
    route_ref[...] = rec


def _router_weights(w_rg, b_rg, w_re, b_re):
    w = jnp.zeros((D_MODEL, LANES), F32)
    w = w.at[:, :N_EXPERTS].set(w_re).at[:, L_GROUP:L_GROUP + N_EXPERT_GROUPS].set(w_rg)
    b = jnp.zeros((1, LANES), F32)
    b = b.at[0, :N_EXPERTS].set(b_re).at[0, L_GROUP:L_GROUP + N_EXPERT_GROUPS].set(b_rg)
    w_hi = w.astype(BF16)
    w_lo = (w - w_hi.astype(F32)).astype(BF16)
    return w_hi, w_lo, b


def _merge_route(xa, xb, yaa, yab, yba, ybb, ga, gb, wbp, wba, wo, n2w, wr_hi, wr_lo, br):
    n_first = xa.shape[0] // TM
    t_all = xa.shape[0] + xb.shape[0]
    n_rows = _moe_blocks(t_all) * MOE_ROWS
    slab_rows = -(-n_rows // MOE_ROWS // (t_all // TM)) * MOE_ROWS
    assert n_rows // slab_rows <= t_all // TM
    ltri = jnp.asarray(np.tril(np.ones((TM, TM), np.float32), -1), BF16)
    ta = lambda w: pl.BlockSpec((TM, w), lambda n: (jnp.minimum(n, n_first - 1), 0))
    tb = lambda w: pl.BlockSpec((TM, w), lambda n: (jnp.maximum(n - n_first, 0), 0))
    tile = lambda w: pl.BlockSpec((TM, w), lambda n: (n, 0))
    return pl.pallas_call(
        functools.partial(_merge_route_kernel, n_first),
        grid=(t_all // TM,),
        in_specs=[
            ta(D_MODEL), tb(D_MODEL), ta(POOL_WIDTH), tb(POOL_WIDTH), ta(ATTN_WIDTH), tb(ATTN_WIDTH),
            ta(N_BRANCH * D_MODEL), tb(N_BRANCH * D_MODEL),
            _const_spec(wbp.shape), _const_spec(wba.shape), _const_spec(wo.shape),
            _const_spec((1, D_MODEL)),
            _const_spec(wr_hi.shape), _const_spec(wr_lo.shape), _const_spec(br.shape),
            _const_spec(ltri.shape),
        ],
        out_specs=[tile(D_MODEL), tile(D_MODEL), tile(LANES), _const_spec((1, LANES)),
                   pl.BlockSpec(memory_space=pl.ANY)],
        out_shape=[
            jax.ShapeDtypeStruct((t_all, D_MODEL), F32),
            jax.ShapeDtypeStruct((t_all, D_MODEL), F32),
            jax.ShapeDtypeStruct((t_all, LANES), F32),
            jax.ShapeDtypeStruct((1, LANES), F32),
            jax.ShapeDtypeStruct((n_rows, D_MODEL), F32),
        ],
        scratch_shapes=[pltpu.VMEM((slab_rows, D_MODEL), F32), pltpu.SemaphoreType.DMA],
        compiler_params=pltpu.CompilerParams(
            dimension_semantics=("arbitrary",), vmem_limit_bytes=VMEM_LIMIT_BYTES),
        name="merge_route",
    )(xa, xb, yaa, yab, yba, ybb, ga, gb, wbp, wba, wo, n2w.reshape(1, -1), wr_hi, wr_lo, br, ltri)


def _moe_blocks(n_tokens):
    return -(-(2 * n_tokens + N_EXPERTS * (MOE_ROWS - 1)) // MOE_ROWS)


def _plan_kernel(nb_pad, cnt_ref, route_ref, upper_ref, dest_ref, blk_ref):
    n = pl.program_id(0)
    cnt = cnt_ref[...]
    nblk = jnp.floor((cnt + (MOE_ROWS - 1)) * (1.0 / MOE_ROWS))
    nb8 = jnp.broadcast_to(nblk, (8, LANES)).astype(BF16)
    bstart = jnp.dot(nb8, upper_ref[...], preferred_element_type=F32)[0:1]
    bend = bstart + nblk
    pstart = bstart * MOE_ROWS

    rec = route_ref[...]
    rows = rec.shape[0]
    lane = lax.broadcasted_iota(jnp.int32, (rows, LANES), 1).astype(F32)
    col = lambda ln: jnp.sum(jnp.where(lane == ln, rec, 0.0), axis=1, keepdims=True)
    look = lambda e: jnp.sum(jnp.where(lane == e, pstart, 0.0), axis=1, keepdims=True)
    d0 = look(col(L_E0)) + col(L_R0)
    d1 = look(col(L_E1)) + col(L_R1)
    dest_ref[...] = jnp.where(lane == 0, d0, jnp.where(lane == 1, d1, 0.0)).astype(jnp.int32)

    @pl.when(n == 0)
    def _():
        b = lax.broadcasted_iota(jnp.int32, (nb_pad, LANES), 0).astype(F32)
        lane_b = lax.broadcasted_iota(jnp.int32, (nb_pad, LANES), 1)
        done = jnp.where((bend <= b) & (lane_b < N_EXPERTS), 1.0, 0.0)
        e_of_b = jnp.minimum(jnp.sum(done, axis=1, keepdims=True), float(N_EXPERTS - 1))
        total = jnp.max(jnp.where(lane_b < N_EXPERTS, bend, 0.0), axis=1, keepdims=True)
        blk_ref[...] = jnp.where(lane_b == 0, e_of_b, jnp.where(lane_b == 1, total, 0.0)).astype(jnp.int32)


def _plan(cnt, route, n_rows):
    t = route.shape[0]
    nb = n_rows // MOE_ROWS
    nb_pad = -(-nb // 8) * 8
    upper = jnp.asarray(np.triu(np.ones((LANES, LANES), np.float32), 1), BF16)
    return pl.pallas_call(
        functools.partial(_plan_kernel, nb_pad),
        grid=(t // TM,),
        in_specs=[_const_spec((1, LANES)), pl.BlockSpec((TM, LANES), lambda n: (n, 0)),
                  _const_spec((LANES, LANES))],
        out_specs=[pl.BlockSpec((TM, LANES), lambda n: (n, 0)), _const_spec((nb_pad, LANES))],
        out_shape=[jax.ShapeDtypeStruct((t, LANES), jnp.int32),
                   jax.ShapeDtypeStruct((nb_pad, LANES), jnp.int32)],
        compiler_params=pltpu.CompilerParams(
            dimension_semantics=("arbitrary",), vmem_limit_bytes=VMEM_LIMIT_BYTES),
        name="moe_plan",
    )(cnt, route, upper)


def _dispatch_kernel(dest_ref, h_ref, xs_in, xs_out, sem):
    del xs_in
    rows = h_ref.shape[0]

    def issue(i, carry):
        base = pl.multiple_of(i * 8, 8)
        for j in range(8):
            for k in range(2):
                pltpu.make_async_copy(h_ref.at[pl.ds(base + j, 1)],
                                      xs_out.at[pl.ds(dest_ref[k, base + j], 1)], sem).start()
        return carry

    lax.fori_loop(0, rows // 8, issue, 0)
    for k in range(2):
        pltpu.make_async_copy(h_ref, xs_out.at[pl.ds(0, rows)], sem).wait()


def _dispatch(dest_t, h, xs):
    t = h.shape[0]
    return pl.pallas_call(
        _dispatch_kernel,
        grid=(t // TM,),
        in_specs=[pl.BlockSpec((None, 8, TM), lambda n: (n, 0, 0), memory_space=pltpu.SMEM),
                  pl.BlockSpec((TM, D_MODEL), lambda n: (n, 0)),
                  pl.BlockSpec(memory_space=pl.ANY)],
        out_specs=pl.BlockSpec(memory_space=pl.ANY),
        out_shape=jax.ShapeDtypeStruct(xs.shape, xs.dtype),
        input_output_aliases={2: 0},
        scratch_shapes=[pltpu.SemaphoreType.DMA],
        compiler_params=pltpu.CompilerParams(
            dimension_semantics=("arbitrary",), vmem_limit_bytes=VMEM_LIMIT_BYTES),
        name="moe_dispatch",
    )(dest_t, h, xs)


def _expert_kernel(blk_ref, x_ref, wg_ref, wu_ref, wd_ref, y_ref, wg_bf, wu_bf, wd_bf):
    b = pl.program_id(0)
    used = b < blk_ref[1, 0]

    @pl.when(used)
    def _():
        prev = blk_ref[0, jnp.maximum(b - 1, 0)]

        @pl.when((b == 0) | (blk_ref[0, b] != prev))
        def _():
            wg_bf[...] = wg_ref[...].astype(BF16)
            wu_bf[...] = wu_ref[...].astype(BF16)
            wd_bf[...] = wd_ref[...].astype(BF16)

        x = x_ref[...].astype(BF16)
        gate = jnp.dot(x, wg_bf[...], preferred_element_type=F32)
        up = jnp.dot(x, wu_bf[...], preferred_element_type=F32)
        mid = (jax.nn.silu(gate) * up).astype(BF16)
        y_ref[...] = jnp.dot(mid, wd_bf[...], preferred_element_type=F32)

    @pl.when(jnp.logical_not(used))
    def _():
        y_ref[...] = jnp.zeros(y_ref.shape, F32)


def _experts(blk_t, xs, w_g, w_u, w_d):
    n_rows = xs.shape[0]
    nb = n_rows // MOE_ROWS
    last = lambda b, blk: jnp.minimum(b, blk[1, 0] - 1)
    grid_spec = pltpu.PrefetchScalarGridSpec(
        num_scalar_prefetch=1,
        grid=(nb,),
        in_specs=[
            pl.BlockSpec((MOE_ROWS, D_MODEL), lambda b, blk: (last(b, blk), 0)),
            pl.BlockSpec((None, D_MODEL, D_EXPERT), lambda b, blk: (blk[0, last(b, blk)], 0, 0)),
            pl.BlockSpec((None, D_MODEL, D_EXPERT), lambda b, blk: (blk[0, last(b, blk)], 0, 0)),
            pl.BlockSpec((None, D_EXPERT, D_MODEL), lambda b, blk: (blk[0, last(b, blk)], 0, 0)),
        ],
        out_specs=pl.BlockSpec((MOE_ROWS, D_MODEL), lambda b, blk: (b, 0)),
        scratch_shapes=[pltpu.VMEM((D_MODEL, D_EXPERT), BF16), pltpu.VMEM((D_MODEL, D_EXPERT), BF16),
                        pltpu.VMEM((D_EXPERT, D_MODEL), BF16)],
    )
    return pl.pallas_call(
        _expert_kernel,
        grid_spec=grid_spec,
        out_shape=jax.ShapeDtypeStruct((n_rows, D_MODEL), F32),
        compiler_params=pltpu.CompilerParams(
            dimension_semantics=("arbitrary",), vmem_limit_bytes=VMEM_LIMIT_BYTES),
        name="moe_experts",
    )(blk_t, xs, w_g, w_u, w_d)


def _combine_kernel(n_first, dest_ref, x1_ref, route_ref, ys_ref, oa_ref, ob_ref, gbuf, sem):
    n = pl.program_id(0)
    rows = x1_ref.shape[0]

    def issue(i, carry):
        base = pl.multiple_of(i * 8, 8)
        for j in range(8):
            for k in range(2):
                pltpu.make_async_copy(ys_ref.at[pl.ds(dest_ref[k, base + j], 1)],
                                      gbuf.at[k, pl.ds(base + j, 1)], sem).start()
        return carry

    lax.fori_loop(0, rows // 8, issue, 0)
    for k in range(2):
        pltpu.make_async_copy(ys_ref.at[pl.ds(0, rows)], gbuf.at[k], sem).wait()
    rec = route_ref[...]
    lane = lax.broadcasted_iota(jnp.int32, rec.shape, 1)
    w0 = jnp.sum(jnp.where(lane == L_W0, rec, 0.0), axis=1, keepdims=True)
    w1 = jnp.sum(jnp.where(lane == L_W1, rec, 0.0), axis=1, keepdims=True)
    res = x1_ref[...] + (gbuf[0] * w0 + gbuf[1] * w1)

    @pl.when(n < n_first)
    def _():
        oa_ref[...] = res

    @pl.when(n >= n_first)
    def _():
        ob_ref[...] = res


def _combine(dest_t, x1, route, ys, t_first):
    t = x1.shape[0]
    n_first = t_first // TM
    return pl.pallas_call(
        functools.partial(_combine_kernel, n_first),
        grid=(t // TM,),
        in_specs=[pl.BlockSpec((None, 8, TM), lambda n: (n, 0, 0), memory_space=pltpu.SMEM),
                  pl.BlockSpec((TM, D_MODEL), lambda n: (n, 0)),
                  pl.BlockSpec((TM, LANES), lambda n: (n, 0)),
                  pl.BlockSpec(memory_space=pl.ANY)],
        out_specs=[pl.BlockSpec((TM, D_MODEL), lambda n: (jnp.minimum(n, n_first - 1), 0)),
                   pl.BlockSpec((TM, D_MODEL), lambda n: (jnp.maximum(n - n_first, 0), 0))],
        out_shape=[jax.ShapeDtypeStruct((t_first, D_MODEL), F32),
                   jax.ShapeDtypeStruct((t - t_first, D_MODEL), F32)],
        scratch_shapes=[pltpu.VMEM((2, TM, D_MODEL), F32), pltpu.SemaphoreType.DMA],
        compiler_params=pltpu.CompilerParams(
            dimension_semantics=("arbitrary",), vmem_limit_bytes=VMEM_LIMIT_BYTES),
        name="moe_combine",
    )(dest_t, x1, route, ys)


def kernel(x_prompt, x_sample, cache_kv_w128, cache_kv_w512, cache_kv_w2048, state_pool, norm1_w, w_in, q_norm_w, k_norm_w, pool_w, pool_scale, w_branch_pool, w_branch_attn, w_out, norm2_w, w_router_group, b_router_group, w_router_expert, b_router_expert, w_expert_gate, w_expert_up, w_expert_down):
    assert x_prompt.shape[0] == 1 and norm1_w.shape[0] == 1
    layer = 0
    s_p = x_prompt.shape[1]
    n_b, n_t, _ = x_sample.shape
    t_s = n_b * n_t
    t_all = s_p + t_s
    caches = (cache_kv_w128[layer], cache_kv_w512[layer], cache_kv_w2048[layer])

    w_bf = w_in[layer].astype(BF16)
    qw_row = jnp.tile(q_norm_w[layer][:, None, :], (1, N_HEADS, 1)).reshape(-1) * (HEAD_DIM ** -0.5)
    kw_row = jnp.tile(k_norm_w[layer][:, None, :], (1, N_HEADS, 1)).reshape(-1)
    poolw_bf = pool_w[layer].astype(BF16)
    wbp, wba, wo = (w.astype(BF16) for w in (w_branch_pool[layer], w_branch_attn[layer], w_out[layer]))
    wr_hi, wr_lo, br = _router_weights(w_router_group[layer], b_router_group[layer],
                                       w_router_expert[layer], b_router_expert[layer])

    ya_p, gates_p, q_r, k_r, v_r, kvt, ut = _proj_prompt(
        x_prompt[0], norm1_w[layer], w_bf, qw_row * LOG2_E, kw_row, poolw_bf, pool_scale[layer])
    yb_p = _attn_prompt(q_r, k_r, v_r)

    xs_t = jnp.transpose(x_sample, (1, 0, 2)).reshape(t_s, D_MODEL)
    state_t = jnp.transpose(state_pool[layer], (1, 0, 2))
    ya_s, gates_s, q_s, k_s, v_s, u_s, kt_s, vt_s = _proj_sample(
        xs_t, x_sample.reshape(t_s, D_MODEL), norm1_w[layer], w_bf, qw_row, kw_row, poolw_bf,
        pool_scale[layer], state_t, n_b, n_t)
    split = lambda a: a.reshape(n_b, n_t, N_DIL, N_HEADS, HEAD_DIM)
    per_head = lambda a: jnp.transpose(split(a), (0, 3, 2, 1, 4))
    new_t = jnp.stack([kt_s, vt_s]).reshape(2, N_DIL, N_HEADS, HEAD_DIM, t_s)
    caches_t = [jnp.transpose(c, (0, 2, 3, 4, 1)) for c in caches]
    o_s, *kv_t = _attn_roll(per_head(q_s), per_head(k_s), per_head(v_s), new_t, caches_t)
    yb_s = jnp.transpose(o_s, (2, 0, 1, 3)).reshape(t_s, ATTN_WIDTH).astype(BF16)
    kv_s = [jnp.transpose(c, (0, 4, 1, 2, 3)) for c in kv_t]

    x1, h, route, cnt, xs = _merge_route(x_prompt[0], xs_t, ya_p, ya_s, yb_p, yb_s, gates_p, gates_s,
                                     wbp, wba, wo, norm2_w[layer], wr_hi, wr_lo, br)

    n_rows = _moe_blocks(t_all) * MOE_ROWS
    dest, blk = _plan(cnt, route, n_rows)
    dest_t = jnp.transpose(dest[:, :8].reshape(t_all // TM, TM, 8), (0, 2, 1))
    blk_t = jnp.transpose(blk[:, :2])
    xs = _dispatch(dest_t, h, xs)
    ys = _experts(blk_t, xs, w_expert_gate[layer], w_expert_up[layer], w_expert_down[layer])
    y_p, y_s = _combine(dest_t, x1, route, ys, s_p)

    y_prompt = y_p[None]
    y_sample = jnp.transpose(y_s.reshape(n_t, n_b, D_MODEL), (1, 0, 2))
    pool_prompt = ut[1:][None, None]
    u_new = jnp.transpose(u_s.reshape(n_t, n_b, POOL_WIDTH), (1, 0, 2))
    pool_sample = jnp.concatenate([state_pool[layer][:, n_t:], u_new], axis=1)[None]
    kv_t = kvt.reshape(2, N_DIL, N_HEADS, HEAD_DIM, kvt.shape[1])
    outs = [y_prompt, y_sample, pool_prompt, pool_sample]
    for g, (win, _) in enumerate(DILATION_PAIRS):
        keep = min(win, s_p)
        kv_g = kv_t[:, g, :, :, kv_t.shape[-1] - keep:]
        outs.append(jnp.transpose(kv_g, (3, 0, 1, 2))[None, None])
        outs.append(kv_s[g][None])
    return tuple(outs)
```

```python
import functools

import numpy as np
import jax
import jax.numpy as jnp
from jax import lax
from jax.experimental import pallas as pl
from jax.experimental.pallas import tpu as pltpu

F32 = jnp.float32
BF16 = jnp.bfloat16

D_MODEL = 1024
PAST_LEN = 8192
POOL_WIDTH = 512
POOL_WINDOWS = (2, 4, 8, 16)
POOL_GROUP_DIM = 128
POOL_STATE = 15
HEAD_DIM = 64
N_HEADS = 8
DILATION_PAIRS = ((128, 1), (512, 4), (2048, 16))
N_DIL = 3
ATTN_WIDTH = 512
QKV_WIDTH = 1536
ROT_DIM = 16
ROPE_THETA = 500000.0
N_BRANCH = 2
IN_COLS = POOL_WIDTH + 3 * QKV_WIDTH + N_BRANCH * D_MODEL
N_EXPERT_GROUPS = 4
EXPERTS_PER_GROUP = 8
N_EXPERTS = 32
D_EXPERT = 512
EPS = 1e-6

LANES = 128
VMEM_LIMIT_BYTES = 56 * 1024 * 1024
ATTN_VMEM_LIMIT_BYTES = 60 * 1024 * 1024

C_U = 0
C_Q = POOL_WIDTH
C_K = C_Q + QKV_WIDTH
C_V = C_K + QKV_WIDTH
C_G = C_V + QKV_WIDTH

N_CLASS = 16
TI = 256
ROWS_PER_CLASS = TI // N_CLASS
NEG = -1e30
LOG2_E = 1.4426950408889634


def _class_order():
    return [(c // 4) + 4 * (c % 4) for c in range(N_CLASS)]


def _const_spec(shape, single_buffer=False):
    nd = len(shape)
    kw = {}
    if single_buffer:
        kw["pipeline_mode"] = pl.Buffered(1)
    return pl.BlockSpec(shape, lambda *_: (0,) * nd, **kw)


def _rms(x, w):
    ms = jnp.mean(x * x, axis=-1, keepdims=True)
    return x * lax.rsqrt(ms + EPS) * w


def _head_norm_rope(y, wrow, seg, segt, cos, sina, sinb):
    sq = (y * y).astype(BF16)
    ssum = jnp.dot(sq, seg, preferred_element_type=F32)
    r = lax.rsqrt(ssum * (1.0 / HEAD_DIM) + EPS)
    r_hi = r.astype(BF16)
    r_lo = (r - r_hi.astype(F32)).astype(BF16)
    rexp = jnp.dot(jnp.concatenate([r_hi, r_lo], axis=1), segt, preferred_element_type=F32)
    yn = y * rexp * wrow
    outs = []
    for j in range(QKV_WIDTH // LANES):
        c = yn[:, j * LANES:(j + 1) * LANES]
        outs.append(c * cos + pltpu.roll(c, LANES - ROT_DIM // 2, 1) * sina
                    + pltpu.roll(c, ROT_DIM // 2, 1) * sinb)
    return jnp.concatenate(outs, axis=1)


def _pool_mix_out(pooled, poolw_ref, pool_scale):
    outs = []
    for g in range(len(POOL_WINDOWS)):
        outs.append(jnp.dot(pooled[g].astype(BF16), poolw_ref[g], preferred_element_type=F32))
    return jnp.concatenate(outs, axis=1) * pool_scale


def _split3(x):
    hi = x.astype(BF16)
    r1 = x - hi.astype(F32)
    mid = r1.astype(BF16)
    lo = (r1 - mid.astype(F32)).astype(BF16)
    return hi, mid, lo


def _proj_prompt_kernel(tail_tiles, x_ref, n1w_ref, w_ref, qw_ref, kw_ref, cosr_ref, sinr_ref, cosn_ref,
                        sinn_ref, rotm_ref, perm_ref, permt_ref, seg_ref, segt_ref, poolw_ref, pools_ref,
                        ya_ref, gates_ref, q_ref, k_ref, v_ref, kvt_ref, ut_ref,
                        uext_ref):
    n = pl.program_id(0)
    xn = _rms(x_ref[...], n1w_ref[...]).astype(BF16)

    u = jnp.dot(xn, w_ref[:, C_U:C_U + POOL_WIDTH], preferred_element_type=F32)

    @pl.when(n == 0)
    def _():
        uext_ref[0:16, :] = jnp.zeros((16, POOL_WIDTH), F32)

    uext_ref[16:16 + TI, :] = u
    pos = n * TI + lax.broadcasted_iota(jnp.int32, (TI, 1), 0)
    pooled = []
    for g, w in enumerate(POOL_WINDOWS):
        lanes = slice(g * POOL_GROUP_DIM, (g + 1) * POOL_GROUP_DIM)
        s = uext_ref[16:16 + TI, lanes]
        for m in range(1, w):
            s = s + uext_ref[16 - m:16 - m + TI, lanes]
        cnt = jnp.minimum(pos + 1, w).astype(F32)
        pooled.append(s / cnt - u[:, lanes])
    ya_ref[...] = _pool_mix_out(pooled, poolw_ref, pools_ref[...]).astype(BF16)
    ut_ref[...] = u[TI - 16:, :]
    uext_ref[0:16, :] = uext_ref[TI:TI + 16, :]

    gl = jnp.dot(xn, w_ref[:, C_G:C_G + N_BRANCH * D_MODEL], preferred_element_type=F32)
    gates_ref[...] = jax.nn.sigmoid(gl).astype(BF16)

    xp = jnp.dot(perm_ref[...], xn, preferred_element_type=F32).astype(BF16)
    cn, sn = cosn_ref[pl.ds(n, 1), :], sinn_ref[pl.ds(n, 1), :]
    cr, sr = cosr_ref[...], sinr_ref[...]
    cos = cn * cr - sn * sr
    sin = sn * cr + cn * sr
    sina, sinb = sin * rotm_ref[0:1, :], sin * rotm_ref[1:2, :]
    seg, segt = seg_ref[...], segt_ref[...]
    shp = (N_CLASS, ROWS_PER_CLASS, QKV_WIDTH)
    yq = jnp.dot(xp, w_ref[:, C_Q:C_Q + QKV_WIDTH], preferred_element_type=F32)
    q_ref[...] = _head_norm_rope(yq, qw_ref[...], seg, segt, cos, sina, sinb).astype(BF16).reshape(shp)
    yk = jnp.dot(xp, w_ref[:, C_K:C_K + QKV_WIDTH], preferred_element_type=F32)
    kr = _head_norm_rope(yk, kw_ref[...], seg, segt, cos, sina, sinb)
    k_ref[...] = kr.astype(BF16).reshape(shp)
    yv = jnp.dot(xp, w_ref[:, C_V:C_V + QKV_WIDTH], preferred_element_type=F32)
    v_ref[...] = yv.astype(BF16).reshape(shp)

    @pl.when(n >= pl.num_programs(0) - tail_tiles)
    def _():
        def natural_t(val):
            nat = sum(jnp.dot(permt_ref[...], part, preferred_element_type=F32) for part in _split3(val))
            return nat.T
        kvt_ref[0:QKV_WIDTH, :] = natural_t(kr)
        kvt_ref[QKV_WIDTH:2 * QKV_WIDTH, :] = natural_t(yv)


def _rope_tables(pos):
    half = ROT_DIM // 2
    inv = ROPE_THETA ** (-(jnp.arange(half, dtype=F32) * 2.0 / ROT_DIM))
    ang = pos.astype(F32)[:, None] * inv[None, :]
    c, s = jnp.cos(ang), jnp.sin(ang)
    n = pos.shape[0]
    ones = jnp.ones((n, HEAD_DIM - ROT_DIM), F32)
    zeros = jnp.zeros((n, HEAD_DIM - ROT_DIM), F32)
    z8 = jnp.zeros((n, half), F32)
    cos64 = jnp.concatenate([c, c, ones], axis=1)
    sina64 = jnp.concatenate([-s, z8, zeros], axis=1)
    sinb64 = jnp.concatenate([z8, s, zeros], axis=1)
    rep = lambda t: jnp.concatenate([t, t], axis=1)
    return rep(cos64), rep(sina64), rep(sinb64)


def _rope_lane_freq():
    half = ROT_DIM // 2
    inv = ROPE_THETA ** (-(np.arange(half, dtype=np.float64) * 2.0 / ROT_DIM))
    d = np.arange(LANES) % HEAD_DIM
    freq = np.where(d < ROT_DIM, inv[d % half], 0.0)
    mask = np.zeros((8, LANES))
    mask[0] = np.where(d < half, -1.0, 0.0)
    mask[1] = np.where((d >= half) & (d < ROT_DIM), 1.0, 0.0)
    return freq, mask


def _seg_mats():
    h = np.arange(QKV_WIDTH) // HEAD_DIM
    seg = (h[:, None] == np.arange(LANES)[None, :]).astype(np.float32)
    return jnp.asarray(seg, BF16), jnp.asarray(np.concatenate([seg.T, seg.T], axis=0), BF16)


def _perm_mat():
    cls = _class_order()
    p = np.zeros((TI, TI), np.float32)
    for c in range(N_CLASS):
        for j in range(ROWS_PER_CLASS):
            p[c * ROWS_PER_CLASS + j, N_CLASS * j + cls[c]] = 1.0
    return p


def _proj_prompt(x, n1w, w_bf, qw_row, kw_row, poolw_bf, pool_scale):
    s = x.shape[0]
    nt = s // TI
    cls = np.asarray(_class_order())
    perm = _perm_mat()
    j = np.arange(ROWS_PER_CLASS)
    inner = (N_CLASS * j[None, :] + cls[:, None]).reshape(-1).astype(np.float64)
    freq, rot_mask = _rope_lane_freq()
    ang_r = inner[:, None] * freq[None, :]
    ang_n = (np.arange(nt, dtype=np.float64) * TI)[:, None] * freq[None, :]
    f32c = lambda a: jnp.asarray(a.astype(np.float32))
    cosr, sinr, cosn, sinn = f32c(np.cos(ang_r)), f32c(np.sin(ang_r)), f32c(np.cos(ang_n)), f32c(np.sin(ang_n))
    rotm = f32c(rot_mask)
    seg, segt = _seg_mats()
    n_i = s // N_CLASS
    tail_tokens = min(DILATION_PAIRS[-1][0], s)
    tail_tiles = tail_tokens // TI
    row = lambda a: a.reshape(1, -1)
    tile = lambda n: (n, 0)
    rblk = lambda n: (0, n, 0)
    tblk = lambda n: (0, jnp.maximum(n - (nt - tail_tiles), 0))
    outs = pl.pallas_call(
        functools.partial(_proj_prompt_kernel, tail_tiles),
        grid=(nt,),
        in_specs=[
            pl.BlockSpec((TI, D_MODEL), tile),
            _const_spec((1, D_MODEL)),
            _const_spec((D_MODEL, IN_COLS), single_buffer=True),
            _const_spec((1, QKV_WIDTH)),
            _const_spec((1, QKV_WIDTH)),
            _const_spec((TI, LANES)), _const_spec((TI, LANES)),
            _const_spec((nt, LANES)), _const_spec((nt, LANES)), _const_spec((8, LANES)),
            _const_spec((TI, TI)), _const_spec((TI, TI)),
            _const_spec((QKV_WIDTH, LANES)),
            _const_spec((2 * LANES, QKV_WIDTH)),
            _const_spec((len(POOL_WINDOWS), POOL_GROUP_DIM, POOL_GROUP_DIM)),
            _const_spec((1, POOL_WIDTH)),
        ],
        out_specs=[
            pl.BlockSpec((TI, POOL_WIDTH), tile),
            pl.BlockSpec((TI, N_BRANCH * D_MODEL), tile),
            pl.BlockSpec((N_CLASS, ROWS_PER_CLASS, QKV_WIDTH), rblk),
            pl.BlockSpec((N_CLASS, ROWS_PER_CLASS, QKV_WIDTH), rblk),
            pl.BlockSpec((N_CLASS, ROWS_PER_CLASS, QKV_WIDTH), rblk),
            pl.BlockSpec((2 * QKV_WIDTH, TI), tblk),
            _const_spec((16, POOL_WIDTH)),
        ],
        out_shape=[
            jax.ShapeDtypeStruct((s, POOL_WIDTH), BF16),
            jax.ShapeDtypeStruct((s, N_BRANCH * D_MODEL), BF16),
            jax.ShapeDtypeStruct((N_CLASS, n_i, QKV_WIDTH), BF16),
            jax.ShapeDtypeStruct((N_CLASS, n_i, QKV_WIDTH), BF16),
            jax.ShapeDtypeStruct((N_CLASS, n_i, QKV_WIDTH), BF16),
            jax.ShapeDtypeStruct((2 * QKV_WIDTH, tail_tokens), F32),
            jax.ShapeDtypeStruct((16, POOL_WIDTH), F32),
        ],
        scratch_shapes=[pltpu.VMEM((16 + TI, POOL_WIDTH), F32)],
        compiler_params=pltpu.CompilerParams(
            dimension_semantics=("arbitrary",), vmem_limit_bytes=VMEM_LIMIT_BYTES),
        name="proj_prompt",
    )(x, row(n1w), w_bf, row(qw_row), row(kw_row), cosr, sinr, cosn, sinn, rotm,
      jnp.asarray(perm, BF16), jnp.asarray(perm.T, BF16), seg, segt, poolw_bf, row(pool_scale))
    return outs


AT_I = 128
N_KEYS_BACK = 128
BLOCKS_PER_TRIP = 4


def _attn_update(q, k, v, bias, m_old, l_old, acc_old):
    nq, nk = q.shape[0], k.shape[0]
    assert nk % LANES == 0
    lane = lax.broadcasted_iota(jnp.int32, (nq, LANES), 1)
    first = lane < HEAD_DIM
    bias2 = jnp.concatenate([bias, bias], axis=0)
    ones = jnp.ones((nk, LANES), BF16)
    m_out, l_out, a_out = [], [], []
    for hp in range(N_HEADS // 2):
        sl = slice(hp * LANES, (hp + 1) * LANES)
        qp, kp, vp = q[:, sl], k[:, sl], v[:, sl]
        lo, ao = l_old[:, sl], acc_old[:, sl]
        mo2 = jnp.concatenate([m_old[:, (2 * hp) * LANES:(2 * hp + 1) * LANES],
                               m_old[:, (2 * hp + 1) * LANES:(2 * hp + 2) * LANES]], axis=0)
        zero = jnp.zeros_like(qp)
        q2 = jnp.concatenate([jnp.where(first, qp, zero), jnp.where(first, zero, qp)], axis=0)
        s2 = lax.dot_general(q2, kp, (((1,), (1,)), ((), ())), preferred_element_type=F32) + bias2
        mn2 = jnp.maximum(mo2, jnp.max(s2, axis=1, keepdims=True))
        p2 = jnp.exp2(s2 - jnp.concatenate([mn2] * (nk // LANES), axis=1)).astype(BF16)
        pv2 = jnp.dot(p2, jnp.concatenate([vp, ones], axis=1), preferred_element_type=F32)
        pv = jnp.where(first, pv2[:nq, :LANES], pv2[nq:, :LANES])
        lc = jnp.where(first, pv2[:nq, LANES:], pv2[nq:, LANES:])
        alpha2 = jnp.exp2(mo2 - mn2)
        alpha = jnp.where(first, alpha2[:nq], alpha2[nq:])
        m_out += [mn2[:nq], mn2[nq:]]
        l_out.append(alpha * lo + lc)
        a_out.append(alpha * ao + pv)
    cat = lambda xs: jnp.concatenate(xs, axis=1)
    return cat(m_out), cat(l_out), cat(a_out)


def _attn_prompt_kernel(q_ref, kc_ref, kp_ref, vc_ref, vp_ref, b0_ref, b1_ref, b2_ref,
                        r0_ref, r1_ref, r2_ref, o_ref,
                        kcat, vcat, acc, m_s, l_s, ost):
    n = pl.program_id(0)
    g = pl.program_id(1)
    i0 = n * AT_I
    kcat[:, 0:AT_I, :] = kp_ref[...]
    kcat[:, AT_I:2 * AT_I, :] = kc_ref[...]
    vcat[:, 0:AT_I, :] = vp_ref[...]
    vcat[:, AT_I:2 * AT_I, :] = vc_ref[...]

    def key_bias(b_ref, r_ref, base):
        return b_ref[...] + jnp.where(base + r_ref[...] >= 0, 0.0, NEG)

    @pl.when(g == 0)
    def _():
        m_s[...] = jnp.full(m_s.shape, NEG, F32)
        l_s[...] = jnp.zeros(l_s.shape, F32)
        acc[...] = jnp.zeros(acc.shape, F32)

        def load(s):
            qo = pl.multiple_of(16 * s, 16)
            ko = pl.multiple_of(AT_I - 16 + 16 * s, 16)
            q16 = [q_ref[c, pl.ds(qo, 16), :] for c in range(N_CLASS)]
            k32 = [kcat[c, pl.ds(ko, 32), :] for c in range(N_CLASS)]
            v32 = [vcat[c, pl.ds(ko, 32), :] for c in range(N_CLASS)]
            cat = lambda f: jnp.concatenate([f(c) for c in range(N_CLASS)], axis=0)
            work = []
            for half in range(2):
                qh = pl.multiple_of(qo + 8 * half, 8)
                q = cat(lambda c: q16[c][8 * half:8 * half + 8])
                k = cat(lambda c: k32[c][8 + 8 * half:24 + 8 * half])
                v = cat(lambda c: v32[c][8 + 8 * half:24 + 8 * half])
                mo = cat(lambda c: m_s[c, pl.ds(qh, 8), :])
                lo = cat(lambda c: l_s[c, pl.ds(qh, 8), :])
                ao = cat(lambda c: acc[c, pl.ds(qh, 8), :])
                bias = key_bias(b0_ref, r0_ref, i0 + 16 * s + 8 * half)
                work.append((qh, (q, k, v, bias, mo, lo, ao)))
            return work

        def body(idx, carry):
            chunks = BLOCKS_PER_TRIP // 2
            work = [w for j in range(chunks) for w in load(chunks * idx + j)]
            done = [(qh, _attn_update(*args)) for qh, args in work]
            for qh, (mn, ln, an) in done:
                for c in range(N_CLASS):
                    m_s[c, pl.ds(qh, 8), :] = mn[8 * c:8 * c + 8]
                    l_s[c, pl.ds(qh, 8), :] = ln[8 * c:8 * c + 8]
                    acc[c, pl.ds(qh, 8), :] = an[8 * c:8 * c + 8]
            return carry

        lax.fori_loop(0, AT_I // 16 // (BLOCKS_PER_TRIP // 2), body, 0)

    @pl.when(g == 1)
    def _():
        def load(r4, s):
            qo = pl.multiple_of(32 * s, 32)
            ko = pl.multiple_of(AT_I - 32 + 32 * s, 32)
            cat = lambda f: jnp.concatenate([f(4 * r4 + a) for a in range(4)], axis=0)
            q = cat(lambda c: q_ref[c, pl.ds(qo, 32), :])
            k = cat(lambda c: kcat[c, pl.ds(ko, 64), :])
            v = cat(lambda c: vcat[c, pl.ds(ko, 64), :])
            mo = cat(lambda c: m_s[c, pl.ds(qo, 32), :])
            lo = cat(lambda c: l_s[c, pl.ds(qo, 32), :])
            ao = cat(lambda c: acc[c, pl.ds(qo, 32), :])
            bias = key_bias(b1_ref, r1_ref, i0 + 32 * s)
            return q, k, v, bias, mo, lo, ao

        def store(r4, s, mn, ln, an):
            qo = pl.multiple_of(32 * s, 32)
            for a in range(4):
                c = 4 * r4 + a
                m_s[c, pl.ds(qo, 32), :] = mn[32 * a:32 * a + 32]
                l_s[c, pl.ds(qo, 32), :] = ln[32 * a:32 * a + 32]
                acc[c, pl.ds(qo, 32), :] = an[32 * a:32 * a + 32]

        def body(idx, carry):
            per_class = 4 // BLOCKS_PER_TRIP
            where = [(idx // per_class, BLOCKS_PER_TRIP * (idx % per_class) + j) for j in range(BLOCKS_PER_TRIP)]
            work = [load(r4, s) for r4, s in where]
            done = [_attn_update(*args) for args in work]
            for (r4, s), res in zip(where, done):
                store(r4, s, *res)
            return carry

        lax.fori_loop(0, 16 // BLOCKS_PER_TRIP, body, 0)

    @pl.when(g == 2)
    def _():
        bias = key_bias(b2_ref, r2_ref, i0)

        def body(cc, carry):
            cs = [BLOCKS_PER_TRIP * cc + j for j in range(BLOCKS_PER_TRIP)]
            work = [(q_ref[c], kcat[c], vcat[c], bias, m_s[c], l_s[c], acc[c]) for c in cs]
            done = [_attn_update(*args) for args in work]
            for c, (mn, ln, an) in zip(cs, done):
                m_s[c] = mn
                l_s[c] = ln
                acc[c] = an
            return carry

        lax.fori_loop(0, N_CLASS // BLOCKS_PER_TRIP, body, 0)

        cls = _class_order()
        for c in range(N_CLASS):
            o = acc[c] / l_s[c]
            for jc in range(ATTN_WIDTH // LANES):
                ost[jc, pl.ds(cls[c], AT_I, stride=N_CLASS), :] = o[:, jc * LANES:(jc + 1) * LANES]
        o_ref[...] = jnp.concatenate([ost[jc] for jc in range(ATTN_WIDTH // LANES)], axis=1).astype(BF16)


def _attn_bias_tables():
    cls = np.asarray(_class_order())

    def bias(diff):
        return np.where((diff >= 0) & (diff <= N_KEYS_BACK), 0.0, NEG).astype(np.float32)

    c, r = np.divmod(np.arange(128), 8)
    c2, r2 = np.divmod(np.arange(256), 16)
    d0 = 16 * (r[:, None] - r2[None, :] + 8) + cls[c][:, None] - cls[c2][None, :]
    rel0 = (r2 - 8).astype(np.int32)[None, :]
    a, r = np.divmod(np.arange(128), 32)
    a2, r2 = np.divmod(np.arange(256), 64)
    d1 = 4 * (r[:, None] - r2[None, :] + 32) + a[:, None] - a2[None, :]
    rel1 = (r2 - 32).astype(np.int32)[None, :]
    r = np.arange(128)
    r2 = np.arange(256)
    d2 = r[:, None] - r2[None, :] + 128
    rel2 = (r2 - 128).astype(np.int32)[None, :]
    return (jnp.asarray(bias(d0)), jnp.asarray(bias(d1)), jnp.asarray(bias(d2)),
            jnp.asarray(rel0), jnp.asarray(rel1), jnp.asarray(rel2))


def _attn_prompt(q_r, k_r, v_r):
    n_i = q_r.shape[1]
    s = n_i * N_CLASS
    nsteps = n_i // AT_I
    b0, b1, b2, r0, r1, r2 = _attn_bias_tables()
    blk = (N_CLASS, AT_I, ATTN_WIDTH)
    cur = lambda n, g: (0, n, g)
    prev = lambda n, g: (0, jnp.maximum(n - 1, 0), g)
    return pl.pallas_call(
        _attn_prompt_kernel,
        grid=(nsteps, N_DIL),
        in_specs=[
            pl.BlockSpec(blk, cur), pl.BlockSpec(blk, cur), pl.BlockSpec(blk, prev),
            pl.BlockSpec(blk, cur), pl.BlockSpec(blk, prev),
            _const_spec(b0.shape), _const_spec(b1.shape), _const_spec(b2.shape),
            _const_spec(r0.shape), _const_spec(r1.shape), _const_spec(r2.shape),
        ],
        out_specs=pl.BlockSpec((AT_I * N_CLASS, ATTN_WIDTH), lambda n, g: (n, 0)),
        out_shape=jax.ShapeDtypeStruct((s, ATTN_WIDTH), BF16),
        scratch_shapes=[
            pltpu.VMEM((N_CLASS, 2 * AT_I, ATTN_WIDTH), BF16),
            pltpu.VMEM((N_CLASS, 2 * AT_I, ATTN_WIDTH), BF16),
            pltpu.VMEM((N_CLASS, AT_I, ATTN_WIDTH), F32),
            pltpu.VMEM((N_CLASS, AT_I, N_HEADS * LANES), F32),
            pltpu.VMEM((N_CLASS, AT_I, ATTN_WIDTH), F32),
            pltpu.VMEM((ATTN_WIDTH // LANES, AT_I * N_CLASS, LANES), F32),
        ],
        compiler_params=pltpu.CompilerParams(
            dimension_semantics=("arbitrary", "arbitrary"), vmem_limit_bytes=ATTN_VMEM_LIMIT_BYTES),
        name="attn_prompt",
    )(q_r, k_r, k_r, v_r, v_r, b0, b1, b2, r0, r1, r2)


def _proj_sample_kernel(n_b, n_t, x_ref, xb_ref, n1w_ref, w_ref, qw_ref, kw_ref, cos_ref, sina_ref, sinb_ref,
                        seg_ref, segt_ref, poolw_ref, pools_ref, state_ref,
                        ya_ref, gates_ref, q_ref, k_ref, v_ref, u_ref, kt_ref, vt_ref):
    xn = _rms(x_ref[...], n1w_ref[...]).astype(BF16)
    u = jnp.dot(xn, w_ref[:, C_U:C_U + POOL_WIDTH], preferred_element_type=F32)
    u_ref[...] = u

    def ext(j, lanes):
        if j < POOL_STATE:
            return state_ref[j, :, lanes]
        return u[(j - POOL_STATE) * n_b:(j - POOL_STATE + 1) * n_b, lanes]

    pooled = []
    for g, w in enumerate(POOL_WINDOWS):
        lanes = slice(g * POOL_GROUP_DIM, (g + 1) * POOL_GROUP_DIM)
        rows = []
        for i in range(n_t):
            s = ext(POOL_STATE + i, lanes)
            for m in range(1, w):
                s = s + ext(POOL_STATE + i - m, lanes)
            cnt = float(min(PAST_LEN + i + 1, w))
            rows.append(s / cnt - ext(POOL_STATE + i, lanes))
        pooled.append(jnp.concatenate(rows, axis=0))
    ya_ref[...] = _pool_mix_out(pooled, poolw_ref, pools_ref[...]).astype(BF16)

    gl = jnp.dot(xn, w_ref[:, C_G:C_G + N_BRANCH * D_MODEL], preferred_element_type=F32)
    gates_ref[...] = jax.nn.sigmoid(gl).astype(BF16)

    xnb = _rms(xb_ref[...], n1w_ref[...]).astype(BF16)
    cos, sina, sinb = cos_ref[...], sina_ref[...], sinb_ref[...]
    seg, segt = seg_ref[...], segt_ref[...]
    yq = jnp.dot(xnb, w_ref[:, C_Q:C_Q + QKV_WIDTH], preferred_element_type=F32)
    q_ref[...] = _head_norm_rope(yq, qw_ref[...], seg, segt, cos, sina, sinb)
    yk = jnp.dot(xnb, w_ref[:, C_K:C_K + QKV_WIDTH], preferred_element_type=F32)
    kr = _head_norm_rope(yk, kw_ref[...], seg, segt, cos, sina, sinb)
    k_ref[...] = kr
    kt_ref[...] = kr.T
    yv = jnp.dot(xnb, w_ref[:, C_V:C_V + QKV_WIDTH], preferred_element_type=F32)
    v_ref[...] = yv
    vt_ref[...] = yv.T


def _proj_sample(x, xb, n1w, w_bf, qw_row, kw_row, poolw_bf, pool_scale, state_t, n_b, n_t):
    rows = n_b * n_t
    pos = PAST_LEN + jnp.arange(rows, dtype=jnp.int32) % n_t
    cos, sina, sinb = _rope_tables(pos)
    seg, segt = _seg_mats()
    row = lambda a: a.reshape(1, -1)
    args = (x, xb, row(n1w), w_bf, row(qw_row), row(kw_row), cos, sina, sinb, seg, segt,
            poolw_bf, row(pool_scale), state_t)
    f32_qkv = jax.ShapeDtypeStruct((rows, QKV_WIDTH), F32)
    out_shape = [
        jax.ShapeDtypeStruct((rows, POOL_WIDTH), BF16),
        jax.ShapeDtypeStruct((rows, N_BRANCH * D_MODEL), BF16),
        f32_qkv, f32_qkv, f32_qkv,
        jax.ShapeDtypeStruct((rows, POOL_WIDTH), F32),
        jax.ShapeDtypeStruct((QKV_WIDTH, rows), F32),
        jax.ShapeDtypeStruct((QKV_WIDTH, rows), F32),
    ]
    return pl.pallas_call(
        functools.partial(_proj_sample_kernel, n_b, n_t),
        grid=(1,),
        in_specs=[_const_spec(a.shape) for a in args],
        out_specs=[_const_spec(o.shape) for o in out_shape],
        out_shape=out_shape,
        compiler_params=pltpu.CompilerParams(
            dimension_semantics=("arbitrary",), vmem_limit_bytes=VMEM_LIMIT_BYTES),
        name="proj_sample",
    )(*args)


HEADS_PER_STEP = 8


def _attn_roll_kernel(n_t, q_ref, kn_ref, vn_ref, newt_ref, c0_ref, c1_ref, c2_ref,
                      b0_ref, b1_ref, b2_ref, bn_ref, o_ref, o0_ref, o1_ref, o2_ref):
    caches = (c0_ref, c1_ref, c2_ref)
    outs = (o0_ref, o1_ref, o2_ref)
    biases = (b0_ref, b1_ref, b2_ref)
    nt_contract = (((1,), (1,)), ((), ()))
    col = pl.program_id(1) * n_t
    new_tile = pl.multiple_of((col // LANES) * LANES, LANES)
    new_shift = (LANES - n_t) - col % LANES
    for hh in range(HEADS_PER_STEP):
        scores, values = [], []
        for g in range(N_DIL):
            q = q_ref[hh, g].astype(BF16)
            kc = caches[g][0, hh].astype(BF16)
            scores.append(jnp.dot(q, kc, preferred_element_type=F32) + biases[g][...])
            kn = kn_ref[hh, g].astype(BF16)
            scores.append(lax.dot_general(q, kn, nt_contract, preferred_element_type=F32) + bn_ref[g])
            values.append(caches[g][1, hh].astype(BF16))
            values.append(vn_ref[hh, g].astype(BF16))
        m = functools.reduce(jnp.maximum, [jnp.max(s, axis=1, keepdims=True) for s in scores])
        l = jnp.zeros_like(m)
        o = jnp.zeros((n_t, HEAD_DIM), F32)
        for idx, (s, v) in enumerate(zip(scores, values)):
            p = jnp.exp(s - m)
            l = l + jnp.sum(p, axis=1, keepdims=True)
            pb = p.astype(BF16)
            if idx % 2 == 0:
                o = o + lax.dot_general(pb, v, nt_contract, preferred_element_type=F32)
            else:
                o = o + jnp.dot(pb, v, preferred_element_type=F32)
        o_ref[hh] = o / l

        for g in range(N_DIL):
            length = caches[g].shape[-1]
            lane = lax.broadcasted_iota(jnp.int32, (HEAD_DIM, LANES), 1)
            for kv in range(2):
                rolled = pltpu.roll(caches[g][kv, hh], length - n_t, 1)
                new = pltpu.roll(newt_ref[kv, g, hh, :, pl.ds(new_tile, LANES)], new_shift, 1)
                if length > LANES:
                    outs[g][kv, hh, :, 0:length - LANES] = rolled[:, 0:length - LANES]
                outs[g][kv, hh, :, length - LANES:length] = jnp.where(
                    lane >= LANES - n_t, new, rolled[:, length - LANES:length])


def _attn_roll_bias(n_t):
    i = np.arange(n_t)
    bias, bias_new = [], []
    for win, dil in DILATION_PAIRS:
        length = min(win, PAST_LEN)
        delta = length + i[:, None] - np.arange(length)[None, :]
        ok = (delta % dil == 0) & (delta // dil >= 1) & (delta // dil <= win // dil)
        bias.append(jnp.asarray(np.where(ok, 0.0, NEG).astype(np.float32)))
        dn = i[:, None] - i[None, :]
        okn = (dn >= 0) & (dn % dil == 0) & (dn // dil <= win // dil)
        bias_new.append(np.where(okn, 0.0, NEG).astype(np.float32))
    return bias, jnp.asarray(np.stack(bias_new))


def _attn_roll(q_h, kn_h, vn_h, new_t, caches_t):
    n_b, _, _, n_t, _ = q_h.shape
    bias, bias_new = _attn_roll_bias(n_t)
    hs = HEADS_PER_STEP
    qspec = pl.BlockSpec((None, hs, N_DIL, n_t, HEAD_DIM), lambda h, b: (b, h, 0, 0, 0))
    cspec = lambda c: pl.BlockSpec((None, 2, hs, HEAD_DIM, c.shape[-1]), lambda h, b: (b, 0, h, 0, 0))
    return pl.pallas_call(
        functools.partial(_attn_roll_kernel, n_t),
        grid=(N_HEADS // hs, n_b),
        in_specs=[qspec, qspec, qspec,
                  pl.BlockSpec((2, N_DIL, hs, HEAD_DIM, n_t * n_b), lambda h, b: (0, 0, h, 0, 0))]
                 + [cspec(c) for c in caches_t]
                 + [_const_spec(b.shape) for b in bias] + [_const_spec(bias_new.shape)],
        out_specs=[pl.BlockSpec((None, hs, n_t, HEAD_DIM), lambda h, b: (b, h, 0, 0))]
                  + [cspec(c) for c in caches_t],
        out_shape=[jax.ShapeDtypeStruct((n_b, N_HEADS, n_t, HEAD_DIM), F32)]
                  + [jax.ShapeDtypeStruct(c.shape, c.dtype) for c in caches_t],
        compiler_params=pltpu.CompilerParams(
            dimension_semantics=("arbitrary", "arbitrary"), vmem_limit_bytes=VMEM_LIMIT_BYTES),
        name="attn_roll",
    )(q_h, kn_h, vn_h, new_t, *caches_t, *bias, bias_new)


TM = 512
MOE_ROWS = 256
L_E0, L_E1, L_W0, L_W1, L_R0, L_R1 = 0, 1, 2, 3, 4, 5
L_GROUP = N_EXPERTS


def _lane_min_index(mask, lane):
    return jnp.min(jnp.where(mask, lane, float(LANES)), axis=1, keepdims=True)


def _zero_fill_step(n, xs_ref, zbuf, sem):
    slab = zbuf.shape[0]
    n_full, rem = divmod(xs_ref.shape[0], slab)
    full = lambda i: pltpu.make_async_copy(
        zbuf, xs_ref.at[pl.ds(pl.multiple_of(i * slab, MOE_ROWS), slab)], sem)
    tail = pltpu.make_async_copy(zbuf.at[pl.ds(0, max(rem, 1))],
                                 xs_ref.at[pl.ds(n_full * slab, max(rem, 1))], sem)

    @pl.when(n == 0)
    def _():
        zbuf[...] = jnp.zeros(zbuf.shape, F32)
        if rem:
            tail.start()

    @pl.when(n < n_full)
    def _():
        full(n).start()

    @pl.when(n == pl.num_programs(0) - 1)
    def _():
        for _ in range(n_full):
            full(0).wait()
        if rem:
            tail.wait()


def _merge_route_kernel(n_first, xa_ref, xb_ref, yaa_ref, yab_ref, yba_ref, ybb_ref, ga_ref, gb_ref,
                        wbp_ref, wba_ref, wo_ref, n2w_ref, wrh_ref, wrl_ref, br_ref, ltri_ref,
                        x1_ref, h_ref, route_ref, cnt_ref, xs_ref, zbuf, sem):
    n = pl.program_id(0)
    _zero_fill_step(n, xs_ref, zbuf, sem)
    first = n < n_first
    pick = lambda a, b: jnp.where(first, a[...], b[...])
    pa = jnp.dot(pick(yaa_ref, yab_ref), wbp_ref[...], preferred_element_type=F32)
    pb = jnp.dot(pick(yba_ref, ybb_ref), wba_ref[...], preferred_element_type=F32)
    gts = pick(ga_ref, gb_ref)
    merged = gts[:, :D_MODEL].astype(F32) * pa + gts[:, D_MODEL:].astype(F32) * pb
    x1 = pick(xa_ref, xb_ref) + jnp.dot(merged.astype(BF16), wo_ref[...], preferred_element_type=F32)
    x1_ref[...] = x1
    h = _rms(x1, n2w_ref[...])
    h_ref[...] = h

    h_hi = h.astype(BF16)
    h_lo = (h - h_hi.astype(F32)).astype(BF16)
    logits = (jnp.dot(h_hi, wrh_ref[...], preferred_element_type=F32)
              + jnp.dot(h_hi, wrl_ref[...], preferred_element_type=F32)
              + jnp.dot(h_lo, wrh_ref[...], preferred_element_type=F32)) + br_ref[...]
    rows = logits.shape[0]
    lane = lax.broadcasted_iota(jnp.int32, (rows, LANES), 1).astype(F32)
    is_group = (lane >= L_GROUP) & (lane < L_GROUP + N_EXPERT_GROUPS)
    gl = jnp.where(is_group, logits, NEG)
    gmax = jnp.max(gl, axis=1, keepdims=True)
    g_w = 1.0 / jnp.sum(jnp.exp(gl - gmax), axis=1, keepdims=True)
    g_sel = _lane_min_index(gl == gmax, lane) - L_GROUP
    in_group = (lane >= g_sel * EXPERTS_PER_GROUP) & (lane < (g_sel + 1.0) * EXPERTS_PER_GROUP)
    el = jnp.where(in_group, logits, NEG)
    v0 = jnp.max(el, axis=1, keepdims=True)
    e0 = _lane_min_index(el == v0, lane)
    el2 = jnp.where(lane == e0, NEG, el)
    v1 = jnp.max(el2, axis=1, keepdims=True)
    e1 = _lane_min_index(el2 == v1, lane)
    t = jnp.exp(v1 - v0)
    w0 = g_w / (1.0 + t)
    w1 = g_w * t / (1.0 + t)

    hot0 = lane == e0
    hot1 = lane == e1
    onehot = jnp.where(hot0 | hot1, 1.0, 0.0)

    @pl.when(n == 0)
    def _():
        cnt_ref[...] = jnp.zeros(cnt_ref.shape, F32)

    before = jnp.dot(ltri_ref[...], onehot.astype(BF16), preferred_element_type=F32) + cnt_ref[...]
    r0 = jnp.sum(jnp.where(hot0, before, 0.0), axis=1, keepdims=True)
    r1 = jnp.sum(jnp.where(hot1, before, 0.0), axis=1, keepdims=True)
    cnt_ref[...] = cnt_ref[...] + jnp.sum(onehot, axis=0, keepdims=True)

    rec = jnp.zeros((rows, LANES), F32)
    for ln, val in ((L_E0, e0), (L_E1, e1), (L_W0, w0), (L_W1, w1), (L_R0, r0), (L_R1, r1)):
        rec = jnp.where(lane == ln, val, rec)
    route_ref[...] = rec


def _router_weights(w_rg, b_rg, w_re, b_re):
    w = jnp.zeros((D_MODEL, LANES), F32)
    w = w.at[:, :N_EXPERTS].set(w_re).at[:, L_GROUP:L_GROUP + N_EXPERT_GROUPS].set(w_rg)
    b = jnp.zeros((1, LANES), F32)
    b = b.at[0, :N_EXPERTS].set(b_re).at[0, L_GROUP:L_GROUP + N_EXPERT_GROUPS].set(b_rg)
    w_hi = w.astype(BF16)
    w_lo = (w - w_hi.astype(F32)).astype(BF16)
    return w_hi, w_lo, b


def _merge_route(xa, xb, yaa, yab, yba, ybb, ga, gb, wbp, wba, wo, n2w, wr_hi, wr_lo, br):
    n_first = xa.shape[0] // TM
    t_all = xa.shape[0] + xb.shape[0]
    n_rows = _moe_blocks(t_all) * MOE_ROWS
    slab_rows = -(-n_rows // MOE_ROWS // (t_all // TM)) * MOE_ROWS
    assert n_rows // slab_rows <= t_all // TM
    ltri = jnp.asarray(np.tril(np.ones((TM, TM), np.float32), -1), BF16)
    ta = lambda w: pl.BlockSpec((TM, w), lambda n: (jnp.minimum(n, n_first - 1), 0))
    tb = lambda w: pl.BlockSpec((TM, w), lambda n: (jnp.maximum(n - n_first, 0), 0))
    tile = lambda w: pl.BlockSpec((TM, w), lambda n: (n, 0))
    return pl.pallas_call(
        functools.partial(_merge_route_kernel, n_first),
        grid=(t_all // TM,),
        in_specs=[
            ta(D_MODEL), tb(D_MODEL), ta(POOL_WIDTH), tb(POOL_WIDTH), ta(ATTN_WIDTH), tb(ATTN_WIDTH),
            ta(N_BRANCH * D_MODEL), tb(N_BRANCH * D_MODEL),
            _const_spec(wbp.shape), _const_spec(wba.shape), _const_spec(wo.shape),
            _const_spec((1, D_MODEL)),
            _const_spec(wr_hi.shape), _const_spec(wr_lo.shape), _const_spec(br.shape),
            _const_spec(ltri.shape),
        ],
        out_specs=[tile(D_MODEL), tile(D_MODEL), tile(LANES), _const_spec((1, LANES)),
                   pl.BlockSpec(memory_space=pl.ANY)],
        out_shape=[
            jax.ShapeDtypeStruct((t_all, D_MODEL), F32),
            jax.ShapeDtypeStruct((t_all, D_MODEL), F32),
            jax.ShapeDtypeStruct((t_all, LANES), F32),
            jax.ShapeDtypeStruct((1, LANES), F32),
            jax.ShapeDtypeStruct((n_rows, D_MODEL), F32),
        ],
        scratch_shapes=[pltpu.VMEM((slab_rows, D_MODEL), F32), pltpu.SemaphoreType.DMA],
        compiler_params=pltpu.CompilerParams(
            dimension_semantics=("arbitrary",), vmem_limit_bytes=VMEM_LIMIT_BYTES),
        name="merge_route",
    )(xa, xb, yaa, yab, yba, ybb, ga, gb, wbp, wba, wo, n2w.reshape(1, -1), wr_hi, wr_lo, br, ltri)


def _moe_blocks(n_tokens):
    return -(-(2 * n_tokens + N_EXPERTS * (MOE_ROWS - 1)) // MOE_ROWS)


def _plan_kernel(nb_pad, cnt_ref, route_ref, upper_ref, dest_ref, blk_ref):
    n = pl.program_id(0)
    cnt = cnt_ref[...]
    nblk = jnp.floor((cnt + (MOE_ROWS - 1)) * (1.0 / MOE_ROWS))
    nb8 = jnp.broadcast_to(nblk, (8, LANES)).astype(BF16)
    bstart = jnp.dot(nb8, upper_ref[...], preferred_element_type=F32)[0:1]
    bend = bstart + nblk
    pstart = bstart * MOE_ROWS

    rec = route_ref[...]
    rows = rec.shape[0]
    lane = lax.broadcasted_iota(jnp.int32, (rows, LANES), 1).astype(F32)
    col = lambda ln: jnp.sum(jnp.where(lane == ln, rec, 0.0), axis=1, keepdims=True)
    look = lambda e: jnp.sum(jnp.where(lane == e, pstart, 0.0), axis=1, keepdims=True)
    d0 = look(col(L_E0)) + col(L_R0)
    d1 = look(col(L_E1)) + col(L_R1)
    dest_ref[...] = jnp.where(lane == 0, d0, jnp.where(lane == 1, d1, 0.0)).astype(jnp.int32)

    @pl.when(n == 0)
    def _():
        b = lax.broadcasted_iota(jnp.int32, (nb_pad, LANES), 0).astype(F32)
        lane_b = lax.broadcasted_iota(jnp.int32, (nb_pad, LANES), 1)
        done = jnp.where((bend <= b) & (lane_b < N_EXPERTS), 1.0, 0.0)
        e_of_b = jnp.minimum(jnp.sum(done, axis=1, keepdims=True), float(N_EXPERTS - 1))
        total = jnp.max(jnp.where(lane_b < N_EXPERTS, bend, 0.0), axis=1, keepdims=True)
        blk_ref[...] = jnp.where(lane_b == 0, e_of_b, jnp.where(lane_b == 1, total, 0.0)).astype(jnp.int32)


def _plan(cnt, route, n_rows):
    t = route.shape[0]
    nb = n_rows // MOE_ROWS
    nb_pad = -(-nb // 8) * 8
    upper = jnp.asarray(np.triu(np.ones((LANES, LANES), np.float32), 1), BF16)
    return pl.pallas_call(
        functools.partial(_plan_kernel, nb_pad),
        grid=(t // TM,),
        in_specs=[_const_spec((1, LANES)), pl.BlockSpec((TM, LANES), lambda n: (n, 0)),
                  _const_spec((LANES, LANES))],
        out_specs=[pl.BlockSpec((TM, LANES), lambda n: (n, 0)), _const_spec((nb_pad, LANES))],
        out_shape=[jax.ShapeDtypeStruct((t, LANES), jnp.int32),
                   jax.ShapeDtypeStruct((nb_pad, LANES), jnp.int32)],
        compiler_params=pltpu.CompilerParams(
            dimension_semantics=("arbitrary",), vmem_limit_bytes=VMEM_LIMIT_BYTES),
        name="moe_plan",
    )(cnt, route, upper)


def _dispatch_kernel(dest_ref, h_ref, xs_in, xs_out, sem):
    del xs_in
    rows = h_ref.shape[0]

    def issue(i, carry):
        base = pl.multiple_of(i * 8, 8)
        for j in range(8):
            for k in range(2):
                pltpu.make_async_copy(h_ref.at[pl.ds(base + j, 1)],
                                      xs_out.at[pl.ds(dest_ref[k, base + j], 1)], sem).start()
        return carry

    lax.fori_loop(0, rows // 8, issue, 0)
    for k in range(2):
        pltpu.make_async_copy(h_ref, xs_out.at[pl.ds(0, rows)], sem).wait()


def _dispatch(dest_t, h, xs):
    t = h.shape[0]
    return pl.pallas_call(
        _dispatch_kernel,
        grid=(t // TM,),
        in_specs=[pl.BlockSpec((None, 8, TM), lambda n: (n, 0, 0), memory_space=pltpu.SMEM),
                  pl.BlockSpec((TM, D_MODEL), lambda n: (n, 0)),
                  pl.BlockSpec(memory_space=pl.ANY)],
        out_specs=pl.BlockSpec(memory_space=pl.ANY),
        out_shape=jax.ShapeDtypeStruct(xs.shape, xs.dtype),
        input_output_aliases={2: 0},
        scratch_shapes=[pltpu.SemaphoreType.DMA],
        compiler_params=pltpu.CompilerParams(
            dimension_semantics=("arbitrary",), vmem_limit_bytes=VMEM_LIMIT_BYTES),
        name="moe_dispatch",
    )(dest_t, h, xs)


def _expert_kernel(blk_ref, x_ref, wg_ref, wu_ref, wd_ref, y_ref, wg_bf, wu_bf, wd_bf):
    b = pl.program_id(0)
    used = b < blk_ref[1, 0]

    @pl.when(used)
    def _():
        prev = blk_ref[0, jnp.maximum(b - 1, 0)]

        @pl.when((b == 0) | (blk_ref[0, b] != prev))
        def _():
            wg_bf[...] = wg_ref[...].astype(BF16)
            wu_bf[...] = wu_ref[...].astype(BF16)
            wd_bf[...] = wd_ref[...].astype(BF16)

        x = x_ref[...].astype(BF16)
        gate = jnp.dot(x, wg_bf[...], preferred_element_type=F32)
        up = jnp.dot(x, wu_bf[...], preferred_element_type=F32)
        mid = (jax.nn.silu(gate) * up).astype(BF16)
        y_ref[...] = jnp.dot(mid, wd_bf[...], preferred_element_type=F32)

    @pl.when(jnp.logical_not(used))
    def _():
        y_ref[...] = jnp.zeros(y_ref.shape, F32)


def _experts(blk_t, xs, w_g, w_u, w_d):
    n_rows = xs.shape[0]
    nb = n_rows // MOE_ROWS
    last = lambda b, blk: jnp.minimum(b, blk[1, 0] - 1)
    grid_spec = pltpu.PrefetchScalarGridSpec(
        num_scalar_prefetch=1,
        grid=(nb,),
        in_specs=[
            pl.BlockSpec((MOE_ROWS, D_MODEL), lambda b, blk: (last(b, blk), 0)),
            pl.BlockSpec((None, D_MODEL, D_EXPERT), lambda b, blk: (blk[0, last(b, blk)], 0, 0)),
            pl.BlockSpec((None, D_MODEL, D_EXPERT), lambda b, blk: (blk[0, last(b, blk)], 0, 0)),
            pl.BlockSpec((None, D_EXPERT, D_MODEL), lambda b, blk: (blk[0, last(b, blk)], 0, 0)),
        ],
        out_specs=pl.BlockSpec((MOE_ROWS, D_MODEL), lambda b, blk: (b, 0)),
        scratch_shapes=[pltpu.VMEM((D_MODEL, D_EXPERT), BF16), pltpu.VMEM((D_MODEL, D_EXPERT), BF16),
                        pltpu.VMEM((D_EXPERT, D_MODEL), BF16)],
    )
    return pl.pallas_call(
        _expert_kernel,
        grid_spec=grid_spec,
        out_shape=jax.ShapeDtypeStruct((n_rows, D_MODEL), F32),
        compiler_params=pltpu.CompilerParams(
            dimension_semantics=("arbitrary",), vmem_limit_bytes=VMEM_LIMIT_BYTES),
        name="moe_experts",
    )(blk_t, xs, w_g, w_u, w_d)


def _combine_kernel(n_first, dest_ref, x1_ref, route_ref, ys_ref, oa_ref, ob_ref, gbuf, sem):
    n = pl.program_id(0)
    rows = x1_ref.shape[0]

    def issue(i, carry):
        base = pl.multiple_of(i * 8, 8)
        for j in range(8):
            for k in range(2):
                pltpu.make_async_copy(ys_ref.at[pl.ds(dest_ref[k, base + j], 1)],
                                      gbuf.at[k, pl.ds(base + j, 1)], sem).start()
        return carry

    lax.fori_loop(0, rows // 8, issue, 0)
    for k in range(2):
        pltpu.make_async_copy(ys_ref.at[pl.ds(0, rows)], gbuf.at[k], sem).wait()
    rec = route_ref[...]
    lane = lax.broadcasted_iota(jnp.int32, rec.shape, 1)
    w0 = jnp.sum(jnp.where(lane == L_W0, rec, 0.0), axis=1, keepdims=True)
    w1 = jnp.sum(jnp.where(lane == L_W1, rec, 0.0), axis=1, keepdims=True)
    res = x1_ref[...] + (gbuf[0] * w0 + gbuf[1] * w1)

    @pl.when(n < n_first)
    def _():
        oa_ref[...] = res

    @pl.when(n >= n_first)
    def _():
        ob_ref[...] = res


def _combine(dest_t, x1, route, ys, t_first):
    t = x1.shape[0]
    n_first = t_first // TM
    return pl.pallas_call(
        functools.partial(_combine_kernel, n_first),
        grid=(t // TM,),
        in_specs=[pl.BlockSpec((None, 8, TM), lambda n: (n, 0, 0), memory_space=pltpu.SMEM),
                  pl.BlockSpec((TM, D_MODEL), lambda n: (n, 0)),
                  pl.BlockSpec((TM, LANES), lambda n: (n, 0)),
                  pl.BlockSpec(memory_space=pl.ANY)],
        out_specs=[pl.BlockSpec((TM, D_MODEL), lambda n: (jnp.minimum(n, n_first - 1), 0)),
                   pl.BlockSpec((TM, D_MODEL), lambda n: (jnp.maximum(n - n_first, 0), 0))],
        out_shape=[jax.ShapeDtypeStruct((t_first, D_MODEL), F32),
                   jax.ShapeDtypeStruct((t - t_first, D_MODEL), F32)],
        scratch_shapes=[pltpu.VMEM((2, TM, D_MODEL), F32), pltpu.SemaphoreType.DMA],
        compiler_params=pltpu.CompilerParams(
            dimension_semantics=("arbitrary",), vmem_limit_bytes=VMEM_LIMIT_BYTES),
        name="moe_combine",
    )(dest_t, x1, route, ys)


def kernel(x_prompt, x_sample, cache_kv_w128, cache_kv_w512, cache_kv_w2048, state_pool, norm1_w, w_in, q_norm_w, k_norm_w, pool_w, pool_scale, w_branch_pool, w_branch_attn, w_out, norm2_w, w_router_group, b_router_group, w_router_expert, b_router_expert, w_expert_gate, w_expert_up, w_expert_down):
    assert x_prompt.shape[0] == 1 and norm1_w.shape[0] == 1
    layer = 0
    s_p = x_prompt.shape[1]
    n_b, n_t, _ = x_sample.shape
    t_s = n_b * n_t
    t_all = s_p + t_s
    caches = (cache_kv_w128[layer], cache_kv_w512[layer], cache_kv_w2048[layer])

    w_bf = w_in[layer].astype(BF16)
    qw_row = jnp.tile(q_norm_w[layer][:, None, :], (1, N_HEADS, 1)).reshape(-1) * (HEAD_DIM ** -0.5)
    kw_row = jnp.tile(k_norm_w[layer][:, None, :], (1, N_HEADS, 1)).reshape(-1)
    poolw_bf = pool_w[layer].astype(BF16)
    wbp, wba, wo = (w.astype(BF16) for w in (w_branch_pool[layer], w_branch_attn[layer], w_out[layer]))
    wr_hi, wr_lo, br = _router_weights(w_router_group[layer], b_router_group[layer],
                                       w_router_expert[layer], b_router_expert[layer])

    ya_p, gates_p, q_r, k_r, v_r, kvt, ut = _proj_prompt(
        x_prompt[0], norm1_w[layer], w_bf, qw_row * LOG2_E, kw_row, poolw_bf, pool_scale[layer])
    yb_p = _attn_prompt(q_r, k_r, v_r)

    xs_t = jnp.transpose(x_sample, (1, 0, 2)).reshape(t_s, D_MODEL)
    state_t = jnp.transpose(state_pool[layer], (1, 0, 2))
    ya_s, gates_s, q_s, k_s, v_s, u_s, kt_s, vt_s = _proj_sample(
        xs_t, x_sample.reshape(t_s, D_MODEL), norm1_w[layer], w_bf, qw_row, kw_row, poolw_bf,
        pool_scale[layer], state_t, n_b, n_t)
    split = lambda a: a.reshape(n_b, n_t, N_DIL, N_HEADS, HEAD_DIM)
    per_head = lambda a: jnp.transpose(split(a), (0, 3, 2, 1, 4))
    new_t = jnp.stack([kt_s, vt_s]).reshape(2, N_DIL, N_HEADS, HEAD_DIM, t_s)
    caches_t = [jnp.transpose(c, (0, 2, 3, 4, 1)) for c in caches]
    o_s, *kv_t = _attn_roll(per_head(q_s), per_head(k_s), per_head(v_s), new_t, caches_t)
    yb_s = jnp.transpose(o_s, (2, 0, 1, 3)).reshape(t_s, ATTN_WIDTH).astype(BF16)
    kv_s = [jnp.transpose(c, (0, 4, 1, 2, 3)) for c in kv_t]

    x1, h, route, cnt, xs = _merge_route(x_prompt[0], xs_t, ya_p, ya_s, yb_p, yb_s, gates_p, gates_s,
                                     wbp, wba, wo, norm2_w[layer], wr_hi, wr_lo, br)

    n_rows = _moe_blocks(t_all) * MOE_ROWS
    dest, blk = _plan(cnt, route, n_rows)
    dest_t = jnp.transpose(dest[:, :8].reshape(t_all // TM, TM, 8), (0, 2, 1))
    blk_t = jnp.transpose(blk[:, :2])
    xs = _dispatch(dest_t, h, xs)
    ys = _experts(blk_t, xs, w_expert_gate[layer], w_expert_up[layer], w_expert_down[layer])
    y_p, y_s = _combine(dest_t, x1, route, ys, s_p)

    y_prompt = y_p[None]
    y_sample = jnp.transpose(y_s.reshape(n_t, n_b, D_MODEL), (1, 0, 2))
    pool_prompt = ut[1:][None, None]
    u_new = jnp.transpose(u_s.reshape(n_t, n_b, POOL_WIDTH), (1, 0, 2))
    pool_sample = jnp.concatenate([state_pool[layer][:, n_t:], u_new], axis=1)[None]
    kv_t = kvt.reshape(2, N_DIL, N_HEADS, HEAD_DIM, kvt.shape[1])
    outs = [y_prompt, y_sample, pool_prompt, pool_sample]
    for g, (win, _) in enumerate(DILATION_PAIRS):
        keep = min(win, s_p)
        kv_g = kv_t[:, g, :, :, kv_t.shape[-1] - keep:]
        outs.append(jnp.transpose(kv_g, (3, 0, 1, 2))[None, None])
        outs.append(kv_s[g][None])
    return tuple(outs)
```

```python
import functools

import numpy as np
import jax
import jax.numpy as jnp
from jax import lax
from jax.experimental import pallas as pl
from jax.experimental.pallas import tpu as pltpu

F32 = jnp.float32
BF16 = jnp.bfloat16

D_MODEL = 1024
PAST_LEN = 8192
POOL_WIDTH = 512
POOL_WINDOWS = (2, 4, 8, 16)
POOL_GROUP_DIM = 128
POOL_STATE = 15
HEAD_DIM = 64
N_HEADS = 8
DILATION_PAIRS = ((128, 1), (512, 4), (2048, 16))
N_DIL = 3
ATTN_WIDTH = 512
QKV_WIDTH = 1536
ROT_DIM = 16
ROPE_THETA = 500000.0
N_BRANCH = 2
IN_COLS = POOL_WIDTH + 3 * QKV_WIDTH + N_BRANCH * D_MODEL
N_EXPERT_GROUPS = 4
EXPERTS_PER_GROUP = 8
N_EXPERTS = 32
D_EXPERT = 512
EPS = 1e-6

LANES = 128
VMEM_LIMIT_BYTES = 56 * 1024 * 1024
ATTN_VMEM_LIMIT_BYTES = 60 * 1024 * 1024

C_U = 0
C_Q = POOL_WIDTH
C_K = C_Q + QKV_WIDTH
C_V = C_K + QKV_WIDTH
C_G = C_V + QKV_WIDTH

N_CLASS = 16
TI = 256
ROWS_PER_CLASS = TI // N_CLASS
NEG = -1e30
LOG2_E = 1.4426950408889634


def _class_order():
    return [(c // 4) + 4 * (c % 4) for c in range(N_CLASS)]


def _const_spec(shape, single_buffer=False):
    nd = len(shape)
    kw = {}
    if single_buffer:
        kw["pipeline_mode"] = pl.Buffered(1)
    return pl.BlockSpec(shape, lambda *_: (0,) * nd, **kw)


def _rms(x, w):
    ms = jnp.mean(x * x, axis=-1, keepdims=True)
    return x * lax.rsqrt(ms + EPS) * w


def _head_norm_rope(y, wrow, seg, segt, cos, sina, sinb):
    sq = (y * y).astype(BF16)
    ssum = jnp.dot(sq, seg, preferred_element_type=F32)
    r = lax.rsqrt(ssum * (1.0 / HEAD_DIM) + EPS)
    r_hi = r.astype(BF16)
    r_lo = (r - r_hi.astype(F32)).astype(BF16)
    rexp = jnp.dot(jnp.concatenate([r_hi, r_lo], axis=1), segt, preferred_element_type=F32)
    yn = y * rexp * wrow
    outs = []
    for j in range(QKV_WIDTH // LANES):
        c = yn[:, j * LANES:(j + 1) * LANES]
        outs.append(c * cos + pltpu.roll(c, LANES - ROT_DIM // 2, 1) * sina
                    + pltpu.roll(c, ROT_DIM // 2, 1) * sinb)
    return jnp.concatenate(outs, axis=1)


def _pool_mix_out(pooled, poolw_ref, pool_scale):
    outs = []
    for g in range(len(POOL_WINDOWS)):
        outs.append(jnp.dot(pooled[g].astype(BF16), poolw_ref[g], preferred_element_type=F32))
    return jnp.concatenate(outs, axis=1) * pool_scale


def _split3(x):
    hi = x.astype(BF16)
    r1 = x - hi.astype(F32)
    mid = r1.astype(BF16)
    lo = (r1 - mid.astype(F32)).astype(BF16)
    return hi, mid, lo


def _proj_prompt_kernel(tail_tiles, x_ref, n1w_ref, w_ref, qw_ref, kw_ref, cosr_ref, sinr_ref, cosn_ref,
                        sinn_ref, rotm_ref, perm_ref, permt_ref, seg_ref, segt_ref, poolw_ref, pools_ref,
                        ya_ref, gates_ref, q_ref, k_ref, v_ref, kvt_ref, ut_ref,
                        uext_ref):
    n = pl.program_id(0)
    xn = _rms(x_ref[...], n1w_ref[...]).astype(BF16)

    u = jnp.dot(xn, w_ref[:, C_U:C_U + POOL_WIDTH], preferred_element_type=F32)

    @pl.when(n == 0)
    def _():
        uext_ref[0:16, :] = jnp.zeros((16, POOL_WIDTH), F32)

    uext_ref[16:16 + TI, :] = u
    pos = n * TI + lax.broadcasted_iota(jnp.int32, (TI, 1), 0)
    pooled = []
    for g, w in enumerate(POOL_WINDOWS):
        lanes = slice(g * POOL_GROUP_DIM, (g + 1) * POOL_GROUP_DIM)
        s = uext_ref[16:16 + TI, lanes]
        for m in range(1, w):
            s = s + uext_ref[16 - m:16 - m + TI, lanes]
        cnt = jnp.minimum(pos + 1, w).astype(F32)
        pooled.append(s / cnt - u[:, lanes])
    ya_ref[...] = _pool_mix_out(pooled, poolw_ref, pools_ref[...]).astype(BF16)
    ut_ref[...] = u[TI - 16:, :]
    uext_ref[0:16, :] = uext_ref[TI:TI + 16, :]

    gl = jnp.dot(xn, w_ref[:, C_G:C_G + N_BRANCH * D_MODEL], preferred_element_type=F32)
    gates_ref[...] = jax.nn.sigmoid(gl).astype(BF16)

    xp = jnp.dot(perm_ref[...], xn, preferred_element_type=F32).astype(BF16)
    cn, sn = cosn_ref[pl.ds(n, 1), :], sinn_ref[pl.ds(n, 1), :]
    cr, sr = cosr_ref[...], sinr_ref[...]
    cos = cn * cr - sn * sr
    sin = sn * cr + cn * sr
    sina, sinb = sin * rotm_ref[0:1, :], sin * rotm_ref[1:2, :]
    seg, segt = seg_ref[...], segt_ref[...]
    shp = (N_CLASS, ROWS_PER_CLASS, QKV_WIDTH)
    yq = jnp.dot(xp, w_ref[:, C_Q:C_Q + QKV_WIDTH], preferred_element_type=F32)
    q_ref[...] = _head_norm_rope(yq, qw_ref[...], seg, segt, cos, sina, sinb).astype(BF16).reshape(shp)
    yk = jnp.dot(xp, w_ref[:, C_K:C_K + QKV_WIDTH], preferred_element_type=F32)
    kr = _head_norm_rope(yk, kw_ref[...], seg, segt, cos, sina, sinb)
    k_ref[...] = kr.astype(BF16).reshape(shp)
    yv = jnp.dot(xp, w_ref[:, C_V:C_V + QKV_WIDTH], preferred_element_type=F32)
    v_ref[...] = yv.astype(BF16).reshape(shp)

    @pl.when(n >= pl.num_programs(0) - tail_tiles)
    def _():
        def natural_t(val):
            nat = sum(jnp.dot(permt_ref[...], part, preferred_element_type=F32) for part in _split3(val))
            return nat.T
        kvt_ref[0:QKV_WIDTH, :] = natural_t(kr)
        kvt_ref[QKV_WIDTH:2 * QKV_WIDTH, :] = natural_t(yv)


def _rope_tables(pos):
    half = ROT_DIM // 2
    inv = ROPE_THETA ** (-(jnp.arange(half, dtype=F32) * 2.0 / ROT_DIM))
    ang = pos.astype(F32)[:, None] * inv[None, :]
    c, s = jnp.cos(ang), jnp.sin(ang)
    n = pos.shape[0]
    ones = jnp.ones((n, HEAD_DIM - ROT_DIM), F32)
    zeros = jnp.zeros((n, HEAD_DIM - ROT_DIM), F32)
    z8 = jnp.zeros((n, half), F32)
    cos64 = jnp.concatenate([c, c, ones], axis=1)
    sina64 = jnp.concatenate([-s, z8, zeros], axis=1)
    sinb64 = jnp.concatenate([z8, s, zeros], axis=1)
    rep = lambda t: jnp.concatenate([t, t], axis=1)
    return rep(cos64), rep(sina64), rep(sinb64)


def _rope_lane_freq():
    half = ROT_DIM // 2
    inv = ROPE_THETA ** (-(np.arange(half, dtype=np.float64) * 2.0 / ROT_DIM))
    d = np.arange(LANES) % HEAD_DIM
    freq = np.where(d < ROT_DIM, inv[d % half], 0.0)
    mask = np.zeros((8, LANES))
    mask[0] = np.where(d < half, -1.0, 0.0)
    mask[1] = np.where((d >= half) & (d < ROT_DIM), 1.0, 0.0)
    return freq, mask


def _seg_mats():
    h = np.arange(QKV_WIDTH) // HEAD_DIM
    seg = (h[:, None] == np.arange(LANES)[None, :]).astype(np.float32)
    return jnp.asarray(seg, BF16), jnp.asarray(np.concatenate([seg.T, seg.T], axis=0), BF16)


def _perm_mat():
    cls = _class_order()
    p = np.zeros((TI, TI), np.float32)
    for c in range(N_CLASS):
        for j in range(ROWS_PER_CLASS):
            p[c * ROWS_PER_CLASS + j, N_CLASS * j + cls[c]] = 1.0
    return p


def _proj_prompt(x, n1w, w_bf, qw_row, kw_row, poolw_bf, pool_scale):
    s = x.shape[0]
    nt = s // TI
    cls = np.asarray(_class_order())
    perm = _perm_mat()
    j = np.arange(ROWS_PER_CLASS)
    inner = (N_CLASS * j[None, :] + cls[:, None]).reshape(-1).astype(np.float64)
    freq, rot_mask = _rope_lane_freq()
    ang_r = inner[:, None] * freq[None, :]
    ang_n = (np.arange(nt, dtype=np.float64) * TI)[:, None] * freq[None, :]
    f32c = lambda a: jnp.asarray(a.astype(np.float32))
    cosr, sinr, cosn, sinn = f32c(np.cos(ang_r)), f32c(np.sin(ang_r)), f32c(np.cos(ang_n)), f32c(np.sin(ang_n))
    rotm = f32c(rot_mask)
    seg, segt = _seg_mats()
    n_i = s // N_CLASS
    tail_tokens = min(DILATION_PAIRS[-1][0], s)
    tail_tiles = tail_tokens // TI
    row = lambda a: a.reshape(1, -1)
    tile = lambda n: (n, 0)
    rblk = lambda n: (0, n, 0)
    tblk = lambda n: (0, jnp.maximum(n - (nt - tail_tiles), 0))
    outs = pl.pallas_call(
        functools.partial(_proj_prompt_kernel, tail_tiles),
        grid=(nt,),
        in_specs=[
            pl.BlockSpec((TI, D_MODEL), tile),
            _const_spec((1, D_MODEL)),
            _const_spec((D_MODEL, IN_COLS), single_buffer=True),
            _const_spec((1, QKV_WIDTH)),
            _const_spec((1, QKV_WIDTH)),
            _const_spec((TI, LANES)), _const_spec((TI, LANES)),
            _const_spec((nt, LANES)), _const_spec((nt, LANES)), _const_spec((8, LANES)),
            _const_spec((TI, TI)), _const_spec((TI, TI)),
            _const_spec((QKV_WIDTH, LANES)),
            _const_spec((2 * LANES, QKV_WIDTH)),
            _const_spec((len(POOL_WINDOWS), POOL_GROUP_DIM, POOL_GROUP_DIM)),
            _const_spec((1, POOL_WIDTH)),
        ],
        out_specs=[
            pl.BlockSpec((TI, POOL_WIDTH), tile),
            pl.BlockSpec((TI, N_BRANCH * D_MODEL), tile),
            pl.BlockSpec((N_CLASS, ROWS_PER_CLASS, QKV_WIDTH), rblk),
            pl.BlockSpec((N_CLASS, ROWS_PER_CLASS, QKV_WIDTH), rblk),
            pl.BlockSpec((N_CLASS, ROWS_PER_CLASS, QKV_WIDTH), rblk),
            pl.BlockSpec((2 * QKV_WIDTH, TI), tblk),
            _const_spec((16, POOL_WIDTH)),
        ],
        out_shape=[
            jax.ShapeDtypeStruct((s, POOL_WIDTH), BF16),
            jax.ShapeDtypeStruct((s, N_BRANCH * D_MODEL), BF16),
            jax.ShapeDtypeStruct((N_CLASS, n_i, QKV_WIDTH), BF16),
            jax.ShapeDtypeStruct((N_CLASS, n_i, QKV_WIDTH), BF16),
            jax.ShapeDtypeStruct((N_CLASS, n_i, QKV_WIDTH), BF16),
            jax.ShapeDtypeStruct((2 * QKV_WIDTH, tail_tokens), F32),
            jax.ShapeDtypeStruct((16, POOL_WIDTH), F32),
        ],
        scratch_shapes=[pltpu.VMEM((16 + TI, POOL_WIDTH), F32)],
        compiler_params=pltpu.CompilerParams(
            dimension_semantics=("arbitrary",), vmem_limit_bytes=VMEM_LIMIT_BYTES),
        name="proj_prompt",
    )(x, row(n1w), w_bf, row(qw_row), row(kw_row), cosr, sinr, cosn, sinn, rotm,
      jnp.asarray(perm, BF16), jnp.asarray(perm.T, BF16), seg, segt, poolw_bf, row(pool_scale))
    return outs


AT_I = 128
N_KEYS_BACK = 128
BLOCKS_PER_TRIP = 4


def _attn_update(q, k, v, bias, m_old, l_old, acc_old):
    nq, nk = q.shape[0], k.shape[0]
    assert nk % LANES == 0
    lane = lax.broadcasted_iota(jnp.int32, (nq, LANES), 1)
    first = lane < HEAD_DIM
    bias2 = jnp.concatenate([bias, bias], axis=0)
    ones = jnp.ones((nk, LANES), BF16)
    m_out, l_out, a_out = [], [], []
    for hp in range(N_HEADS // 2):
        sl = slice(hp * LANES, (hp + 1) * LANES)
        qp, kp, vp = q[:, sl], k[:, sl], v[:, sl]
        lo, ao = l_old[:, sl], acc_old[:, sl]
        mo2 = jnp.concatenate([m_old[:, (2 * hp) * LANES:(2 * hp + 1) * LANES],
                               m_old[:, (2 * hp + 1) * LANES:(2 * hp + 2) * LANES]], axis=0)
        zero = jnp.zeros_like(qp)
        q2 = jnp.concatenate([jnp.where(first, qp, zero), jnp.where(first, zero, qp)], axis=0)
        s2 = lax.dot_general(q2, kp, (((1,), (1,)), ((), ())), preferred_element_type=F32) + bias2
        mn2 = jnp.maximum(mo2, jnp.max(s2, axis=1, keepdims=True))
        p2 = jnp.exp2(s2 - jnp.concatenate([mn2] * (nk // LANES), axis=1)).astype(BF16)
        pv2 = jnp.dot(p2, jnp.concatenate([vp, ones], axis=1), preferred_element_type=F32)
        pv = jnp.where(first, pv2[:nq, :LANES], pv2[nq:, :LANES])
        lc = jnp.where(first, pv2[:nq, LANES:], pv2[nq:, LANES:])
        alpha2 = jnp.exp2(mo2 - mn2)
        alpha = jnp.where(first, alpha2[:nq], alpha2[nq:])
        m_out += [mn2[:nq], mn2[nq:]]
        l_out.append(alpha * lo + lc)
        a_out.append(alpha * ao + pv)
    cat = lambda xs: jnp.concatenate(xs, axis=1)
    return cat(m_out), cat(l_out), cat(a_out)


def _attn_prompt_kernel(q_ref, kc_ref, kp_ref, vc_ref, vp_ref, b0_ref, b1_ref, b2_ref,
                        r0_ref, r1_ref, r2_ref, o_ref,
                        kcat, vcat, acc, m_s, l_s, ost):
    n = pl.program_id(0)
    g = pl.program_id(1)
    i0 = n * AT_I
    kcat[:, 0:AT_I, :] = kp_ref[...]
    kcat[:, AT_I:2 * AT_I, :] = kc_ref[...]
    vcat[:, 0:AT_I, :] = vp_ref[...]
    vcat[:, AT_I:2 * AT_I, :] = vc_ref[...]

    def key_bias(b_ref, r_ref, base):
        return b_ref[...] + jnp.where(base + r_ref[...] >= 0, 0.0, NEG)

    @pl.when(g == 0)
    def _():
        m_s[...] = jnp.full(m_s.shape, NEG, F32)
        l_s[...] = jnp.zeros(l_s.shape, F32)
        acc[...] = jnp.zeros(acc.shape, F32)

        def load(s):
            qo = pl.multiple_of(16 * s, 16)
            ko = pl.multiple_of(AT_I - 16 + 16 * s, 16)
            q16 = [q_ref[c, pl.ds(qo, 16), :] for c in range(N_CLASS)]
            k32 = [kcat[c, pl.ds(ko, 32), :] for c in range(N_CLASS)]
            v32 = [vcat[c, pl.ds(ko, 32), :] for c in range(N_CLASS)]
            cat = lambda f: jnp.concatenate([f(c) for c in range(N_CLASS)], axis=0)
            work = []
            for half in range(2):
                qh = pl.multiple_of(qo + 8 * half, 8)
                q = cat(lambda c: q16[c][8 * half:8 * half + 8])
                k = cat(lambda c: k32[c][8 + 8 * half:24 + 8 * half])
                v = cat(lambda c: v32[c][8 + 8 * half:24 + 8 * half])
                mo = cat(lambda c: m_s[c, pl.ds(qh, 8), :])
                lo = cat(lambda c: l_s[c, pl.ds(qh, 8), :])
                ao = cat(lambda c: acc[c, pl.ds(qh, 8), :])
                bias = key_bias(b0_ref, r0_ref, i0 + 16 * s + 8 * half)
                work.append((qh, (q, k, v, bias, mo, lo, ao)))
            return work

        def body(idx, carry):
            chunks = BLOCKS_PER_TRIP // 2
            work = [w for j in range(chunks) for w in load(chunks * idx + j)]
            done = [(qh, _attn_update(*args)) for qh, args in work]
            for qh, (mn, ln, an) in done:
                for c in range(N_CLASS):
                    m_s[c, pl.ds(qh, 8), :] = mn[8 * c:8 * c + 8]
                    l_s[c, pl.ds(qh, 8), :] = ln[8 * c:8 * c + 8]
                    acc[c, pl.ds(qh, 8), :] = an[8 * c:8 * c + 8]
            return carry

        lax.fori_loop(0, AT_I // 16 // (BLOCKS_PER_TRIP // 2), body, 0)

    @pl.when(g == 1)
    def _():
        def load(r4, s):
            qo = pl.multiple_of(32 * s, 32)
            ko = pl.multiple_of(AT_I - 32 + 32 * s, 32)
            cat = lambda f: jnp.concatenate([f(4 * r4 + a) for a in range(4)], axis=0)
            q = cat(lambda c: q_ref[c, pl.ds(qo, 32), :])
            k = cat(lambda c: kcat[c, pl.ds(ko, 64), :])
            v = cat(lambda c: vcat[c, pl.ds(ko, 64), :])
            mo = cat(lambda c: m_s[c, pl.ds(qo, 32), :])
            lo = cat(lambda c: l_s[c, pl.ds(qo, 32), :])
            ao = cat(lambda c: acc[c, pl.ds(qo, 32), :])
            bias = key_bias(b1_ref, r1_ref, i0 + 32 * s)
            return q, k, v, bias, mo, lo, ao

        def store(r4, s, mn, ln, an):
            qo = pl.multiple_of(32 * s, 32)
            for a in range(4):
                c = 4 * r4 + a
                m_s[c, pl.ds(qo, 32), :] = mn[32 * a:32 * a + 32]
                l_s[c, pl.ds(qo, 32), :] = ln[32 * a:32 * a + 32]
                acc[c, pl.ds(qo, 32), :] = an[32 * a:32 * a + 32]

        def body(idx, carry):
            per_class = 4 // BLOCKS_PER_TRIP
            where = [(idx // per_class, BLOCKS_PER_TRIP * (idx % per_class) + j) for j in range(BLOCKS_PER_TRIP)]
            work = [load(r4, s) for r4, s in where]
            done = [_attn_update(*args) for args in work]
            for (r4, s), res in zip(where, done):
                store(r4, s, *res)
            return carry

        lax.fori_loop(0, 16 // BLOCKS_PER_TRIP, body, 0)

    @pl.when(g == 2)
    def _():
        bias = key_bias(b2_ref, r2_ref, i0)

        def body(cc, carry):
            cs = [BLOCKS_PER_TRIP * cc + j for j in range(BLOCKS_PER_TRIP)]
            work = [(q_ref[c], kcat[c], vcat[c], bias, m_s[c], l_s[c], acc[c]) for c in cs]
            done = [_attn_update(*args) for args in work]
            for c, (mn, ln, an) in zip(cs, done):
                m_s[c] = mn
                l_s[c] = ln
                acc[c] = an
            return carry

        lax.fori_loop(0, N_CLASS // BLOCKS_PER_TRIP, body, 0)

        cls = _class_order()
        for c in range(N_CLASS):
            o = acc[c] / l_s[c]
            for jc in range(ATTN_WIDTH // LANES):
                ost[jc, pl.ds(cls[c], AT_I, stride=N_CLASS), :] = o[:, jc * LANES:(jc + 1) * LANES]
        o_ref[...] = jnp.concatenate([ost[jc] for jc in range(ATTN_WIDTH // LANES)], axis=1).astype(BF16)


def _attn_bias_tables():
    cls = np.asarray(_class_order())

    def bias(diff):
        return np.where((diff >= 0) & (diff <= N_KEYS_BACK), 0.0, NEG).astype(np.float32)

    c, r = np.divmod(np.arange(128), 8)
    c2, r2 = np.divmod(np.arange(256), 16)
    d0 = 16 * (r[:, None] - r2[None, :] + 8) + cls[c][:, None] - cls[c2][None, :]
    rel0 = (r2 - 8).astype(np.int32)[None, :]
    a, r = np.divmod(np.arange(128), 32)
    a2, r2 = np.divmod(np.arange(256), 64)
    d1 = 4 * (r[:, None] - r2[None, :] + 32) + a[:, None] - a2[None, :]
    rel1 = (r2 - 32).astype(np.int32)[None, :]
    r = np.arange(128)
    r2 = np.arange(256)
    d2 = r[:, None] - r2[None, :] + 128
    rel2 = (r2 - 128).astype(np.int32)[None, :]
    return (jnp.asarray(bias(d0)), jnp.asarray(bias(d1)), jnp.asarray(bias(d2)),
            jnp.asarray(rel0), jnp.asarray(rel1), jnp.asarray(rel2))


def _attn_prompt(q_r, k_r, v_r):
    n_i = q_r.shape[1]
    s = n_i * N_CLASS
    nsteps = n_i // AT_I
    b0, b1, b2, r0, r1, r2 = _attn_bias_tables()
    blk = (N_CLASS, AT_I, ATTN_WIDTH)
    cur = lambda n, g: (0, n, g)
    prev = lambda n, g: (0, jnp.maximum(n - 1, 0), g)
    return pl.pallas_call(
        _attn_prompt_kernel,
        grid=(nsteps, N_DIL),
        in_specs=[
            pl.BlockSpec(blk, cur), pl.BlockSpec(blk, cur), pl.BlockSpec(blk, prev),
            pl.BlockSpec(blk, cur), pl.BlockSpec(blk, prev),
            _const_spec(b0.shape), _const_spec(b1.shape), _const_spec(b2.shape),
            _const_spec(r0.shape), _const_spec(r1.shape), _const_spec(r2.shape),
        ],
        out_specs=pl.BlockSpec((AT_I * N_CLASS, ATTN_WIDTH), lambda n, g: (n, 0)),
        out_shape=jax.ShapeDtypeStruct((s, ATTN_WIDTH), BF16),
        scratch_shapes=[
            pltpu.VMEM((N_CLASS, 2 * AT_I, ATTN_WIDTH), BF16),
            pltpu.VMEM((N_CLASS, 2 * AT_I, ATTN_WIDTH), BF16),
            pltpu.VMEM((N_CLASS, AT_I, ATTN_WIDTH), F32),
            pltpu.VMEM((N_CLASS, AT_I, N_HEADS * LANES), F32),
            pltpu.VMEM((N_CLASS, AT_I, ATTN_WIDTH), F32),
            pltpu.VMEM((ATTN_WIDTH // LANES, AT_I * N_CLASS, LANES), F32),
        ],
        compiler_params=pltpu.CompilerParams(
            dimension_semantics=("arbitrary", "arbitrary"), vmem_limit_bytes=ATTN_VMEM_LIMIT_BYTES),
        name="attn_prompt",
    )(q_r, k_r, k_r, v_r, v_r, b0, b1, b2, r0, r1, r2)


def _proj_sample_kernel(n_b, n_t, x_ref, xb_ref, n1w_ref, w_ref, qw_ref, kw_ref, cos_ref, sina_ref, sinb_ref,
                        seg_ref, segt_ref, poolw_ref, pools_ref, state_ref,
                        ya_ref, gates_ref, q_ref, k_ref, v_ref, u_ref, kt_ref, vt_ref):
    xn = _rms(x_ref[...], n1w_ref[...]).astype(BF16)
    u = jnp.dot(xn, w_ref[:, C_U:C_U + POOL_WIDTH], preferred_element_type=F32)
    u_ref[...] = u

    def ext(j, lanes):
        if j < POOL_STATE:
            return state_ref[j, :, lanes]
        return u[(j - POOL_STATE) * n_b:(j - POOL_STATE + 1) * n_b, lanes]

    pooled = []
    for g, w in enumerate(POOL_WINDOWS):
        lanes = slice(g * POOL_GROUP_DIM, (g + 1) * POOL_GROUP_DIM)
        rows = []
        for i in range(n_t):
            s = ext(POOL_STATE + i, lanes)
            for m in range(1, w):
                s = s + ext(POOL_STATE + i - m, lanes)
            cnt = float(min(PAST_LEN + i + 1, w))
            rows.append(s / cnt - ext(POOL_STATE + i, lanes))
        pooled.append(jnp.concatenate(rows, axis=0))
    ya_ref[...] = _pool_mix_out(pooled, poolw_ref, pools_ref[...]).astype(BF16)

    gl = jnp.dot(xn, w_ref[:, C_G:C_G + N_BRANCH * D_MODEL], preferred_element_type=F32)
    gates_ref[...] = jax.nn.sigmoid(gl).astype(BF16)

    xnb = _rms(xb_ref[...], n1w_ref[...]).astype(BF16)
    cos, sina, sinb = cos_ref[...], sina_ref[...], sinb_ref[...]
    seg, segt = seg_ref[...], segt_ref[...]
    yq = jnp.dot(xnb, w_ref[:, C_Q:C_Q + QKV_WIDTH], preferred_element_type=F32)
    q_ref[...] = _head_norm_rope(yq, qw_ref[...], seg, segt, cos, sina, sinb)
    yk = jnp.dot(xnb, w_ref[:, C_K:C_K + QKV_WIDTH], preferred_element_type=F32)
    kr = _head_norm_rope(yk, kw_ref[...], seg, segt, cos, sina, sinb)
    k_ref[...] = kr
    kt_ref[...] = kr.T
    yv = jnp.dot(xnb, w_ref[:, C_V:C_V + QKV_WIDTH], preferred_element_type=F32)
    v_ref[...] = yv
    vt_ref[...] = yv.T


def _proj_sample(x, xb, n1w, w_bf, qw_row, kw_row, poolw_bf, pool_scale, state_t, n_b, n_t):
    rows = n_b * n_t
    pos = PAST_LEN + jnp.arange(rows, dtype=jnp.int32) % n_t
    cos, sina, sinb = _rope_tables(pos)
    seg, segt = _seg_mats()
    row = lambda a: a.reshape(1, -1)
    args = (x, xb, row(n1w), w_bf, row(qw_row), row(kw_row), cos, sina, sinb, seg, segt,
            poolw_bf, row(pool_scale), state_t)
    f32_qkv = jax.ShapeDtypeStruct((rows, QKV_WIDTH), F32)
    out_shape = [
        jax.ShapeDtypeStruct((rows, POOL_WIDTH), BF16),
        jax.ShapeDtypeStruct((rows, N_BRANCH * D_MODEL), BF16),
        f32_qkv, f32_qkv, f32_qkv,
        jax.ShapeDtypeStruct((rows, POOL_WIDTH), F32),
        jax.ShapeDtypeStruct((QKV_WIDTH, rows), F32),
        jax.ShapeDtypeStruct((QKV_WIDTH, rows), F32),
    ]
    return pl.pallas_call(
        functools.partial(_proj_sample_kernel, n_b, n_t),
        grid=(1,),
        in_specs=[_const_spec(a.shape) for a in args],
        out_specs=[_const_spec(o.shape) for o in out_shape],
        out_shape=out_shape,
        compiler_params=pltpu.CompilerParams(
            dimension_semantics=("arbitrary",), vmem_limit_bytes=VMEM_LIMIT_BYTES),
        name="proj_sample",
    )(*args)


HEADS_PER_STEP = 8


def _attn_roll_kernel(n_t, q_ref, kn_ref, vn_ref, newt_ref, c0_ref, c1_ref, c2_ref,
                      b0_ref, b1_ref, b2_ref, bn_ref, o_ref, o0_ref, o1_ref, o2_ref):
    caches = (c0_ref, c1_ref, c2_ref)
    outs = (o0_ref, o1_ref, o2_ref)
    biases = (b0_ref, b1_ref, b2_ref)
    nt_contract = (((1,), (1,)), ((), ()))
    col = pl.program_id(1) * n_t
    new_tile = pl.multiple_of((col // LANES) * LANES, LANES)
    new_shift = (LANES - n_t) - col % LANES
    for hh in range(HEADS_PER_STEP):
        scores, values = [], []
        for g in range(N_DIL):
            q = q_ref[hh, g].astype(BF16)
            kc = caches[g][0, hh].astype(BF16)
            scores.append(jnp.dot(q, kc, preferred_element_type=F32) + biases[g][...])
            kn = kn_ref[hh, g].astype(BF16)
            scores.append(lax.dot_general(q, kn, nt_contract, preferred_element_type=F32) + bn_ref[g])
            values.append(caches[g][1, hh].astype(BF16))
            values.append(vn_ref[hh, g].astype(BF16))
        m = functools.reduce(jnp.maximum, [jnp.max(s, axis=1, keepdims=True) for s in scores])
        l = jnp.zeros_like(m)
        o = jnp.zeros((n_t, HEAD_DIM), F32)
        for idx, (s, v) in enumerate(zip(scores, values)):
            p = jnp.exp(s - m)
            l = l + jnp.sum(p, axis=1, keepdims=True)
            pb = p.astype(BF16)
            if idx % 2 == 0:
                o = o + lax.dot_general(pb, v, nt_contract, preferred_element_type=F32)
            else:
                o = o + jnp.dot(pb, v, preferred_element_type=F32)
        o_ref[hh] = o / l

        for g in range(N_DIL):
            length = caches[g].shape[-1]
            lane = lax.broadcasted_iota(jnp.int32, (HEAD_DIM, LANES), 1)
            for kv in range(2):
                rolled = pltpu.roll(caches[g][kv, hh], length - n_t, 1)
                new = pltpu.roll(newt_ref[kv, g, hh, :, pl.ds(new_tile, LANES)], new_shift, 1)
                if length > LANES:
                    outs[g][kv, hh, :, 0:length - LANES] = rolled[:, 0:length - LANES]
                outs[g][kv, hh, :, length - LANES:length] = jnp.where(
                    lane >= LANES - n_t, new, rolled[:, length - LANES:length])


def _attn_roll_bias(n_t):
    i = np.arange(n_t)
    bias, bias_new = [], []
    for win, dil in DILATION_PAIRS:
        length = min(win, PAST_LEN)
        delta = length + i[:, None] - np.arange(length)[None, :]
        ok = (delta % dil == 0) & (delta // dil >= 1) & (delta // dil <= win // dil)
        bias.append(jnp.asarray(np.where(ok, 0.0, NEG).astype(np.float32)))
        dn = i[:, None] - i[None, :]
        okn = (dn >= 0) & (dn % dil == 0) & (dn // dil <= win // dil)
        bias_new.append(np.where(okn, 0.0, NEG).astype(np.float32))
    return bias, jnp.asarray(np.stack(bias_new))


def _attn_roll(q_h, kn_h, vn_h, new_t, caches_t):
    n_b, _, _, n_t, _ = q_h.shape
    bias, bias_new = _attn_roll_bias(n_t)
    hs = HEADS_PER_STEP
    qspec = pl.BlockSpec((None, hs, N_DIL, n_t, HEAD_DIM), lambda h, b: (b, h, 0, 0, 0))
    cspec = lambda c: pl.BlockSpec((None, 2, hs, HEAD_DIM, c.shape[-1]), lambda h, b: (b, 0, h, 0, 0))
    return pl.pallas_call(
        functools.partial(_attn_roll_kernel, n_t),
        grid=(N_HEADS // hs, n_b),
        in_specs=[qspec, qspec, qspec,
                  pl.BlockSpec((2, N_DIL, hs, HEAD_DIM, n_t * n_b), lambda h, b: (0, 0, h, 0, 0))]
                 + [cspec(c) for c in caches_t]
                 + [_const_spec(b.shape) for b in bias] + [_const_spec(bias_new.shape)],
        out_specs=[pl.BlockSpec((None, hs, n_t, HEAD_DIM), lambda h, b: (b, h, 0, 0))]
                  + [cspec(c) for c in caches_t],
        out_shape=[jax.ShapeDtypeStruct((n_b, N_HEADS, n_t, HEAD_DIM), F32)]
                  + [jax.ShapeDtypeStruct(c.shape, c.dtype) for c in caches_t],
        compiler_params=pltpu.CompilerParams(
            dimension_semantics=("arbitrary", "arbitrary"), vmem_limit_bytes=VMEM_LIMIT_BYTES),
        name="attn_roll",
    )(q_h, kn_h, vn_h, new_t, *caches_t, *bias, bias_new)


TM = 512
MOE_ROWS = 512
L_E0, L_E1, L_W0, L_W1, L_R0, L_R1 = 0, 1, 2, 3, 4, 5
L_GROUP = N_EXPERTS


def _lane_min_index(mask, lane):
    return jnp.min(jnp.where(mask, lane, float(LANES)), axis=1, keepdims=True)


def _zero_fill_step(n, xs_ref, zbuf, sem):
    slab = zbuf.shape[0]
    n_full, rem = divmod(xs_ref.shape[0], slab)
    full = lambda i: pltpu.make_async_copy(
        zbuf, xs_ref.at[pl.ds(pl.multiple_of(i * slab, MOE_ROWS), slab)], sem)
    tail = pltpu.make_async_copy(zbuf.at[pl.ds(0, max(rem, 1))],
                                 xs_ref.at[pl.ds(n_full * slab, max(rem, 1))], sem)

    @pl.when(n == 0)
    def _():
        zbuf[...] = jnp.zeros(zbuf.shape, F32)
        if rem:
            tail.start()

    @pl.when(n < n_full)
    def _():
        full(n).start()

    @pl.when(n == pl.num_programs(0) - 1)
    def _():
        for _ in range(n_full):
            full(0).wait()
        if rem:
            tail.wait()


def _merge_route_kernel(n_first, xa_ref, xb_ref, yaa_ref, yab_ref, yba_ref, ybb_ref, ga_ref, gb_ref,
                        wbp_ref, wba_ref, wo_ref, n2w_ref, wrh_ref, wrl_ref, br_ref, ltri_ref,
                        x1_ref, h_ref, route_ref, cnt_ref, xs_ref, zbuf, sem):
    n = pl.program_id(0)
    _zero_fill_step(n, xs_ref, zbuf, sem)
    first = n < n_first
    pick = lambda a, b: jnp.where(first, a[...], b[...])
    pa = jnp.dot(pick(yaa_ref, yab_ref), wbp_ref[...], preferred_element_type=F32)
    pb = jnp.dot(pick(yba_ref, ybb_ref), wba_ref[...], preferred_element_type=F32)
    gts = pick(ga_ref, gb_ref)
    merged = gts[:, :D_MODEL].astype(F32) * pa + gts[:, D_MODEL:].astype(F32) * pb
    x1 = pick(xa_ref, xb_ref) + jnp.dot(merged.astype(BF16), wo_ref[...], preferred_element_type=F32)
    x1_ref[...] = x1
    h = _rms(x1, n2w_ref[...])
    h_ref[...] = h

    h_hi = h.astype(BF16)
    h_lo = (h - h_hi.astype(F32)).astype(BF16)
    both = jnp.dot(h_hi, jnp.concatenate([wrh_ref[...], wrl_ref[...]], axis=1), preferred_element_type=F32)
    logits = (both[:, :LANES] + both[:, LANES:]
              + jnp.dot(h_lo, wrh_ref[...], preferred_element_type=F32)) + br_ref[...]
    rows = logits.shape[0]
    lane = lax.broadcasted_iota(jnp.int32, (rows, LANES), 1).astype(F32)
    is_group = (lane >= L_GROUP) & (lane < L_GROUP + N_EXPERT_GROUPS)
    gl = jnp.where(is_group, logits, NEG)
    gmax = jnp.max(gl, axis=1, keepdims=True)
    g_w = 1.0 / jnp.sum(jnp.exp(gl - gmax), axis=1, keepdims=True)
    g_sel = _lane_min_index(gl == gmax, lane) - L_GROUP
    in_group = (lane >= g_sel * EXPERTS_PER_GROUP) & (lane < (g_sel + 1.0) * EXPERTS_PER_GROUP)
    el = jnp.where(in_group, logits, NEG)
    v0 = jnp.max(el, axis=1, keepdims=True)
    e0 = _lane_min_index(el == v0, lane)
    el2 = jnp.where(lane == e0, NEG, el)
    v1 = jnp.max(el2, axis=1, keepdims=True)
    e1 = _lane_min_index(el2 == v1, lane)
    t = jnp.exp(v1 - v0)
    w0 = g_w / (1.0 + t)
    w1 = g_w * t / (1.0 + t)

    hot0 = lane == e0
    hot1 = lane == e1
    onehot = jnp.where(hot0 | hot1, 1.0, 0.0)

    @pl.when(n == 0)
    def _():
        cnt_ref[...] = jnp.zeros(cnt_ref.shape, F32)

    before = jnp.dot(ltri_ref[...], onehot.astype(BF16), preferred_element_type=F32) + cnt_ref[...]
    r0 = jnp.sum(jnp.where(hot0, before, 0.0), axis=1, keepdims=True)
    r1 = jnp.sum(jnp.where(hot1, before, 0.0), axis=1, keepdims=True)
    cnt_ref[...] = cnt_ref[...] + jnp.sum(onehot, axis=0, keepdims=True)

    rec = jnp.zeros((rows, LANES), F32)
    for ln, val in ((L_E0, e0), (L_E1, e1), (L_W0, w0), (L_W1, w1), (L_R0, r0), (L_R1, r1)):
        rec = jnp.where(lane == ln, val, rec)
    route_ref[...] = rec


def _router_weights(w_rg, b_rg, w_re, b_re):
    w = jnp.zeros((D_MODEL, LANES), F32)
    w = w.at[:, :N_EXPERTS].set(w_re).at[:, L_GROUP:L_GROUP + N_EXPERT_GROUPS].set(w_rg)
    b = jnp.zeros((1, LANES), F32)
    b = b.at[0, :N_EXPERTS].set(b_re).at[0, L_GROUP:L_GROUP + N_EXPERT_GROUPS].set(b_rg)
    w_hi = w.astype(BF16)
    w_lo = (w - w_hi.astype(F32)).astype(BF16)
    return w_hi, w_lo, b


def _merge_route(xa, xb, yaa, yab, yba, ybb, ga, gb, wbp, wba, wo, n2w, wr_hi, wr_lo, br):
    n_first = xa.shape[0] // TM
    t_all = xa.shape[0] + xb.shape[0]
    n_rows = _moe_blocks(t_all) * MOE_ROWS
    slab_rows = -(-n_rows // MOE_ROWS // (t_all // TM)) * MOE_ROWS
    assert n_rows // slab_rows <= t_all // TM
    ltri = jnp.asarray(np.tril(np.ones((TM, TM), np.float32), -1), BF16)
    ta = lambda w: pl.BlockSpec((TM, w), lambda n: (jnp.minimum(n, n_first - 1), 0))
    tb = lambda w: pl.BlockSpec((TM, w), lambda n: (jnp.maximum(n - n_first, 0), 0))
    tile = lambda w: pl.BlockSpec((TM, w), lambda n: (n, 0))
    return pl.pallas_call(
        functools.partial(_merge_route_kernel, n_first),
        grid=(t_all // TM,),
        in_specs=[
            ta(D_MODEL), tb(D_MODEL), ta(POOL_WIDTH), tb(POOL_WIDTH), ta(ATTN_WIDTH), tb(ATTN_WIDTH),
            ta(N_BRANCH * D_MODEL), tb(N_BRANCH * D_MODEL),
            _const_spec(wbp.shape), _const_spec(wba.shape), _const_spec(wo.shape),
            _const_spec((1, D_MODEL)),
            _const_spec(wr_hi.shape), _const_spec(wr_lo.shape), _const_spec(br.shape),
            _const_spec(ltri.shape),
        ],
        out_specs=[tile(D_MODEL), tile(D_MODEL), tile(LANES), _const_spec((1, LANES)),
                   pl.BlockSpec(memory_space=pl.ANY)],
        out_shape=[
            jax.ShapeDtypeStruct((t_all, D_MODEL), F32),
            jax.ShapeDtypeStruct((t_all, D_MODEL), F32),
            jax.ShapeDtypeStruct((t_all, LANES), F32),
            jax.ShapeDtypeStruct((1, LANES), F32),
            jax.ShapeDtypeStruct((n_rows, D_MODEL), F32),
        ],
        scratch_shapes=[pltpu.VMEM((slab_rows, D_MODEL), F32), pltpu.SemaphoreType.DMA],
        compiler_params=pltpu.CompilerParams(
            dimension_semantics=("arbitrary",), vmem_limit_bytes=VMEM_LIMIT_BYTES),
        name="merge_route",
    )(xa, xb, yaa, yab, yba, ybb, ga, gb, wbp, wba, wo, n2w.reshape(1, -1), wr_hi, wr_lo, br, ltri)


def _moe_blocks(n_tokens):
    return -(-(2 * n_tokens + N_EXPERTS * (MOE_ROWS - 1)) // MOE_ROWS)


def _plan_kernel(nb_pad, cnt_ref, route_ref, upper_ref, dest_ref, blk_ref):
    n = pl.program_id(0)
    cnt = cnt_ref[...]
    nblk = jnp.floor((cnt + (MOE_ROWS - 1)) * (1.0 / MOE_ROWS))
    nb8 = jnp.broadcast_to(nblk, (8, LANES)).astype(BF16)
    bstart = jnp.dot(nb8, upper_ref[...], preferred_element_type=F32)[0:1]
    bend = bstart + nblk
    pstart = bstart * MOE_ROWS

    rec = route_ref[...]
    rows = rec.shape[0]
    lane = lax.broadcasted_iota(jnp.int32, (rows, LANES), 1).astype(F32)
    col = lambda ln: jnp.sum(jnp.where(lane == ln, rec, 0.0), axis=1, keepdims=True)
    look = lambda e: jnp.sum(jnp.where(lane == e, pstart, 0.0), axis=1, keepdims=True)
    d0 = look(col(L_E0)) + col(L_R0)
    d1 = look(col(L_E1)) + col(L_R1)
    dest_ref[...] = jnp.where(lane == 0, d0, jnp.where(lane == 1, d1, 0.0)).astype(jnp.int32)

    @pl.when(n == 0)
    def _():
        b = lax.broadcasted_iota(jnp.int32, (nb_pad, LANES), 0).astype(F32)
        lane_b = lax.broadcasted_iota(jnp.int32, (nb_pad, LANES), 1)
        done = jnp.where((bend <= b) & (lane_b < N_EXPERTS), 1.0, 0.0)
        e_of_b = jnp.minimum(jnp.sum(done, axis=1, keepdims=True), float(N_EXPERTS - 1))
        total = jnp.max(jnp.where(lane_b < N_EXPERTS, bend, 0.0), axis=1, keepdims=True)
        blk_ref[...] = jnp.where(lane_b == 0, e_of_b, jnp.where(lane_b == 1, total, 0.0)).astype(jnp.int32)


def _plan(cnt, route, n_rows):
    t = route.shape[0]
    nb = n_rows // MOE_ROWS
    nb_pad = -(-nb // 8) * 8
    upper = jnp.asarray(np.triu(np.ones((LANES, LANES), np.float32), 1), BF16)
    return pl.pallas_call(
        functools.partial(_plan_kernel, nb_pad),
        grid=(t // TM,),
        in_specs=[_const_spec((1, LANES)), pl.BlockSpec((TM, LANES), lambda n: (n, 0)),
                  _const_spec((LANES, LANES))],
        out_specs=[pl.BlockSpec((TM, LANES), lambda n: (n, 0)), _const_spec((nb_pad, LANES))],
        out_shape=[jax.ShapeDtypeStruct((t, LANES), jnp.int32),
                   jax.ShapeDtypeStruct((nb_pad, LANES), jnp.int32)],
        compiler_params=pltpu.CompilerParams(
            dimension_semantics=("arbitrary",), vmem_limit_bytes=VMEM_LIMIT_BYTES),
        name="moe_plan",
    )(cnt, route, upper)


def _dispatch_kernel(dest_ref, h_ref, xs_in, xs_out, sem):
    del xs_in
    rows = h_ref.shape[0]

    def issue(i, carry):
        base = pl.multiple_of(i * 8, 8)
        for j in range(8):
            for k in range(2):
                pltpu.make_async_copy(h_ref.at[pl.ds(base + j, 1)],
                                      xs_out.at[pl.ds(dest_ref[k, base + j], 1)], sem).start()
        return carry

    lax.fori_loop(0, rows // 8, issue, 0)
    for k in range(2):
        pltpu.make_async_copy(h_ref, xs_out.at[pl.ds(0, rows)], sem).wait()


def _dispatch(dest_t, h, xs):
    t = h.shape[0]
    return pl.pallas_call(
        _dispatch_kernel,
        grid=(t // TM,),
        in_specs=[pl.BlockSpec((None, 8, TM), lambda n: (n, 0, 0), memory_space=pltpu.SMEM),
                  pl.BlockSpec((TM, D_MODEL), lambda n: (n, 0)),
                  pl.BlockSpec(memory_space=pl.ANY)],
        out_specs=pl.BlockSpec(memory_space=pl.ANY),
        out_shape=jax.ShapeDtypeStruct(xs.shape, xs.dtype),
        input_output_aliases={2: 0},
        scratch_shapes=[pltpu.SemaphoreType.DMA],
        compiler_params=pltpu.CompilerParams(
            dimension_semantics=("arbitrary",), vmem_limit_bytes=VMEM_LIMIT_BYTES),
        name="moe_dispatch",
    )(dest_t, h, xs)


def _expert_kernel(blk_ref, x_ref, wg_ref, wu_ref, wd_ref, y_ref, wg_bf, wu_bf, wd_bf):
    b = pl.program_id(0)
    used = b < blk_ref[1, 0]

    @pl.when(used)
    def _():
        prev = blk_ref[0, jnp.maximum(b - 1, 0)]

        @pl.when((b == 0) | (blk_ref[0, b] != prev))
        def _():
            wg_bf[...] = wg_ref[...].astype(BF16)
            wu_bf[...] = wu_ref[...].astype(BF16)
            wd_bf[...] = wd_ref[...].astype(BF16)

        x = x_ref[...].astype(BF16)
        gate = jnp.dot(x, wg_bf[...], preferred_element_type=F32)
        up = jnp.dot(x, wu_bf[...], preferred_element_type=F32)
        mid = (jax.nn.silu(gate) * up).astype(BF16)
        y_ref[...] = jnp.dot(mid, wd_bf[...], preferred_element_type=F32)

    @pl.when(jnp.logical_not(used))
    def _():
        y_ref[...] = jnp.zeros(y_ref.shape, F32)


def _experts(blk_t, xs, w_g, w_u, w_d):
    n_rows = xs.shape[0]
    nb = n_rows // MOE_ROWS
    last = lambda b, blk: jnp.minimum(b, blk[1, 0] - 1)
    grid_spec = pltpu.PrefetchScalarGridSpec(
        num_scalar_prefetch=1,
        grid=(nb,),
        in_specs=[
            pl.BlockSpec((MOE_ROWS, D_MODEL), lambda b, blk: (last(b, blk), 0)),
            pl.BlockSpec((None, D_MODEL, D_EXPERT), lambda b, blk: (blk[0, last(b, blk)], 0, 0)),
            pl.BlockSpec((None, D_MODEL, D_EXPERT), lambda b, blk: (blk[0, last(b, blk)], 0, 0)),
            pl.BlockSpec((None, D_EXPERT, D_MODEL), lambda b, blk: (blk[0, last(b, blk)], 0, 0)),
        ],
        out_specs=pl.BlockSpec((MOE_ROWS, D_MODEL), lambda b, blk: (b, 0)),
        scratch_shapes=[pltpu.VMEM((D_MODEL, D_EXPERT), BF16), pltpu.VMEM((D_MODEL, D_EXPERT), BF16),
                        pltpu.VMEM((D_EXPERT, D_MODEL), BF16)],
    )
    return pl.pallas_call(
        _expert_kernel,
        grid_spec=grid_spec,
        out_shape=jax.ShapeDtypeStruct((n_rows, D_MODEL), F32),
        compiler_params=pltpu.CompilerParams(
            dimension_semantics=("arbitrary",), vmem_limit_bytes=VMEM_LIMIT_BYTES),
        name="moe_experts",
    )(blk_t, xs, w_g, w_u, w_d)


def _combine_kernel(n_first, dest_ref, x1_ref, route_ref, ys_ref, oa_ref, ob_ref, gbuf, sem):
    n = pl.program_id(0)
    rows = x1_ref.shape[0]

    def issue(i, carry):
        base = pl.multiple_of(i * 8, 8)
        for j in range(8):
            for k in range(2):
                pltpu.make_async_copy(ys_ref.at[pl.ds(dest_ref[k, base + j], 1)],
                                      gbuf.at[k, pl.ds(base + j, 1)], sem).start()
        return carry

    lax.fori_loop(0, rows // 8, issue, 0)
    for k in range(2):
        pltpu.make_async_copy(ys_ref.at[pl.ds(0, rows)], gbuf.at[k], sem).wait()
    rec = route_ref[...]
    lane = lax.broadcasted_iota(jnp.int32, rec.shape, 1)
    w0 = jnp.sum(jnp.where(lane == L_W0, rec, 0.0), axis=1, keepdims=True)
    w1 = jnp.sum(jnp.where(lane == L_W1, rec, 0.0), axis=1, keepdims=True)
    res = x1_ref[...] + (gbuf[0] * w0 + gbuf[1] * w1)

    @pl.when(n < n_first)
    def _():
        oa_ref[...] = res

    @pl.when(n >= n_first)
    def _():
        ob_ref[...] = res


def _combine(dest_t, x1, route, ys, t_first):
    t = x1.shape[0]
    n_first = t_first // TM
    return pl.pallas_call(
        functools.partial(_combine_kernel, n_first),
        grid=(t // TM,),
        in_specs=[pl.BlockSpec((None, 8, TM), lambda n: (n, 0, 0), memory_space=pltpu.SMEM),
                  pl.BlockSpec((TM, D_MODEL), lambda n: (n, 0)),
                  pl.BlockSpec((TM, LANES), lambda n: (n, 0)),
                  pl.BlockSpec(memory_space=pl.ANY)],
        out_specs=[pl.BlockSpec((TM, D_MODEL), lambda n: (jnp.minimum(n, n_first - 1), 0)),
                   pl.BlockSpec((TM, D_MODEL), lambda n: (jnp.maximum(n - n_first, 0), 0))],
        out_shape=[jax.ShapeDtypeStruct((t_first, D_MODEL), F32),
                   jax.ShapeDtypeStruct((t - t_first, D_MODEL), F32)],
        scratch_shapes=[pltpu.VMEM((2, TM, D_MODEL), F32), pltpu.SemaphoreType.DMA],
        compiler_params=pltpu.CompilerParams(
            dimension_semantics=("arbitrary",), vmem_limit_bytes=VMEM_LIMIT_BYTES),
        name="moe_combine",
    )(dest_t, x1, route, ys)


def kernel(x_prompt, x_sample, cache_kv_w128, cache_kv_w512, cache_kv_w2048, state_pool, norm1_w, w_in, q_norm_w, k_norm_w, pool_w, pool_scale, w_branch_pool, w_branch_attn, w_out, norm2_w, w_router_group, b_router_group, w_router_expert, b_router_expert, w_expert_gate, w_expert_up, w_expert_down):
    assert x_prompt.shape[0] == 1 and norm1_w.shape[0] == 1
    layer = 0
    s_p = x_prompt.shape[1]
    n_b, n_t, _ = x_sample.shape
    t_s = n_b * n_t
    t_all = s_p + t_s
    caches = (cache_kv_w128[layer], cache_kv_w512[layer], cache_kv_w2048[layer])

    w_bf = w_in[layer].astype(BF16)
    qw_row = jnp.tile(q_norm_w[layer][:, None, :], (1, N_HEADS, 1)).reshape(-1) * (HEAD_DIM ** -0.5)
    kw_row = jnp.tile(k_norm_w[layer][:, None, :], (1, N_HEADS, 1)).reshape(-1)
    poolw_bf = pool_w[layer].astype(BF16)
    wbp, wba, wo = (w.astype(BF16) for w in (w_branch_pool[layer], w_branch_attn[layer], w_out[layer]))
    wr_hi, wr_lo, br = _router_weights(w_router_group[layer], b_router_group[layer],
                                       w_router_expert[layer], b_router_expert[layer])

    ya_p, gates_p, q_r, k_r, v_r, kvt, ut = _proj_prompt(
        x_prompt[0], norm1_w[layer], w_bf, qw_row * LOG2_E, kw_row, poolw_bf, pool_scale[layer])
    yb_p = _attn_prompt(q_r, k_r, v_r)

    xs_t = jnp.transpose(x_sample, (1, 0, 2)).reshape(t_s, D_MODEL)
    state_t = jnp.transpose(state_pool[layer], (1, 0, 2))
    ya_s, gates_s, q_s, k_s, v_s, u_s, kt_s, vt_s = _proj_sample(
        xs_t, x_sample.reshape(t_s, D_MODEL), norm1_w[layer], w_bf, qw_row, kw_row, poolw_bf,
        pool_scale[layer], state_t, n_b, n_t)
    split = lambda a: a.reshape(n_b, n_t, N_DIL, N_HEADS, HEAD_DIM)
    per_head = lambda a: jnp.transpose(split(a), (0, 3, 2, 1, 4))
    new_t = jnp.stack([kt_s, vt_s]).reshape(2, N_DIL, N_HEADS, HEAD_DIM, t_s)
    caches_t = [jnp.transpose(c, (0, 2, 3, 4, 1)) for c in caches]
    o_s, *kv_t = _attn_roll(per_head(q_s), per_head(k_s), per_head(v_s), new_t, caches_t)
    yb_s = jnp.transpose(o_s, (2, 0, 1, 3)).reshape(t_s, ATTN_WIDTH).astype(BF16)
    kv_s = [jnp.transpose(c, (0, 4, 1, 2, 3)) for c in kv_t]

    x1, h, route, cnt, xs = _merge_route(x_prompt[0], xs_t, ya_p, ya_s, yb_p, yb_s, gates_p, gates_s,
                                     wbp, wba, wo, norm2_w[layer], wr_hi, wr_lo, br)

    n_rows = _moe_blocks(t_all) * MOE_ROWS
    dest, blk = _plan(cnt, route, n_rows)
    dest_t = jnp.transpose(dest[:, :8].reshape(t_all // TM, TM, 8), (0, 2, 1))
    blk_t = jnp.transpose(blk[:, :2])
    xs = _dispatch(dest_t, h, xs)
    ys = _experts(blk_t, xs, w_expert_gate[layer], w_expert_up[layer], w_expert_down[layer])
    y_p, y_s = _combine(dest_t, x1, route, ys, s_p)

    y_prompt = y_p[None]
    y_sample = jnp.transpose(y_s.reshape(n_t, n_b, D_MODEL), (1, 0, 2))
    pool_prompt = ut[1:][None, None]
    u_new = jnp.transpose(u_s.reshape(n_t, n_b, POOL_WIDTH), (1, 0, 2))
    pool_sample = jnp.concatenate([state_pool[layer][:, n_t:], u_new], axis=1)[None]
    kv_t = kvt.reshape(2, N_DIL, N_HEADS, HEAD_DIM, kvt.shape[1])
    outs = [y_prompt, y_sample, pool_prompt, pool_sample]
    for g, (win, _) in enumerate(DILATION_PAIRS):
        keep = min(win, s_p)
        kv_g = kv_t[:, g, :, :, kv_t.shape[-1] - keep:]
        outs.append(jnp.transpose(kv_g, (3, 0, 1, 2))[None, None])
        outs.append(kv_s[g][None])
    return tuple(outs)
```

```python
import functools

import numpy as np
import jax
import jax.numpy as jnp
from jax import lax
from jax.experimental import pallas as pl
from jax.experimental.pallas import tpu as pltpu

F32 = jnp.float32
BF16 = jnp.bfloat16

D_MODEL = 1024
PAST_LEN = 8192
POOL_WIDTH = 512
POOL_WINDOWS = (2, 4, 8, 16)
POOL_GROUP_DIM = 128
POOL_STATE = 15
HEAD_DIM = 64
N_HEADS = 8
DILATION_PAIRS = ((128, 1), (512, 4), (2048, 16))
N_DIL = 3
ATTN_WIDTH = 512
QKV_WIDTH = 1536
ROT_DIM = 16
ROPE_THETA = 500000.0
N_BRANCH = 2
IN_COLS = POOL_WIDTH + 3 * QKV_WIDTH + N_BRANCH * D_MODEL
N_EXPERT_GROUPS = 4
EXPERTS_PER_GROUP = 8
N_EXPERTS = 32
D_EXPERT = 512
EPS = 1e-6

LANES = 128
VMEM_LIMIT_BYTES = 56 * 1024 * 1024
ATTN_VMEM_LIMIT_BYTES = 60 * 1024 * 1024

C_U = 0
C_Q = POOL_WIDTH
C_K = C_Q + QKV_WIDTH
C_V = C_K + QKV_WIDTH
C_G = C_V + QKV_WIDTH

N_CLASS = 16
TI = 256
ROWS_PER_CLASS = TI // N_CLASS
NEG = -1e30
LOG2_E = 1.4426950408889634


def _class_order():
    return [(c // 4) + 4 * (c % 4) for c in range(N_CLASS)]


def _const_spec(shape, single_buffer=False):
    nd = len(shape)
    kw = {}
    if single_buffer:
        kw["pipeline_mode"] = pl.Buffered(1)
    return pl.BlockSpec(shape, lambda *_: (0,) * nd, **kw)


def _rms(x, w):
    ms = jnp.mean(x * x, axis=-1, keepdims=True)
    return x * lax.rsqrt(ms + EPS) * w


def _head_norm_rope(y, wrow, seg, segt, cos, sina, sinb):
    sq = (y * y).astype(BF16)
    ssum = jnp.dot(sq, seg, preferred_element_type=F32)
    r = lax.rsqrt(ssum * (1.0 / HEAD_DIM) + EPS)
    r_hi = r.astype(BF16)
    r_lo = (r - r_hi.astype(F32)).astype(BF16)
    rexp = jnp.dot(jnp.concatenate([r_hi, r_lo], axis=1), segt, preferred_element_type=F32)
    yn = y * rexp * wrow
    outs = []
    for j in range(QKV_WIDTH // LANES):
        c = yn[:, j * LANES:(j + 1) * LANES]
        outs.append(c * cos + pltpu.roll(c, LANES - ROT_DIM // 2, 1) * sina
                    + pltpu.roll(c, ROT_DIM // 2, 1) * sinb)
    return jnp.concatenate(outs, axis=1)


def _pool_mix_out(pooled, poolw_ref, pool_scale):
    outs = []
    for g in range(len(POOL_WINDOWS)):
        outs.append(jnp.dot(pooled[g].astype(BF16), poolw_ref[g], preferred_element_type=F32))
    return jnp.concatenate(outs, axis=1) * pool_scale


def _split3(x):
    hi = x.astype(BF16)
    r1 = x - hi.astype(F32)
    mid = r1.astype(BF16)
    lo = (r1 - mid.astype(F32)).astype(BF16)
    return hi, mid, lo


def _proj_prompt_kernel(tail_tiles, x_ref, n1w_ref, w_ref, qw_ref, kw_ref, cosr_ref, sinr_ref, cosn_ref,
                        sinn_ref, rotm_ref, perm_ref, permt_ref, seg_ref, segt_ref, poolw_ref, pools_ref,
                        ya_ref, gates_ref, q_ref, k_ref, v_ref, kvt_ref, ut_ref,
                        uext_ref):
    n = pl.program_id(0)
    xn = _rms(x_ref[...], n1w_ref[...]).astype(BF16)

    u = jnp.dot(xn, w_ref[:, C_U:C_U + POOL_WIDTH], preferred_element_type=F32)

    @pl.when(n == 0)
    def _():
        uext_ref[0:16, :] = jnp.zeros((16, POOL_WIDTH), F32)

    uext_ref[16:16 + TI, :] = u
    pos = n * TI + lax.broadcasted_iota(jnp.int32, (TI, 1), 0)
    pooled = []
    for g, w in enumerate(POOL_WINDOWS):
        lanes = slice(g * POOL_GROUP_DIM, (g + 1) * POOL_GROUP_DIM)
        s = uext_ref[16:16 + TI, lanes]
        for m in range(1, w):
            s = s + uext_ref[16 - m:16 - m + TI, lanes]
        cnt = jnp.minimum(pos + 1, w).astype(F32)
        pooled.append(s / cnt - u[:, lanes])
    ya_ref[...] = _pool_mix_out(pooled, poolw_ref, pools_ref[...]).astype(BF16)
    ut_ref[...] = u[TI - 16:, :]
    uext_ref[0:16, :] = uext_ref[TI:TI + 16, :]

    gl = jnp.dot(xn, w_ref[:, C_G:C_G + N_BRANCH * D_MODEL], preferred_element_type=F32)
    gates_ref[...] = jax.nn.sigmoid(gl).astype(BF16)

    xp = jnp.dot(perm_ref[...], xn, preferred_element_type=F32).astype(BF16)
    cn, sn = cosn_ref[pl.ds(n, 1), :], sinn_ref[pl.ds(n, 1), :]
    cr, sr = cosr_ref[...], sinr_ref[...]
    cos = cn * cr - sn * sr
    sin = sn * cr + cn * sr
    sina, sinb = sin * rotm_ref[0:1, :], sin * rotm_ref[1:2, :]
    seg, segt = seg_ref[...], segt_ref[...]
    shp = (N_CLASS, ROWS_PER_CLASS, QKV_WIDTH)
    yq = jnp.dot(xp, w_ref[:, C_Q:C_Q + QKV_WIDTH], preferred_element_type=F32)
    q_ref[...] = _head_norm_rope(yq, qw_ref[...], seg, segt, cos, sina, sinb).astype(BF16).reshape(shp)
    yk = jnp.dot(xp, w_ref[:, C_K:C_K + QKV_WIDTH], preferred_element_type=F32)
    kr = _head_norm_rope(yk, kw_ref[...], seg, segt, cos, sina, sinb)
    k_ref[...] = kr.astype(BF16).reshape(shp)
    yv = jnp.dot(xp, w_ref[:, C_V:C_V + QKV_WIDTH], preferred_element_type=F32)
    v_ref[...] = yv.astype(BF16).reshape(shp)

    @pl.when(n >= pl.num_programs(0) - tail_tiles)
    def _():
        def natural_t(val):
            nat = sum(jnp.dot(permt_ref[...], part, preferred_element_type=F32) for part in _split3(val))
            return nat.T
        kvt_ref[0:QKV_WIDTH, :] = natural_t(kr)
        kvt_ref[QKV_WIDTH:2 * QKV_WIDTH, :] = natural_t(yv)


def _rope_tables(pos):
    half = ROT_DIM // 2
    inv = ROPE_THETA ** (-(jnp.arange(half, dtype=F32) * 2.0 / ROT_DIM))
    ang = pos.astype(F32)[:, None] * inv[None, :]
    c, s = jnp.cos(ang), jnp.sin(ang)
    n = pos.shape[0]
    ones = jnp.ones((n, HEAD_DIM - ROT_DIM), F32)
    zeros = jnp.zeros((n, HEAD_DIM - ROT_DIM), F32)
    z8 = jnp.zeros((n, half), F32)
    cos64 = jnp.concatenate([c, c, ones], axis=1)
    sina64 = jnp.concatenate([-s, z8, zeros], axis=1)
    sinb64 = jnp.concatenate([z8, s, zeros], axis=1)
    rep = lambda t: jnp.concatenate([t, t], axis=1)
    return rep(cos64), rep(sina64), rep(sinb64)


def _rope_lane_freq():
    half = ROT_DIM // 2
    inv = ROPE_THETA ** (-(np.arange(half, dtype=np.float64) * 2.0 / ROT_DIM))
    d = np.arange(LANES) % HEAD_DIM
    freq = np.where(d < ROT_DIM, inv[d % half], 0.0)
    mask = np.zeros((8, LANES))
    mask[0] = np.where(d < half, -1.0, 0.0)
    mask[1] = np.where((d >= half) & (d < ROT_DIM), 1.0, 0.0)
    return freq, mask


def _seg_mats():
    h = np.arange(QKV_WIDTH) // HEAD_DIM
    seg = (h[:, None] == np.arange(LANES)[None, :]).astype(np.float32)
    return jnp.asarray(seg, BF16), jnp.asarray(np.concatenate([seg.T, seg.T], axis=0), BF16)


def _perm_mat():
    cls = _class_order()
    p = np.zeros((TI, TI), np.float32)
    for c in range(N_CLASS):
        for j in range(ROWS_PER_CLASS):
            p[c * ROWS_PER_CLASS + j, N_CLASS * j + cls[c]] = 1.0
    return p


def _proj_prompt(x, n1w, w_bf, qw_row, kw_row, poolw_bf, pool_scale):
    s = x.shape[0]
    nt = s // TI
    cls = np.asarray(_class_order())
    perm = _perm_mat()
    j = np.arange(ROWS_PER_CLASS)
    inner = (N_CLASS * j[None, :] + cls[:, None]).reshape(-1).astype(np.float64)
    freq, rot_mask = _rope_lane_freq()
    ang_r = inner[:, None] * freq[None, :]
    ang_n = (np.arange(nt, dtype=np.float64) * TI)[:, None] * freq[None, :]
    f32c = lambda a: jnp.asarray(a.astype(np.float32))
    cosr, sinr, cosn, sinn = f32c(np.cos(ang_r)), f32c(np.sin(ang_r)), f32c(np.cos(ang_n)), f32c(np.sin(ang_n))
    rotm = f32c(rot_mask)
    seg, segt = _seg_mats()
    n_i = s // N_CLASS
    tail_tokens = min(DILATION_PAIRS[-1][0], s)
    tail_tiles = tail_tokens // TI
    row = lambda a: a.reshape(1, -1)
    tile = lambda n: (n, 0)
    rblk = lambda n: (0, n, 0)
    tblk = lambda n: (0, jnp.maximum(n - (nt - tail_tiles), 0))
    outs = pl.pallas_call(
        functools.partial(_proj_prompt_kernel, tail_tiles),
        grid=(nt,),
        in_specs=[
            pl.BlockSpec((TI, D_MODEL), tile),
            _const_spec((1, D_MODEL)),
            _const_spec((D_MODEL, IN_COLS), single_buffer=True),
            _const_spec((1, QKV_WIDTH)),
            _const_spec((1, QKV_WIDTH)),
            _const_spec((TI, LANES)), _const_spec((TI, LANES)),
            _const_spec((nt, LANES)), _const_spec((nt, LANES)), _const_spec((8, LANES)),
            _const_spec((TI, TI)), _const_spec((TI, TI)),
            _const_spec((QKV_WIDTH, LANES)),
            _const_spec((2 * LANES, QKV_WIDTH)),
            _const_spec((len(POOL_WINDOWS), POOL_GROUP_DIM, POOL_GROUP_DIM)),
            _const_spec((1, POOL_WIDTH)),
        ],
        out_specs=[
            pl.BlockSpec((TI, POOL_WIDTH), tile),
            pl.BlockSpec((TI, N_BRANCH * D_MODEL), tile),
            pl.BlockSpec((N_CLASS, ROWS_PER_CLASS, QKV_WIDTH), rblk),
            pl.BlockSpec((N_CLASS, ROWS_PER_CLASS, QKV_WIDTH), rblk),
            pl.BlockSpec((N_CLASS, ROWS_PER_CLASS, QKV_WIDTH), rblk),
            pl.BlockSpec((2 * QKV_WIDTH, TI), tblk),
            _const_spec((16, POOL_WIDTH)),
        ],
        out_shape=[
            jax.ShapeDtypeStruct((s, POOL_WIDTH), BF16),
            jax.ShapeDtypeStruct((s, N_BRANCH * D_MODEL), BF16),
            jax.ShapeDtypeStruct((N_CLASS, n_i, QKV_WIDTH), BF16),
            jax.ShapeDtypeStruct((N_CLASS, n_i, QKV_WIDTH), BF16),
            jax.ShapeDtypeStruct((N_CLASS, n_i, QKV_WIDTH), BF16),
            jax.ShapeDtypeStruct((2 * QKV_WIDTH, tail_tokens), F32),
            jax.ShapeDtypeStruct((16, POOL_WIDTH), F32),
        ],
        scratch_shapes=[pltpu.VMEM((16 + TI, POOL_WIDTH), F32)],
        compiler_params=pltpu.CompilerParams(
            dimension_semantics=("arbitrary",), vmem_limit_bytes=VMEM_LIMIT_BYTES),
        name="proj_prompt",
    )(x, row(n1w), w_bf, row(qw_row), row(kw_row), cosr, sinr, cosn, sinn, rotm,
      jnp.asarray(perm, BF16), jnp.asarray(perm.T, BF16), seg, segt, poolw_bf, row(pool_scale))
    return outs


AT_I = 128
N_KEYS_BACK = 128
BLOCKS_PER_TRIP = 4


def _attn_update(q, k, v, bias, m_old, l_old, acc_old):
    nq, nk = q.shape[0], k.shape[0]
    assert nk % LANES == 0
    lane = lax.broadcasted_iota(jnp.int32, (nq, LANES), 1)
    first = lane < HEAD_DIM
    bias2 = jnp.concatenate([bias, bias], axis=0)
    ones = jnp.ones((nk, LANES), BF16)
    m_out, l_out, a_out = [], [], []
    for hp in range(N_HEADS // 2):
        sl = slice(hp * LANES, (hp + 1) * LANES)
        qp, kp, vp = q[:, sl], k[:, sl], v[:, sl]
        lo, ao = l_old[:, sl], acc_old[:, sl]
        mo2 = jnp.concatenate([m_old[:, (2 * hp) * LANES:(2 * hp + 1) * LANES],
                               m_old[:, (2 * hp + 1) * LANES:(2 * hp + 2) * LANES]], axis=0)
        zero = jnp.zeros_like(qp)
        q2 = jnp.concatenate([jnp.where(first, qp, zero), jnp.where(first, zero, qp)], axis=0)
        s2 = lax.dot_general(q2, kp, (((1,), (1,)), ((), ())), preferred_element_type=F32) + bias2
        mn2 = jnp.maximum(mo2, jnp.max(s2, axis=1, keepdims=True))
        p2 = jnp.exp2(s2 - jnp.concatenate([mn2] * (nk // LANES), axis=1)).astype(BF16)
        pv2 = jnp.dot(p2, jnp.concatenate([vp, ones], axis=1), preferred_element_type=F32)
        pv = jnp.where(first, pv2[:nq, :LANES], pv2[nq:, :LANES])
        lc = jnp.where(first, pv2[:nq, LANES:], pv2[nq:, LANES:])
        alpha2 = jnp.exp2(mo2 - mn2)
        alpha = jnp.where(first, alpha2[:nq], alpha2[nq:])
        m_out += [mn2[:nq], mn2[nq:]]
        l_out.append(alpha * lo + lc)
        a_out.append(alpha * ao + pv)
    cat = lambda xs: jnp.concatenate(xs, axis=1)
    return cat(m_out), cat(l_out), cat(a_out)


def _attn_prompt_kernel(q_ref, kc_ref, kp_ref, vc_ref, vp_ref, b0_ref, b1_ref, b2_ref,
                        r0_ref, r1_ref, r2_ref, o_ref,
                        kcat, vcat, acc, m_s, l_s, ost):
    n = pl.program_id(0)
    g = pl.program_id(1)
    i0 = n * AT_I
    @pl.when(g < 2)
    def _():
        kcat[:, 0:AT_I, :] = kp_ref[...]
        kcat[:, AT_I:2 * AT_I, :] = kc_ref[...]
        vcat[:, 0:AT_I, :] = vp_ref[...]
        vcat[:, AT_I:2 * AT_I, :] = vc_ref[...]

    def key_bias(b_ref, r_ref, base):
        return b_ref[...] + jnp.where(base + r_ref[...] >= 0, 0.0, NEG)

    @pl.when(g == 0)
    def _():
        m_s[...] = jnp.full(m_s.shape, NEG, F32)
        l_s[...] = jnp.zeros(l_s.shape, F32)
        acc[...] = jnp.zeros(acc.shape, F32)

        def load(s):
            qo = pl.multiple_of(16 * s, 16)
            ko = pl.multiple_of(AT_I - 16 + 16 * s, 16)
            q16 = [q_ref[c, pl.ds(qo, 16), :] for c in range(N_CLASS)]
            k32 = [kcat[c, pl.ds(ko, 32), :] for c in range(N_CLASS)]
            v32 = [vcat[c, pl.ds(ko, 32), :] for c in range(N_CLASS)]
            cat = lambda f: jnp.concatenate([f(c) for c in range(N_CLASS)], axis=0)
            work = []
            for half in range(2):
                qh = pl.multiple_of(qo + 8 * half, 8)
                q = cat(lambda c: q16[c][8 * half:8 * half + 8])
                k = cat(lambda c: k32[c][8 + 8 * half:24 + 8 * half])
                v = cat(lambda c: v32[c][8 + 8 * half:24 + 8 * half])
                mo = cat(lambda c: m_s[c, pl.ds(qh, 8), :])
                lo = cat(lambda c: l_s[c, pl.ds(qh, 8), :])
                ao = cat(lambda c: acc[c, pl.ds(qh, 8), :])
                bias = key_bias(b0_ref, r0_ref, i0 + 16 * s + 8 * half)
                work.append((qh, (q, k, v, bias, mo, lo, ao)))
            return work

        def body(idx, carry):
            chunks = BLOCKS_PER_TRIP // 2
            work = [w for j in range(chunks) for w in load(chunks * idx + j)]
            done = [(qh, _attn_update(*args)) for qh, args in work]
            for qh, (mn, ln, an) in done:
                for c in range(N_CLASS):
                    m_s[c, pl.ds(qh, 8), :] = mn[8 * c:8 * c + 8]
                    l_s[c, pl.ds(qh, 8), :] = ln[8 * c:8 * c + 8]
                    acc[c, pl.ds(qh, 8), :] = an[8 * c:8 * c + 8]
            return carry

        lax.fori_loop(0, AT_I // 16 // (BLOCKS_PER_TRIP // 2), body, 0)

    @pl.when(g == 1)
    def _():
        def load(r4, s):
            qo = pl.multiple_of(32 * s, 32)
            ko = pl.multiple_of(AT_I - 32 + 32 * s, 32)
            cat = lambda f: jnp.concatenate([f(4 * r4 + a) for a in range(4)], axis=0)
            q = cat(lambda c: q_ref[c, pl.ds(qo, 32), :])
            k = cat(lambda c: kcat[c, pl.ds(ko, 64), :])
            v = cat(lambda c: vcat[c, pl.ds(ko, 64), :])
            mo = cat(lambda c: m_s[c, pl.ds(qo, 32), :])
            lo = cat(lambda c: l_s[c, pl.ds(qo, 32), :])
            ao = cat(lambda c: acc[c, pl.ds(qo, 32), :])
            bias = key_bias(b1_ref, r1_ref, i0 + 32 * s)
            return q, k, v, bias, mo, lo, ao

        def store(r4, s, mn, ln, an):
            qo = pl.multiple_of(32 * s, 32)
            for a in range(4):
                c = 4 * r4 + a
                m_s[c, pl.ds(qo, 32), :] = mn[32 * a:32 * a + 32]
                l_s[c, pl.ds(qo, 32), :] = ln[32 * a:32 * a + 32]
                acc[c, pl.ds(qo, 32), :] = an[32 * a:32 * a + 32]

        def body(idx, carry):
            per_class = 4 // BLOCKS_PER_TRIP
            where = [(idx // per_class, BLOCKS_PER_TRIP * (idx % per_class) + j) for j in range(BLOCKS_PER_TRIP)]
            work = [load(r4, s) for r4, s in where]
            done = [_attn_update(*args) for args in work]
            for (r4, s), res in zip(where, done):
                store(r4, s, *res)
            return carry

        lax.fori_loop(0, 16 // BLOCKS_PER_TRIP, body, 0)

    @pl.when(g == 2)
    def _():
        bias = key_bias(b2_ref, r2_ref, i0)

        def body(cc, carry):
            cs = [BLOCKS_PER_TRIP * cc + j for j in range(BLOCKS_PER_TRIP)]
            both = lambda prev, cur, c: jnp.concatenate([prev[c], cur[c]], axis=0)
            work = [(q_ref[c], both(kp_ref, kc_ref, c), both(vp_ref, vc_ref, c), bias, m_s[c], l_s[c], acc[c])
                    for c in cs]
            done = [_attn_update(*args) for args in work]
            for c, (mn, ln, an) in zip(cs, done):
                m_s[c] = mn
                l_s[c] = ln
                acc[c] = an
            return carry

        lax.fori_loop(0, N_CLASS // BLOCKS_PER_TRIP, body, 0)

        cls = _class_order()
        for c in range(N_CLASS):
            o = acc[c] / l_s[c]
            for jc in range(ATTN_WIDTH // LANES):
                ost[jc, pl.ds(cls[c], AT_I, stride=N_CLASS), :] = o[:, jc * LANES:(jc + 1) * LANES]
        o_ref[...] = jnp.concatenate([ost[jc] for jc in range(ATTN_WIDTH // LANES)], axis=1).astype(BF16)


def _attn_bias_tables():
    cls = np.asarray(_class_order())

    def bias(diff):
        return np.where((diff >= 0) & (diff <= N_KEYS_BACK), 0.0, NEG).astype(np.float32)

    c, r = np.divmod(np.arange(128), 8)
    c2, r2 = np.divmod(np.arange(256), 16)
    d0 = 16 * (r[:, None] - r2[None, :] + 8) + cls[c][:, None] - cls[c2][None, :]
    rel0 = (r2 - 8).astype(np.int32)[None, :]
    a, r = np.divmod(np.arange(128), 32)
    a2, r2 = np.divmod(np.arange(256), 64)
    d1 = 4 * (r[:, None] - r2[None, :] + 32) + a[:, None] - a2[None, :]
    rel1 = (r2 - 32).astype(np.int32)[None, :]
    r = np.arange(128)
    r2 = np.arange(256)
    d2 = r[:, None] - r2[None, :] + 128
    rel2 = (r2 - 128).astype(np.int32)[None, :]
    return (jnp.asarray(bias(d0)), jnp.asarray(bias(d1)), jnp.asarray(bias(d2)),
            jnp.asarray(rel0), jnp.asarray(rel1), jnp.asarray(rel2))


def _attn_prompt(q_r, k_r, v_r):
    n_i = q_r.shape[1]
    s = n_i * N_CLASS
    nsteps = n_i // AT_I
    b0, b1, b2, r0, r1, r2 = _attn_bias_tables()
    blk = (N_CLASS, AT_I, ATTN_WIDTH)
    cur = lambda n, g: (0, n, g)
    prev = lambda n, g: (0, jnp.maximum(n - 1, 0), g)
    return pl.pallas_call(
        _attn_prompt_kernel,
        grid=(nsteps, N_DIL),
        in_specs=[
            pl.BlockSpec(blk, cur), pl.BlockSpec(blk, cur), pl.BlockSpec(blk, prev),
            pl.BlockSpec(blk, cur), pl.BlockSpec(blk, prev),
            _const_spec(b0.shape), _const_spec(b1.shape), _const_spec(b2.shape),
            _const_spec(r0.shape), _const_spec(r1.shape), _const_spec(r2.shape),
        ],
        out_specs=pl.BlockSpec((AT_I * N_CLASS, ATTN_WIDTH), lambda n, g: (n, 0)),
        out_shape=jax.ShapeDtypeStruct((s, ATTN_WIDTH), BF16),
        scratch_shapes=[
            pltpu.VMEM((N_CLASS, 2 * AT_I, ATTN_WIDTH), BF16),
            pltpu.VMEM((N_CLASS, 2 * AT_I, ATTN_WIDTH), BF16),
            pltpu.VMEM((N_CLASS, AT_I, ATTN_WIDTH), F32),
            pltpu.VMEM((N_CLASS, AT_I, N_HEADS * LANES), F32),
            pltpu.VMEM((N_CLASS, AT_I, ATTN_WIDTH), F32),
            pltpu.VMEM((ATTN_WIDTH // LANES, AT_I * N_CLASS, LANES), F32),
        ],
        compiler_params=pltpu.CompilerParams(
            dimension_semantics=("arbitrary", "arbitrary"), vmem_limit_bytes=ATTN_VMEM_LIMIT_BYTES),
        name="attn_prompt",
    )(q_r, k_r, k_r, v_r, v_r, b0, b1, b2, r0, r1, r2)


def _proj_sample_kernel(n_b, n_t, x_ref, xb_ref, n1w_ref, w_ref, qw_ref, kw_ref, cos_ref, sina_ref, sinb_ref,
                        seg_ref, segt_ref, poolw_ref, pools_ref, state_ref,
                        ya_ref, gates_ref, q_ref, k_ref, v_ref, u_ref, kt_ref, vt_ref):
    xn = _rms(x_ref[...], n1w_ref[...]).astype(BF16)
    u = jnp.dot(xn, w_ref[:, C_U:C_U + POOL_WIDTH], preferred_element_type=F32)
    u_ref[...] = u

    def ext(j, lanes):
        if j < POOL_STATE:
            return state_ref[j, :, lanes]
        return u[(j - POOL_STATE) * n_b:(j - POOL_STATE + 1) * n_b, lanes]

    pooled = []
    for g, w in enumerate(POOL_WINDOWS):
        lanes = slice(g * POOL_GROUP_DIM, (g + 1) * POOL_GROUP_DIM)
        rows = []
        for i in range(n_t):
            s = ext(POOL_STATE + i, lanes)
            for m in range(1, w):
                s = s + ext(POOL_STATE + i - m, lanes)
            cnt = float(min(PAST_LEN + i + 1, w))
            rows.append(s / cnt - ext(POOL_STATE + i, lanes))
        pooled.append(jnp.concatenate(rows, axis=0))
    ya_ref[...] = _pool_mix_out(pooled, poolw_ref, pools_ref[...]).astype(BF16)

    gl = jnp.dot(xn, w_ref[:, C_G:C_G + N_BRANCH * D_MODEL], preferred_element_type=F32)
    gates_ref[...] = jax.nn.sigmoid(gl).astype(BF16)

    xnb = _rms(xb_ref[...], n1w_ref[...]).astype(BF16)
    cos, sina, sinb = cos_ref[...], sina_ref[...], sinb_ref[...]
    seg, segt = seg_ref[...], segt_ref[...]
    yq = jnp.dot(xnb, w_ref[:, C_Q:C_Q + QKV_WIDTH], preferred_element_type=F32)
    q_ref[...] = _head_norm_rope(yq, qw_ref[...], seg, segt, cos, sina, sinb)
    yk = jnp.dot(xnb, w_ref[:, C_K:C_K + QKV_WIDTH], preferred_element_type=F32)
    kr = _head_norm_rope(yk, kw_ref[...], seg, segt, cos, sina, sinb)
    k_ref[...] = kr
    kt_ref[...] = kr.T
    yv = jnp.dot(xnb, w_ref[:, C_V:C_V + QKV_WIDTH], preferred_element_type=F32)
    v_ref[...] = yv
    vt_ref[...] = yv.T


def _proj_sample(x, xb, n1w, w_bf, qw_row, kw_row, poolw_bf, pool_scale, state_t, n_b, n_t):
    rows = n_b * n_t
    pos = PAST_LEN + jnp.arange(rows, dtype=jnp.int32) % n_t
    cos, sina, sinb = _rope_tables(pos)
    seg, segt = _seg_mats()
    row = lambda a: a.reshape(1, -1)
    args = (x, xb, row(n1w), w_bf, row(qw_row), row(kw_row), cos, sina, sinb, seg, segt,
            poolw_bf, row(pool_scale), state_t)
    f32_qkv = jax.ShapeDtypeStruct((rows, QKV_WIDTH), F32)
    out_shape = [
        jax.ShapeDtypeStruct((rows, POOL_WIDTH), BF16),
        jax.ShapeDtypeStruct((rows, N_BRANCH * D_MODEL), BF16),
        f32_qkv, f32_qkv, f32_qkv,
        jax.ShapeDtypeStruct((rows, POOL_WIDTH), F32),
        jax.ShapeDtypeStruct((QKV_WIDTH, rows), F32),
        jax.ShapeDtypeStruct((QKV_WIDTH, rows), F32),
    ]
    return pl.pallas_call(
        functools.partial(_proj_sample_kernel, n_b, n_t),
        grid=(1,),
        in_specs=[_const_spec(a.shape) for a in args],
        out_specs=[_const_spec(o.shape) for o in out_shape],
        out_shape=out_shape,
        compiler_params=pltpu.CompilerParams(
            dimension_semantics=("arbitrary",), vmem_limit_bytes=VMEM_LIMIT_BYTES),
        name="proj_sample",
    )(*args)


HEADS_PER_STEP = 8


def _attn_roll_kernel(n_t, q_ref, kn_ref, vn_ref, newt_ref, c0_ref, c1_ref, c2_ref,
                      b0_ref, b1_ref, b2_ref, bn_ref, o_ref, o0_ref, o1_ref, o2_ref):
    caches = (c0_ref, c1_ref, c2_ref)
    outs = (o0_ref, o1_ref, o2_ref)
    biases = (b0_ref, b1_ref, b2_ref)
    nt_contract = (((1,), (1,)), ((), ()))
    col = pl.program_id(1) * n_t
    new_tile = pl.multiple_of((col // LANES) * LANES, LANES)
    new_shift = (LANES - n_t) - col % LANES
    for hh in range(HEADS_PER_STEP):
        scores, values = [], []
        for g in range(N_DIL):
            q = q_ref[hh, g].astype(BF16)
            kc = caches[g][0, hh].astype(BF16)
            scores.append(jnp.dot(q, kc, preferred_element_type=F32) + biases[g][...])
            kn = kn_ref[hh, g].astype(BF16)
            scores.append(lax.dot_general(q, kn, nt_contract, preferred_element_type=F32) + bn_ref[g])
            values.append(caches[g][1, hh].astype(BF16))
            values.append(vn_ref[hh, g].astype(BF16))
        m = functools.reduce(jnp.maximum, [jnp.max(s, axis=1, keepdims=True) for s in scores])
        l = jnp.zeros_like(m)
        o = jnp.zeros((n_t, HEAD_DIM), F32)
        for idx, (s, v) in enumerate(zip(scores, values)):
            p = jnp.exp(s - m)
            l = l + jnp.sum(p, axis=1, keepdims=True)
            pb = p.astype(BF16)
            if idx % 2 == 0:
                o = o + lax.dot_general(pb, v, nt_contract, preferred_element_type=F32)
            else:
                o = o + jnp.dot(pb, v, preferred_element_type=F32)
        o_ref[hh] = o / l

        for g in range(N_DIL):
            length = caches[g].shape[-1]
            lane = lax.broadcasted_iota(jnp.int32, (HEAD_DIM, LANES), 1)
            for kv in range(2):
                rolled = pltpu.roll(caches[g][kv, hh], length - n_t, 1)
                new = pltpu.roll(newt_ref[kv, g, hh, :, pl.ds(new_tile, LANES)], new_shift, 1)
                if length > LANES:
                    outs[g][kv, hh, :, 0:length - LANES] = rolled[:, 0:length - LANES]
                outs[g][kv, hh, :, length - LANES:length] = jnp.where(
                    lane >= LANES - n_t, new, rolled[:, length - LANES:length])


def _attn_roll_bias(n_t):
    i = np.arange(n_t)
    bias, bias_new = [], []
    for win, dil in DILATION_PAIRS:
        length = min(win, PAST_LEN)
        delta = length + i[:, None] - np.arange(length)[None, :]
        ok = (delta % dil == 0) & (delta // dil >= 1) & (delta // dil <= win // dil)
        bias.append(jnp.asarray(np.where(ok, 0.0, NEG).astype(np.float32)))
        dn = i[:, None] - i[None, :]
        okn = (dn >= 0) & (dn % dil == 0) & (dn // dil <= win // dil)
        bias_new.append(np.where(okn, 0.0, NEG).astype(np.float32))
    return bias, jnp.asarray(np.stack(bias_new))


def _attn_roll(q_h, kn_h, vn_h, new_t, caches_t):
    n_b, _, _, n_t, _ = q_h.shape
    bias, bias_new = _attn_roll_bias(n_t)
    hs = HEADS_PER_STEP
    qspec = pl.BlockSpec((None, hs, N_DIL, n_t, HEAD_DIM), lambda h, b: (b, h, 0, 0, 0))
    cspec = lambda c: pl.BlockSpec((None, 2, hs, HEAD_DIM, c.shape[-1]), lambda h, b: (b, 0, h, 0, 0))
    return pl.pallas_call(
        functools.partial(_attn_roll_kernel, n_t),
        grid=(N_HEADS // hs, n_b),
        in_specs=[qspec, qspec, qspec,
                  pl.BlockSpec((2, N_DIL, hs, HEAD_DIM, n_t * n_b), lambda h, b: (0, 0, h, 0, 0))]
                 + [cspec(c) for c in caches_t]
                 + [_const_spec(b.shape) for b in bias] + [_const_spec(bias_new.shape)],
        out_specs=[pl.BlockSpec((None, hs, n_t, HEAD_DIM), lambda h, b: (b, h, 0, 0))]
                  + [cspec(c) for c in caches_t],
        out_shape=[jax.ShapeDtypeStruct((n_b, N_HEADS, n_t, HEAD_DIM), F32)]
                  + [jax.ShapeDtypeStruct(c.shape, c.dtype) for c in caches_t],
        compiler_params=pltpu.CompilerParams(
            dimension_semantics=("arbitrary", "arbitrary"), vmem_limit_bytes=VMEM_LIMIT_BYTES),
        name="attn_roll",
    )(q_h, kn_h, vn_h, new_t, *caches_t, *bias, bias_new)


TM = 512
MOE_ROWS = 512
L_E0, L_E1, L_W0, L_W1, L_R0, L_R1 = 0, 1, 2, 3, 4, 5
L_GROUP = N_EXPERTS


def _lane_min_index(mask, lane):
    return jnp.min(jnp.where(mask, lane, float(LANES)), axis=1, keepdims=True)


def _zero_fill_step(n, xs_ref, zbuf, sem):
    slab = zbuf.shape[0]
    n_full, rem = divmod(xs_ref.shape[0], slab)
    full = lambda i: pltpu.make_async_copy(
        zbuf, xs_ref.at[pl.ds(pl.multiple_of(i * slab, MOE_ROWS), slab)], sem)
    tail = pltpu.make_async_copy(zbuf.at[pl.ds(0, max(rem, 1))],
                                 xs_ref.at[pl.ds(n_full * slab, max(rem, 1))], sem)

    @pl.when(n == 0)
    def _():
        zbuf[...] = jnp.zeros(zbuf.shape, F32)
        if rem:
            tail.start()

    @pl.when(n < n_full)
    def _():
        full(n).start()

    @pl.when(n == pl.num_programs(0) - 1)
    def _():
        for _ in range(n_full):
            full(0).wait()
        if rem:
            tail.wait()


def _merge_route_kernel(n_first, xa_ref, xb_ref, yaa_ref, yab_ref, yba_ref, ybb_ref, ga_ref, gb_ref,
                        wbp_ref, wba_ref, wo_ref, n2w_ref, wrh_ref, wrl_ref, br_ref, ltri_ref,
                        x1_ref, h_ref, route_ref, cnt_ref, xs_ref, zbuf, sem):
    n = pl.program_id(0)
    _zero_fill_step(n, xs_ref, zbuf, sem)
    first = n < n_first
    pick = lambda a, b: jnp.where(first, a[...], b[...])
    pa = jnp.dot(pick(yaa_ref, yab_ref), wbp_ref[...], preferred_element_type=F32)
    pb = jnp.dot(pick(yba_ref, ybb_ref), wba_ref[...], preferred_element_type=F32)
    gts = pick(ga_ref, gb_ref)
    merged = gts[:, :D_MODEL].astype(F32) * pa + gts[:, D_MODEL:].astype(F32) * pb
    x1 = pick(xa_ref, xb_ref) + jnp.dot(merged.astype(BF16), wo_ref[...], preferred_element_type=F32)
    x1_ref[...] = x1
    h = _rms(x1, n2w_ref[...])
    h_ref[...] = h

    h_hi = h.astype(BF16)
    h_lo = (h - h_hi.astype(F32)).astype(BF16)
    both = jnp.dot(h_hi, jnp.concatenate([wrh_ref[...], wrl_ref[...]], axis=1), preferred_element_type=F32)
    logits = (both[:, :LANES] + both[:, LANES:]
              + jnp.dot(h_lo, wrh_ref[...], preferred_element_type=F32)) + br_ref[...]
    rows = logits.shape[0]
    lane = lax.broadcasted_iota(jnp.int32, (rows, LANES), 1).astype(F32)
    is_group = (lane >= L_GROUP) & (lane < L_GROUP + N_EXPERT_GROUPS)
    gl = jnp.where(is_group, logits, NEG)
    gmax = jnp.max(gl, axis=1, keepdims=True)
    g_w = 1.0 / jnp.sum(jnp.exp(gl - gmax), axis=1, keepdims=True)
    g_sel = _lane_min_index(gl == gmax, lane) - L_GROUP
    in_group = (lane >= g_sel * EXPERTS_PER_GROUP) & (lane < (g_sel + 1.0) * EXPERTS_PER_GROUP)
    el = jnp.where(in_group, logits, NEG)
    v0 = jnp.max(el, axis=1, keepdims=True)
    e0 = _lane_min_index(el == v0, lane)
    el2 = jnp.where(lane == e0, NEG, el)
    v1 = jnp.max(el2, axis=1, keepdims=True)
    e1 = _lane_min_index(el2 == v1, lane)
    t = jnp.exp(v1 - v0)
    w0 = g_w / (1.0 + t)
    w1 = g_w * t / (1.0 + t)

    hot0 = lane == e0
    hot1 = lane == e1
    onehot = jnp.where(hot0 | hot1, 1.0, 0.0)

    @pl.when(n == 0)
    def _():
        cnt_ref[...] = jnp.zeros(cnt_ref.shape, F32)

    before = jnp.dot(ltri_ref[...], onehot.astype(BF16), preferred_element_type=F32) + cnt_ref[...]
    r0 = jnp.sum(jnp.where(hot0, before, 0.0), axis=1, keepdims=True)
    r1 = jnp.sum(jnp.where(hot1, before, 0.0), axis=1, keepdims=True)
    cnt_ref[...] = cnt_ref[...] + jnp.sum(onehot, axis=0, keepdims=True)

    rec = jnp.zeros((rows, LANES), F32)
    for ln, val in ((L_E0, e0), (L_E1, e1), (L_W0, w0), (L_W1, w1), (L_R0, r0), (L_R1, r1)):
        rec = jnp.where(lane == ln, val, rec)
    route_ref[...] = rec


def _router_weights(w_rg, b_rg, w_re, b_re):
    w = jnp.zeros((D_MODEL, LANES), F32)
    w = w.at[:, :N_EXPERTS].set(w_re).at[:, L_GROUP:L_GROUP + N_EXPERT_GROUPS].set(w_rg)
    b = jnp.zeros((1, LANES), F32)
    b = b.at[0, :N_EXPERTS].set(b_re).at[0, L_GROUP:L_GROUP + N_EXPERT_GROUPS].set(b_rg)
    w_hi = w.astype(BF16)
    w_lo = (w - w_hi.astype(F32)).astype(BF16)
    return w_hi, w_lo, b


def _merge_route(xa, xb, yaa, yab, yba, ybb, ga, gb, wbp, wba, wo, n2w, wr_hi, wr_lo, br):
    n_first = xa.shape[0] // TM
    t_all = xa.shape[0] + xb.shape[0]
    n_rows = _moe_blocks(t_all) * MOE_ROWS
    slab_rows = -(-n_rows // MOE_ROWS // (t_all // TM)) * MOE_ROWS
    assert n_rows // slab_rows <= t_all // TM
    ltri = jnp.asarray(np.tril(np.ones((TM, TM), np.float32), -1), BF16)
    ta = lambda w: pl.BlockSpec((TM, w), lambda n: (jnp.minimum(n, n_first - 1), 0))
    tb = lambda w: pl.BlockSpec((TM, w), lambda n: (jnp.maximum(n - n_first, 0), 0))
    tile = lambda w: pl.BlockSpec((TM, w), lambda n: (n, 0))
    return pl.pallas_call(
        functools.partial(_merge_route_kernel, n_first),
        grid=(t_all // TM,),
        in_specs=[
            ta(D_MODEL), tb(D_MODEL), ta(POOL_WIDTH), tb(POOL_WIDTH), ta(ATTN_WIDTH), tb(ATTN_WIDTH),
            ta(N_BRANCH * D_MODEL), tb(N_BRANCH * D_MODEL),
            _const_spec(wbp.shape), _const_spec(wba.shape), _const_spec(wo.shape),
            _const_spec((1, D_MODEL)),
            _const_spec(wr_hi.shape), _const_spec(wr_lo.shape), _const_spec(br.shape),
            _const_spec(ltri.shape),
        ],
        out_specs=[tile(D_MODEL), tile(D_MODEL), tile(LANES), _const_spec((1, LANES)),
                   pl.BlockSpec(memory_space=pl.ANY)],
        out_shape=[
            jax.ShapeDtypeStruct((t_all, D_MODEL), F32),
            jax.ShapeDtypeStruct((t_all, D_MODEL), F32),
            jax.ShapeDtypeStruct((t_all, LANES), F32),
            jax.ShapeDtypeStruct((1, LANES), F32),
            jax.ShapeDtypeStruct((n_rows, D_MODEL), F32),
        ],
        scratch_shapes=[pltpu.VMEM((slab_rows, D_MODEL), F32), pltpu.SemaphoreType.DMA],
        compiler_params=pltpu.CompilerParams(
            dimension_semantics=("arbitrary",), vmem_limit_bytes=VMEM_LIMIT_BYTES),
        name="merge_route",
    )(xa, xb, yaa, yab, yba, ybb, ga, gb, wbp, wba, wo, n2w.reshape(1, -1), wr_hi, wr_lo, br, ltri)


def _moe_blocks(n_tokens):
    return -(-(2 * n_tokens + N_EXPERTS * (MOE_ROWS - 1)) // MOE_ROWS)


def _plan_kernel(nb_pad, cnt_ref, route_ref, upper_ref, dest_ref, blk_ref):
    n = pl.program_id(0)
    cnt = cnt_ref[...]
    nblk = jnp.floor((cnt + (MOE_ROWS - 1)) * (1.0 / MOE_ROWS))
    nb8 = jnp.broadcast_to(nblk, (8, LANES)).astype(BF16)
    bstart = jnp.dot(nb8, upper_ref[...], preferred_element_type=F32)[0:1]
    bend = bstart + nblk
    pstart = bstart * MOE_ROWS

    rec = route_ref[...]
    rows = rec.shape[0]
    lane = lax.broadcasted_iota(jnp.int32, (rows, LANES), 1).astype(F32)
    col = lambda ln: jnp.sum(jnp.where(lane == ln, rec, 0.0), axis=1, keepdims=True)
    look = lambda e: jnp.sum(jnp.where(lane == e, pstart, 0.0), axis=1, keepdims=True)
    d0 = look(col(L_E0)) + col(L_R0)
    d1 = look(col(L_E1)) + col(L_R1)
    dest_ref[...] = jnp.where(lane == 0, d0, jnp.where(lane == 1, d1, 0.0)).astype(jnp.int32)

    @pl.when(n == 0)
    def _():
        b = lax.broadcasted_iota(jnp.int32, (nb_pad, LANES), 0).astype(F32)
        lane_b = lax.broadcasted_iota(jnp.int32, (nb_pad, LANES), 1)
        done = jnp.where((bend <= b) & (lane_b < N_EXPERTS), 1.0, 0.0)
        e_of_b = jnp.minimum(jnp.sum(done, axis=1, keepdims=True), float(N_EXPERTS - 1))
        total = jnp.max(jnp.where(lane_b < N_EXPERTS, bend, 0.0), axis=1, keepdims=True)
        blk_ref[...] = jnp.where(lane_b == 0, e_of_b, jnp.where(lane_b == 1, total, 0.0)).astype(jnp.int32)


def _plan(cnt, route, n_rows):
    t = route.shape[0]
    nb = n_rows // MOE_ROWS
    nb_pad = -(-nb // 8) * 8
    upper = jnp.asarray(np.triu(np.ones((LANES, LANES), np.float32), 1), BF16)
    return pl.pallas_call(
        functools.partial(_plan_kernel, nb_pad),
        grid=(t // TM,),
        in_specs=[_const_spec((1, LANES)), pl.BlockSpec((TM, LANES), lambda n: (n, 0)),
                  _const_spec((LANES, LANES))],
        out_specs=[pl.BlockSpec((TM, LANES), lambda n: (n, 0)), _const_spec((nb_pad, LANES))],
        out_shape=[jax.ShapeDtypeStruct((t, LANES), jnp.int32),
                   jax.ShapeDtypeStruct((nb_pad, LANES), jnp.int32)],
        compiler_params=pltpu.CompilerParams(
            dimension_semantics=("arbitrary",), vmem_limit_bytes=VMEM_LIMIT_BYTES),
        name="moe_plan",
    )(cnt, route, upper)


def _dispatch_kernel(dest_ref, h_ref, xs_in, xs_out, sem):
    del xs_in
    rows = h_ref.shape[0]

    def issue(i, carry):
        base = pl.multiple_of(i * 8, 8)
        for j in range(8):
            for k in range(2):
                pltpu.make_async_copy(h_ref.at[pl.ds(base + j, 1)],
                                      xs_out.at[pl.ds(dest_ref[k, base + j], 1)], sem).start()
        return carry

    lax.fori_loop(0, rows // 8, issue, 0)
    for k in range(2):
        pltpu.make_async_copy(h_ref, xs_out.at[pl.ds(0, rows)], sem).wait()


def _dispatch(dest_t, h, xs):
    t = h.shape[0]
    return pl.pallas_call(
        _dispatch_kernel,
        grid=(t // TM,),
        in_specs=[pl.BlockSpec((None, 8, TM), lambda n: (n, 0, 0), memory_space=pltpu.SMEM),
                  pl.BlockSpec((TM, D_MODEL), lambda n: (n, 0)),
                  pl.BlockSpec(memory_space=pl.ANY)],
        out_specs=pl.BlockSpec(memory_space=pl.ANY),
        out_shape=jax.ShapeDtypeStruct(xs.shape, xs.dtype),
        input_output_aliases={2: 0},
        scratch_shapes=[pltpu.SemaphoreType.DMA],
        compiler_params=pltpu.CompilerParams(
            dimension_semantics=("arbitrary",), vmem_limit_bytes=VMEM_LIMIT_BYTES),
        name="moe_dispatch",
    )(dest_t, h, xs)


def _expert_kernel(blk_ref, x_ref, wg_ref, wu_ref, wd_ref, y_ref, wg_bf, wu_bf, wd_bf):
    b = pl.program_id(0)
    used = b < blk_ref[1, 0]

    @pl.when(used)
    def _():
        prev = blk_ref[0, jnp.maximum(b - 1, 0)]

        @pl.when((b == 0) | (blk_ref[0, b] != prev))
        def _():
            wg_bf[...] = wg_ref[...].astype(BF16)
            wu_bf[...] = wu_ref[...].astype(BF16)
            wd_bf[...] = wd_ref[...].astype(BF16)

        x = x_ref[...].astype(BF16)
        gate = jnp.dot(x, wg_bf[...], preferred_element_type=F32)
        up = jnp.dot(x, wu_bf[...], preferred_element_type=F32)
        mid = (jax.nn.silu(gate) * up).astype(BF16)
        y_ref[...] = jnp.dot(mid, wd_bf[...], preferred_element_type=F32)

    @pl.when(jnp.logical_not(used))
    def _():
        y_ref[...] = jnp.zeros(y_ref.shape, F32)


def _experts(blk_t, xs, w_g, w_u, w_d):
    n_rows = xs.shape[0]
    nb = n_rows // MOE_ROWS
    last = lambda b, blk: jnp.minimum(b, blk[1, 0] - 1)
    grid_spec = pltpu.PrefetchScalarGridSpec(
        num_scalar_prefetch=1,
        grid=(nb,),
        in_specs=[
            pl.BlockSpec((MOE_ROWS, D_MODEL), lambda b, blk: (last(b, blk), 0)),
            pl.BlockSpec((None, D_MODEL, D_EXPERT), lambda b, blk: (blk[0, last(b, blk)], 0, 0)),
            pl.BlockSpec((None, D_MODEL, D_EXPERT), lambda b, blk: (blk[0, last(b, blk)], 0, 0)),
            pl.BlockSpec((None, D_EXPERT, D_MODEL), lambda b, blk: (blk[0, last(b, blk)], 0, 0)),
        ],
        out_specs=pl.BlockSpec((MOE_ROWS, D_MODEL), lambda b, blk: (b, 0)),
        scratch_shapes=[pltpu.VMEM((D_MODEL, D_EXPERT), BF16), pltpu.VMEM((D_MODEL, D_EXPERT), BF16),
                        pltpu.VMEM((D_EXPERT, D_MODEL), BF16)],
    )
    return pl.pallas_call(
        _expert_kernel,
        grid_spec=grid_spec,
        out_shape=jax.ShapeDtypeStruct((n_rows, D_MODEL), F32),
        compiler_params=pltpu.CompilerParams(
            dimension_semantics=("arbitrary",), vmem_limit_bytes=VMEM_LIMIT_BYTES),
        name="moe_experts",
    )(blk_t, xs, w_g, w_u, w_d)


def _combine_kernel(n_first, dest_ref, x1_ref, route_ref, ys_ref, oa_ref, ob_ref, gbuf, sem):
    n = pl.program_id(0)
    rows = x1_ref.shape[0]

    def issue(i, carry):
        base = pl.multiple_of(i * 8, 8)
        for j in range(8):
            for k in range(2):
                pltpu.make_async_copy(ys_ref.at[pl.ds(dest_ref[k, base + j], 1)],
                                      gbuf.at[k, pl.ds(base + j, 1)], sem).start()
        return carry

    lax.fori_loop(0, rows // 8, issue, 0)
    for k in range(2):
        pltpu.make_async_copy(ys_ref.at[pl.ds(0, rows)], gbuf.at[k], sem).wait()
    rec = route_ref[...]
    lane = lax.broadcasted_iota(jnp.int32, rec.shape, 1)
    w0 = jnp.sum(jnp.where(lane == L_W0, rec, 0.0), axis=1, keepdims=True)
    w1 = jnp.sum(jnp.where(lane == L_W1, rec, 0.0), axis=1, keepdims=True)
    res = x1_ref[...] + (gbuf[0] * w0 + gbuf[1] * w1)

    @pl.when(n < n_first)
    def _():
        oa_ref[...] = res

    @pl.when(n >= n_first)
    def _():
        ob_ref[...] = res


def _combine(dest_t, x1, route, ys, t_first):
    t = x1.shape[0]
    n_first = t_first // TM
    return pl.pallas_call(
        functools.partial(_combine_kernel, n_first),
        grid=(t // TM,),
        in_specs=[pl.BlockSpec((None, 8, TM), lambda n: (n, 0, 0), memory_space=pltpu.SMEM),
                  pl.BlockSpec((TM, D_MODEL), lambda n: (n, 0)),
                  pl.BlockSpec((TM, LANES), lambda n: (n, 0)),
                  pl.BlockSpec(memory_space=pl.ANY)],
        out_specs=[pl.BlockSpec((TM, D_MODEL), lambda n: (jnp.minimum(n, n_first - 1), 0)),
                   pl.BlockSpec((TM, D_MODEL), lambda n: (jnp.maximum(n - n_first, 0), 0))],
        out_shape=[jax.ShapeDtypeStruct((t_first, D_MODEL), F32),
                   jax.ShapeDtypeStruct((t - t_first, D_MODEL), F32)],
        scratch_shapes=[pltpu.VMEM((2, TM, D_MODEL), F32), pltpu.SemaphoreType.DMA],
        compiler_params=pltpu.CompilerParams(
            dimension_semantics=("arbitrary",), vmem_limit_bytes=VMEM_LIMIT_BYTES),
        name="moe_combine",
    )(dest_t, x1, route, ys)


def kernel(x_prompt, x_sample, cache_kv_w128, cache_kv_w512, cache_kv_w2048, state_pool, norm1_w, w_in, q_norm_w, k_norm_w, pool_w, pool_scale, w_branch_pool, w_branch_attn, w_out, norm2_w, w_router_group, b_router_group, w_router_expert, b_router_expert, w_expert_gate, w_expert_up, w_expert_down):
    assert x_prompt.shape[0] == 1 and norm1_w.shape[0] == 1
    layer = 0
    s_p = x_prompt.shape[1]
    n_b, n_t, _ = x_sample.shape
    t_s = n_b * n_t
    t_all = s_p + t_s
    caches = (cache_kv_w128[layer], cache_kv_w512[layer], cache_kv_w2048[layer])

    w_bf = w_in[layer].astype(BF16)
    qw_row = jnp.tile(q_norm_w[layer][:, None, :], (1, N_HEADS, 1)).reshape(-1) * (HEAD_DIM ** -0.5)
    kw_row = jnp.tile(k_norm_w[layer][:, None, :], (1, N_HEADS, 1)).reshape(-1)
    poolw_bf = pool_w[layer].astype(BF16)
    wbp, wba, wo = (w.astype(BF16) for w in (w_branch_pool[layer], w_branch_attn[layer], w_out[layer]))
    wr_hi, wr_lo, br = _router_weights(w_router_group[layer], b_router_group[layer],
                                       w_router_expert[layer], b_router_expert[layer])

    ya_p, gates_p, q_r, k_r, v_r, kvt, ut = _proj_prompt(
        x_prompt[0], norm1_w[layer], w_bf, qw_row * LOG2_E, kw_row, poolw_bf, pool_scale[layer])
    yb_p = _attn_prompt(q_r, k_r, v_r)

    xs_t = jnp.transpose(x_sample, (1, 0, 2)).reshape(t_s, D_MODEL)
    state_t = jnp.transpose(state_pool[layer], (1, 0, 2))
    ya_s, gates_s, q_s, k_s, v_s, u_s, kt_s, vt_s = _proj_sample(
        xs_t, x_sample.reshape(t_s, D_MODEL), norm1_w[layer], w_bf, qw_row, kw_row, poolw_bf,
        pool_scale[layer], state_t, n_b, n_t)
    split = lambda a: a.reshape(n_b, n_t, N_DIL, N_HEADS, HEAD_DIM)
    per_head = lambda a: jnp.transpose(split(a), (0, 3, 2, 1, 4))
    new_t = jnp.stack([kt_s, vt_s]).reshape(2, N_DIL, N_HEADS, HEAD_DIM, t_s)
    caches_t = [jnp.transpose(c, (0, 2, 3, 4, 1)) for c in caches]
    o_s, *kv_t = _attn_roll(per_head(q_s), per_head(k_s), per_head(v_s), new_t, caches_t)
    yb_s = jnp.transpose(o_s, (2, 0, 1, 3)).reshape(t_s, ATTN_WIDTH).astype(BF16)
    kv_s = [jnp.transpose(c, (0, 4, 1, 2, 3)) for c in kv_t]

    x1, h, route, cnt, xs = _merge_route(x_prompt[0], xs_t, ya_p, ya_s, yb_p, yb_s, gates_p, gates_s,
                                     wbp, wba, wo, norm2_w[layer], wr_hi, wr_lo, br)

    n_rows = _moe_blocks(t_all) * MOE_ROWS
    dest, blk = _plan(cnt, route, n_rows)
    dest_t = jnp.transpose(dest[:, :8].reshape(t_all // TM, TM, 8), (0, 2, 1))
    blk_t = jnp.transpose(blk[:, :2])
    xs = _dispatch(dest_t, h, xs)
    ys = _experts(blk_t, xs, w_expert_gate[layer], w_expert_up[layer], w_expert_down[layer])
    y_p, y_s = _combine(dest_t, x1, route, ys, s_p)

    y_prompt = y_p[None]
    y_sample = jnp.transpose(y_s.reshape(n_t, n_b, D_MODEL), (1, 0, 2))
    pool_prompt = ut[1:][None, None]
    u_new = jnp.transpose(u_s.reshape(n_t, n_b, POOL_WIDTH), (1, 0, 2))
    pool_sample = jnp.concatenate([state_pool[layer][:, n_t:], u_new], axis=1)[None]
    kv_t = kvt.reshape(2, N_DIL, N_HEADS, HEAD_DIM, kvt.shape[1])
    outs = [y_prompt, y_sample, pool_prompt, pool_sample]
    for g, (win, _) in enumerate(DILATION_PAIRS):
        keep = min(win, s_p)
        kv_g = kv_t[:, g, :, :, kv_t.shape[-1] - keep:]
        outs.append(jnp.transpose(kv_g, (3, 0, 1, 2))[None, None])
        outs.append(kv_s[g][None])
    return tuple(outs)
```

```python
import functools

import numpy as np
import jax
import jax.numpy as jnp
from jax import lax
from jax.experimental import pallas as pl
from jax.experimental.pallas import tpu as pltpu

F32 = jnp.float32
BF16 = jnp.bfloat16

D_MODEL = 1024
PAST_LEN = 8192
POOL_WIDTH = 512
POOL_WINDOWS = (2, 4, 8, 16)
POOL_GROUP_DIM = 128
POOL_STATE = 15
HEAD_DIM = 64
N_HEADS = 8
DILATION_PAIRS = ((128, 1), (512, 4), (2048, 16))
N_DIL = 3
ATTN_WIDTH = 512
QKV_WIDTH = 1536
ROT_DIM = 16
ROPE_THETA = 500000.0
N_BRANCH = 2
IN_COLS = POOL_WIDTH + 3 * QKV_WIDTH + N_BRANCH * D_MODEL
N_EXPERT_GROUPS = 4
EXPERTS_PER_GROUP = 8
N_EXPERTS = 32
D_EXPERT = 512
EPS = 1e-6

LANES = 128
VMEM_LIMIT_BYTES = 56 * 1024 * 1024
ATTN_VMEM_LIMIT_BYTES = 60 * 1024 * 1024

C_U = 0
C_Q = POOL_WIDTH
C_K = C_Q + QKV_WIDTH
C_V = C_K + QKV_WIDTH
C_G = C_V + QKV_WIDTH

N_CLASS = 16
TI = 256
ROWS_PER_CLASS = TI // N_CLASS
NEG = -1e30
LOG2_E = 1.4426950408889634


def _class_order():
    return [(c // 4) + 4 * (c % 4) for c in range(N_CLASS)]


def _const_spec(shape, single_buffer=False):
    nd = len(shape)
    kw = {}
    if single_buffer:
        kw["pipeline_mode"] = pl.Buffered(1)
    return pl.BlockSpec(shape, lambda *_: (0,) * nd, **kw)


def _rms(x, w):
    ms = jnp.mean(x * x, axis=-1, keepdims=True)
    return x * lax.rsqrt(ms + EPS) * w


def _head_norm_rope(y, wrow, seg, segt, cos, sina, sinb):
    sq = (y * y).astype(BF16)
    ssum = jnp.dot(sq, seg, preferred_element_type=F32)
    r = lax.rsqrt(ssum * (1.0 / HEAD_DIM) + EPS)
    r_hi = r.astype(BF16)
    r_lo = (r - r_hi.astype(F32)).astype(BF16)
    rexp = jnp.dot(jnp.concatenate([r_hi, r_lo], axis=1), segt, preferred_element_type=F32)
    yn = y * rexp * wrow
    outs = []
    for j in range(QKV_WIDTH // LANES):
        c = yn[:, j * LANES:(j + 1) * LANES]
        outs.append(c * cos + pltpu.roll(c, LANES - ROT_DIM // 2, 1) * sina
                    + pltpu.roll(c, ROT_DIM // 2, 1) * sinb)
    return jnp.concatenate(outs, axis=1)


def _pool_mix_out(pooled, poolw_ref, pool_scale):
    outs = []
    for g in range(len(POOL_WINDOWS)):
        outs.append(jnp.dot(pooled[g].astype(BF16), poolw_ref[g], preferred_element_type=F32))
    return jnp.concatenate(outs, axis=1) * pool_scale


def _split3(x):
    hi = x.astype(BF16)
    r1 = x - hi.astype(F32)
    mid = r1.astype(BF16)
    lo = (r1 - mid.astype(F32)).astype(BF16)
    return hi, mid, lo


def _proj_prompt_kernel(tail_tiles, x_ref, n1w_ref, w_ref, qw_ref, kw_ref, cosr_ref, sinr_ref, cosn_ref,
                        sinn_ref, rotm_ref, perm_ref, permt_ref, seg_ref, segt_ref, poolw_ref, pools_ref,
                        ya_ref, gates_ref, q_ref, k_ref, v_ref, kvt_ref, ut_ref,
                        uext_ref):
    n = pl.program_id(0)
    xn = _rms(x_ref[...], n1w_ref[...]).astype(BF16)

    u = jnp.dot(xn, w_ref[:, C_U:C_U + POOL_WIDTH], preferred_element_type=F32)

    @pl.when(n == 0)
    def _():
        uext_ref[0:16, :] = jnp.zeros((16, POOL_WIDTH), F32)

    uext_ref[16:16 + TI, :] = u
    pos = n * TI + lax.broadcasted_iota(jnp.int32, (TI, 1), 0)
    pooled = []
    for g, w in enumerate(POOL_WINDOWS):
        lanes = slice(g * POOL_GROUP_DIM, (g + 1) * POOL_GROUP_DIM)
        s = uext_ref[16:16 + TI, lanes]
        for m in range(1, w):
            s = s + uext_ref[16 - m:16 - m + TI, lanes]
        cnt = jnp.minimum(pos + 1, w).astype(F32)
        pooled.append(s / cnt - u[:, lanes])
    ya_ref[...] = _pool_mix_out(pooled, poolw_ref, pools_ref[...]).astype(BF16)
    ut_ref[...] = u[TI - 16:, :]
    uext_ref[0:16, :] = uext_ref[TI:TI + 16, :]

    gl = jnp.dot(xn, w_ref[:, C_G:C_G + N_BRANCH * D_MODEL], preferred_element_type=F32)
    gates_ref[...] = jax.nn.sigmoid(gl).astype(BF16)

    xp = jnp.dot(perm_ref[...], xn, preferred_element_type=F32).astype(BF16)
    cn, sn = cosn_ref[pl.ds(n, 1), :], sinn_ref[pl.ds(n, 1), :]
    cr, sr = cosr_ref[...], sinr_ref[...]
    cos = cn * cr - sn * sr
    sin = sn * cr + cn * sr
    sina, sinb = sin * rotm_ref[0:1, :], sin * rotm_ref[1:2, :]
    seg, segt = seg_ref[...], segt_ref[...]
    shp = (N_CLASS, ROWS_PER_CLASS, QKV_WIDTH)
    yq = jnp.dot(xp, w_ref[:, C_Q:C_Q + QKV_WIDTH], preferred_element_type=F32)
    q_ref[...] = _head_norm_rope(yq, qw_ref[...], seg, segt, cos, sina, sinb).astype(BF16).reshape(shp)
    yk = jnp.dot(xp, w_ref[:, C_K:C_K + QKV_WIDTH], preferred_element_type=F32)
    kr = _head_norm_rope(yk, kw_ref[...], seg, segt, cos, sina, sinb)
    k_ref[...] = kr.astype(BF16).reshape(shp)
    yv = jnp.dot(xp, w_ref[:, C_V:C_V + QKV_WIDTH], preferred_element_type=F32)
    v_ref[...] = yv.astype(BF16).reshape(shp)

    @pl.when(n >= pl.num_programs(0) - tail_tiles)
    def _():
        def natural_t(val):
            nat = sum(jnp.dot(permt_ref[...], part, preferred_element_type=F32) for part in _split3(val))
            return nat.T
        kvt_ref[0:QKV_WIDTH, :] = natural_t(kr)
        kvt_ref[QKV_WIDTH:2 * QKV_WIDTH, :] = natural_t(yv)


def _rope_tables(pos):
    half = ROT_DIM // 2
    inv = ROPE_THETA ** (-(jnp.arange(half, dtype=F32) * 2.0 / ROT_DIM))
    ang = pos.astype(F32)[:, None] * inv[None, :]
    c, s = jnp.cos(ang), jnp.sin(ang)
    n = pos.shape[0]
    ones = jnp.ones((n, HEAD_DIM - ROT_DIM), F32)
    zeros = jnp.zeros((n, HEAD_DIM - ROT_DIM), F32)
    z8 = jnp.zeros((n, half), F32)
    cos64 = jnp.concatenate([c, c, ones], axis=1)
    sina64 = jnp.concatenate([-s, z8, zeros], axis=1)
    sinb64 = jnp.concatenate([z8, s, zeros], axis=1)
    rep = lambda t: jnp.concatenate([t, t], axis=1)
    return rep(cos64), rep(sina64), rep(sinb64)


def _rope_lane_freq():
    half = ROT_DIM // 2
    inv = ROPE_THETA ** (-(np.arange(half, dtype=np.float64) * 2.0 / ROT_DIM))
    d = np.arange(LANES) % HEAD_DIM
    freq = np.where(d < ROT_DIM, inv[d % half], 0.0)
    mask = np.zeros((8, LANES))
    mask[0] = np.where(d < half, -1.0, 0.0)
    mask[1] = np.where((d >= half) & (d < ROT_DIM), 1.0, 0.0)
    return freq, mask


def _seg_mats():
    h = np.arange(QKV_WIDTH) // HEAD_DIM
    seg = (h[:, None] == np.arange(LANES)[None, :]).astype(np.float32)
    return jnp.asarray(seg, BF16), jnp.asarray(np.concatenate([seg.T, seg.T], axis=0), BF16)


def _perm_mat():
    cls = _class_order()
    p = np.zeros((TI, TI), np.float32)
    for c in range(N_CLASS):
        for j in range(ROWS_PER_CLASS):
            p[c * ROWS_PER_CLASS + j, N_CLASS * j + cls[c]] = 1.0
    return p


def _proj_prompt(x, n1w, w_bf, qw_row, kw_row, poolw_bf, pool_scale):
    s = x.shape[0]
    nt = s // TI
    cls = np.asarray(_class_order())
    perm = _perm_mat()
    j = np.arange(ROWS_PER_CLASS)
    inner = (N_CLASS * j[None, :] + cls[:, None]).reshape(-1).astype(np.float64)
    freq, rot_mask = _rope_lane_freq()
    ang_r = inner[:, None] * freq[None, :]
    ang_n = (np.arange(nt, dtype=np.float64) * TI)[:, None] * freq[None, :]
    f32c = lambda a: jnp.asarray(a.astype(np.float32))
    cosr, sinr, cosn, sinn = f32c(np.cos(ang_r)), f32c(np.sin(ang_r)), f32c(np.cos(ang_n)), f32c(np.sin(ang_n))
    rotm = f32c(rot_mask)
    seg, segt = _seg_mats()
    n_i = s // N_CLASS
    tail_tokens = min(DILATION_PAIRS[-1][0], s)
    tail_tiles = tail_tokens // TI
    row = lambda a: a.reshape(1, -1)
    tile = lambda n: (n, 0)
    rblk = lambda n: (0, n, 0)
    tblk = lambda n: (0, jnp.maximum(n - (nt - tail_tiles), 0))
    outs = pl.pallas_call(
        functools.partial(_proj_prompt_kernel, tail_tiles),
        grid=(nt,),
        in_specs=[
            pl.BlockSpec((TI, D_MODEL), tile),
            _const_spec((1, D_MODEL)),
            _const_spec((D_MODEL, IN_COLS), single_buffer=True),
            _const_spec((1, QKV_WIDTH)),
            _const_spec((1, QKV_WIDTH)),
            _const_spec((TI, LANES)), _const_spec((TI, LANES)),
            _const_spec((nt, LANES)), _const_spec((nt, LANES)), _const_spec((8, LANES)),
            _const_spec((TI, TI)), _const_spec((TI, TI)),
            _const_spec((QKV_WIDTH, LANES)),
            _const_spec((2 * LANES, QKV_WIDTH)),
            _const_spec((len(POOL_WINDOWS), POOL_GROUP_DIM, POOL_GROUP_DIM)),
            _const_spec((1, POOL_WIDTH)),
        ],
        out_specs=[
            pl.BlockSpec((TI, POOL_WIDTH), tile),
            pl.BlockSpec((TI, N_BRANCH * D_MODEL), tile),
            pl.BlockSpec((N_CLASS, ROWS_PER_CLASS, QKV_WIDTH), rblk),
            pl.BlockSpec((N_CLASS, ROWS_PER_CLASS, QKV_WIDTH), rblk),
            pl.BlockSpec((N_CLASS, ROWS_PER_CLASS, QKV_WIDTH), rblk),
            pl.BlockSpec((2 * QKV_WIDTH, TI), tblk),
            _const_spec((16, POOL_WIDTH)),
        ],
        out_shape=[
            jax.ShapeDtypeStruct((s, POOL_WIDTH), BF16),
            jax.ShapeDtypeStruct((s, N_BRANCH * D_MODEL), BF16),
            jax.ShapeDtypeStruct((N_CLASS, n_i, QKV_WIDTH), BF16),
            jax.ShapeDtypeStruct((N_CLASS, n_i, QKV_WIDTH), BF16),
            jax.ShapeDtypeStruct((N_CLASS, n_i, QKV_WIDTH), BF16),
            jax.ShapeDtypeStruct((2 * QKV_WIDTH, tail_tokens), F32),
            jax.ShapeDtypeStruct((16, POOL_WIDTH), F32),
        ],
        scratch_shapes=[pltpu.VMEM((16 + TI, POOL_WIDTH), F32)],
        compiler_params=pltpu.CompilerParams(
            dimension_semantics=("arbitrary",), vmem_limit_bytes=VMEM_LIMIT_BYTES),
        name="proj_prompt",
    )(x, row(n1w), w_bf, row(qw_row), row(kw_row), cosr, sinr, cosn, sinn, rotm,
      jnp.asarray(perm, BF16), jnp.asarray(perm.T, BF16), seg, segt, poolw_bf, row(pool_scale))
    return outs


AT_I = 128
N_KEYS_BACK = 128
BLOCKS_PER_TRIP = 4


def _attn_update(q, k, v, bias, m_old, l_old, acc_old):
    nq, nk = q.shape[0], k.shape[0]
    assert nk % LANES == 0
    lane = lax.broadcasted_iota(jnp.int32, (nq, LANES), 1)
    first = lane < HEAD_DIM
    bias2 = jnp.concatenate([bias, bias], axis=0)
    ones = jnp.ones((nk, LANES), BF16)
    m_out, l_out, a_out = [], [], []
    for hp in range(N_HEADS // 2):
        sl = slice(hp * LANES, (hp + 1) * LANES)
        qp, kp, vp = q[:, sl], k[:, sl], v[:, sl]
        lo, ao = l_old[:, sl], acc_old[:, sl]
        mo2 = jnp.concatenate([m_old[:, (2 * hp) * LANES:(2 * hp + 1) * LANES],
                               m_old[:, (2 * hp + 1) * LANES:(2 * hp + 2) * LANES]], axis=0)
        zero = jnp.zeros_like(qp)
        q2 = jnp.concatenate([jnp.where(first, qp, zero), jnp.where(first, zero, qp)], axis=0)
        s2 = lax.dot_general(q2, kp, (((1,), (1,)), ((), ())), preferred_element_type=F32) + bias2
        mn2 = jnp.maximum(mo2, jnp.max(s2, axis=1, keepdims=True))
        p2 = jnp.exp2(s2 - jnp.concatenate([mn2] * (nk // LANES), axis=1)).astype(BF16)
        pv2 = jnp.dot(p2, jnp.concatenate([vp, ones], axis=1), preferred_element_type=F32)
        pv = jnp.where(first, pv2[:nq, :LANES], pv2[nq:, :LANES])
        lc = jnp.where(first, pv2[:nq, LANES:], pv2[nq:, LANES:])
        alpha2 = jnp.exp2(mo2 - mn2)
        alpha = jnp.where(first, alpha2[:nq], alpha2[nq:])
        m_out += [mn2[:nq], mn2[nq:]]
        l_out.append(alpha * lo + lc)
        a_out.append(alpha * ao + pv)
    cat = lambda xs: jnp.concatenate(xs, axis=1)
    return cat(m_out), cat(l_out), cat(a_out)


def _attn_prompt_kernel(q_ref, kc_ref, kp_ref, vc_ref, vp_ref, b0_ref, b1_ref, b2_ref,
                        r0_ref, r1_ref, r2_ref, o_ref,
                        kcat, vcat, acc, m_s, l_s, ost):
    n = pl.program_id(0)
    g = pl.program_id(1)
    i0 = n * AT_I
    @pl.when(g < 2)
    def _():
        kcat[:, 0:AT_I, :] = kp_ref[...]
        kcat[:, AT_I:2 * AT_I, :] = kc_ref[...]
        vcat[:, 0:AT_I, :] = vp_ref[...]
        vcat[:, AT_I:2 * AT_I, :] = vc_ref[...]

    def key_bias(b_ref, r_ref, base):
        return b_ref[...] + jnp.where(base + r_ref[...] >= 0, 0.0, NEG)

    @pl.when(g == 0)
    def _():
        m_s[...] = jnp.full(m_s.shape, NEG, F32)
        l_s[...] = jnp.zeros(l_s.shape, F32)
        acc[...] = jnp.zeros(acc.shape, F32)

        def load(s):
            qo = pl.multiple_of(16 * s, 16)
            ko = pl.multiple_of(AT_I - 16 + 16 * s, 16)
            q16 = [q_ref[c, pl.ds(qo, 16), :] for c in range(N_CLASS)]
            k32 = [kcat[c, pl.ds(ko, 32), :] for c in range(N_CLASS)]
            v32 = [vcat[c, pl.ds(ko, 32), :] for c in range(N_CLASS)]
            cat = lambda f: jnp.concatenate([f(c) for c in range(N_CLASS)], axis=0)
            work = []
            for half in range(2):
                qh = pl.multiple_of(qo + 8 * half, 8)
                q = cat(lambda c: q16[c][8 * half:8 * half + 8])
                k = cat(lambda c: k32[c][8 + 8 * half:24 + 8 * half])
                v = cat(lambda c: v32[c][8 + 8 * half:24 + 8 * half])
                mo = cat(lambda c: m_s[c, pl.ds(qh, 8), :])
                lo = cat(lambda c: l_s[c, pl.ds(qh, 8), :])
                ao = cat(lambda c: acc[c, pl.ds(qh, 8), :])
                bias = key_bias(b0_ref, r0_ref, i0 + 16 * s + 8 * half)
                work.append((qh, (q, k, v, bias, mo, lo, ao)))
            return work

        def body(idx, carry):
            chunks = BLOCKS_PER_TRIP // 2
            work = [w for j in range(chunks) for w in load(chunks * idx + j)]
            done = [(qh, _attn_update(*args)) for qh, args in work]
            for qh, (mn, ln, an) in done:
                for c in range(N_CLASS):
                    m_s[c, pl.ds(qh, 8), :] = mn[8 * c:8 * c + 8]
                    l_s[c, pl.ds(qh, 8), :] = ln[8 * c:8 * c + 8]
                    acc[c, pl.ds(qh, 8), :] = an[8 * c:8 * c + 8]
            return carry

        lax.fori_loop(0, AT_I // 16 // (BLOCKS_PER_TRIP // 2), body, 0)

    @pl.when(g == 1)
    def _():
        def load(r4, s):
            qo = pl.multiple_of(32 * s, 32)
            ko = pl.multiple_of(AT_I - 32 + 32 * s, 32)
            cat = lambda f: jnp.concatenate([f(4 * r4 + a) for a in range(4)], axis=0)
            q = cat(lambda c: q_ref[c, pl.ds(qo, 32), :])
            k = cat(lambda c: kcat[c, pl.ds(ko, 64), :])
            v = cat(lambda c: vcat[c, pl.ds(ko, 64), :])
            mo = cat(lambda c: m_s[c, pl.ds(qo, 32), :])
            lo = cat(lambda c: l_s[c, pl.ds(qo, 32), :])
            ao = cat(lambda c: acc[c, pl.ds(qo, 32), :])
            bias = key_bias(b1_ref, r1_ref, i0 + 32 * s)
            return q, k, v, bias, mo, lo, ao

        def store(r4, s, mn, ln, an):
            qo = pl.multiple_of(32 * s, 32)
            for a in range(4):
                c = 4 * r4 + a
                m_s[c, pl.ds(qo, 32), :] = mn[32 * a:32 * a + 32]
                l_s[c, pl.ds(qo, 32), :] = ln[32 * a:32 * a + 32]
                acc[c, pl.ds(qo, 32), :] = an[32 * a:32 * a + 32]

        def body(idx, carry):
            per_class = 4 // BLOCKS_PER_TRIP
            where = [(idx // per_class, BLOCKS_PER_TRIP * (idx % per_class) + j) for j in range(BLOCKS_PER_TRIP)]
            work = [load(r4, s) for r4, s in where]
            done = [_attn_update(*args) for args in work]
            for (r4, s), res in zip(where, done):
                store(r4, s, *res)
            return carry

        lax.fori_loop(0, 16 // BLOCKS_PER_TRIP, body, 0)

    @pl.when(g == 2)
    def _():
        bias = key_bias(b2_ref, r2_ref, i0)

        def body(cc, carry):
            cs = [BLOCKS_PER_TRIP * cc + j for j in range(BLOCKS_PER_TRIP)]
            both = lambda prev, cur, c: jnp.concatenate([prev[c], cur[c]], axis=0)
            work = [(q_ref[c], both(kp_ref, kc_ref, c), both(vp_ref, vc_ref, c), bias, m_s[c], l_s[c], acc[c])
                    for c in cs]
            done = [_attn_update(*args) for args in work]
            for c, (mn, ln, an) in zip(cs, done):
                m_s[c] = mn
                l_s[c] = ln
                acc[c] = an
            return carry

        lax.fori_loop(0, N_CLASS // BLOCKS_PER_TRIP, body, 0)

        cls = _class_order()
        for c in range(N_CLASS):
            o = acc[c] / l_s[c]
            for jc in range(ATTN_WIDTH // LANES):
                ost[jc, pl.ds(cls[c], AT_I, stride=N_CLASS), :] = o[:, jc * LANES:(jc + 1) * LANES]
        o_ref[...] = jnp.concatenate([ost[jc] for jc in range(ATTN_WIDTH // LANES)], axis=1).astype(BF16)


def _attn_bias_tables():
    cls = np.asarray(_class_order())

    def bias(diff):
        return np.where((diff >= 0) & (diff <= N_KEYS_BACK), 0.0, NEG).astype(np.float32)

    c, r = np.divmod(np.arange(128), 8)
    c2, r2 = np.divmod(np.arange(256), 16)
    d0 = 16 * (r[:, None] - r2[None, :] + 8) + cls[c][:, None] - cls[c2][None, :]
    rel0 = (r2 - 8).astype(np.int32)[None, :]
    a, r = np.divmod(np.arange(128), 32)
    a2, r2 = np.divmod(np.arange(256), 64)
    d1 = 4 * (r[:, None] - r2[None, :] + 32) + a[:, None] - a2[None, :]
    rel1 = (r2 - 32).astype(np.int32)[None, :]
    r = np.arange(128)
    r2 = np.arange(256)
    d2 = r[:, None] - r2[None, :] + 128
    rel2 = (r2 - 128).astype(np.int32)[None, :]
    return (jnp.asarray(bias(d0)), jnp.asarray(bias(d1)), jnp.asarray(bias(d2)),
            jnp.asarray(rel0), jnp.asarray(rel1), jnp.asarray(rel2))


def _attn_prompt(q_r, k_r, v_r):
    n_i = q_r.shape[1]
    s = n_i * N_CLASS
    nsteps = n_i // AT_I
    b0, b1, b2, r0, r1, r2 = _attn_bias_tables()
    blk = (N_CLASS, AT_I, ATTN_WIDTH)
    cur = lambda n, g: (0, n, g)
    prev = lambda n, g: (0, jnp.maximum(n - 1, 0), g)
    return pl.pallas_call(
        _attn_prompt_kernel,
        grid=(nsteps, N_DIL),
        in_specs=[
            pl.BlockSpec(blk, cur), pl.BlockSpec(blk, cur), pl.BlockSpec(blk, prev),
            pl.BlockSpec(blk, cur), pl.BlockSpec(blk, prev),
            _const_spec(b0.shape), _const_spec(b1.shape), _const_spec(b2.shape),
            _const_spec(r0.shape), _const_spec(r1.shape), _const_spec(r2.shape),
        ],
        out_specs=pl.BlockSpec((AT_I * N_CLASS, ATTN_WIDTH), lambda n, g: (n, 0)),
        out_shape=jax.ShapeDtypeStruct((s, ATTN_WIDTH), BF16),
        scratch_shapes=[
            pltpu.VMEM((N_CLASS, 2 * AT_I, ATTN_WIDTH), BF16),
            pltpu.VMEM((N_CLASS, 2 * AT_I, ATTN_WIDTH), BF16),
            pltpu.VMEM((N_CLASS, AT_I, ATTN_WIDTH), F32),
            pltpu.VMEM((N_CLASS, AT_I, N_HEADS * LANES), F32),
            pltpu.VMEM((N_CLASS, AT_I, ATTN_WIDTH), F32),
            pltpu.VMEM((ATTN_WIDTH // LANES, AT_I * N_CLASS, LANES), F32),
        ],
        compiler_params=pltpu.CompilerParams(
            dimension_semantics=("arbitrary", "arbitrary"), vmem_limit_bytes=ATTN_VMEM_LIMIT_BYTES),
        name="attn_prompt",
    )(q_r, k_r, k_r, v_r, v_r, b0, b1, b2, r0, r1, r2)


def _proj_sample_kernel(n_b, n_t, x_ref, xb_ref, n1w_ref, w_ref, qw_ref, kw_ref, cos_ref, sina_ref, sinb_ref,
                        seg_ref, segt_ref, poolw_ref, pools_ref, state_ref,
                        ya_ref, gates_ref, q_ref, k_ref, v_ref, u_ref, kt_ref, vt_ref):
    xn = _rms(x_ref[...], n1w_ref[...]).astype(BF16)
    u = jnp.dot(xn, w_ref[:, C_U:C_U + POOL_WIDTH], preferred_element_type=F32)
    u_ref[...] = u

    def ext(j, lanes):
        if j < POOL_STATE:
            return state_ref[j, :, lanes]
        return u[(j - POOL_STATE) * n_b:(j - POOL_STATE + 1) * n_b, lanes]

    pooled = []
    for g, w in enumerate(POOL_WINDOWS):
        lanes = slice(g * POOL_GROUP_DIM, (g + 1) * POOL_GROUP_DIM)
        rows = []
        for i in range(n_t):
            s = ext(POOL_STATE + i, lanes)
            for m in range(1, w):
                s = s + ext(POOL_STATE + i - m, lanes)
            cnt = float(min(PAST_LEN + i + 1, w))
            rows.append(s / cnt - ext(POOL_STATE + i, lanes))
        pooled.append(jnp.concatenate(rows, axis=0))
    ya_ref[...] = _pool_mix_out(pooled, poolw_ref, pools_ref[...]).astype(BF16)

    gl = jnp.dot(xn, w_ref[:, C_G:C_G + N_BRANCH * D_MODEL], preferred_element_type=F32)
    gates_ref[...] = jax.nn.sigmoid(gl).astype(BF16)

    xnb = _rms(xb_ref[...], n1w_ref[...]).astype(BF16)
    cos, sina, sinb = cos_ref[...], sina_ref[...], sinb_ref[...]
    seg, segt = seg_ref[...], segt_ref[...]
    yq = jnp.dot(xnb, w_ref[:, C_Q:C_Q + QKV_WIDTH], preferred_element_type=F32)
    q_ref[...] = _head_norm_rope(yq, qw_ref[...], seg, segt, cos, sina, sinb)
    yk = jnp.dot(xnb, w_ref[:, C_K:C_K + QKV_WIDTH], preferred_element_type=F32)
    kr = _head_norm_rope(yk, kw_ref[...], seg, segt, cos, sina, sinb)
    k_ref[...] = kr
    kt_ref[...] = kr.T
    yv = jnp.dot(xnb, w_ref[:, C_V:C_V + QKV_WIDTH], preferred_element_type=F32)
    v_ref[...] = yv
    vt_ref[...] = yv.T


def _proj_sample(x, xb, n1w, w_bf, qw_row, kw_row, poolw_bf, pool_scale, state_t, n_b, n_t):
    rows = n_b * n_t
    pos = PAST_LEN + jnp.arange(rows, dtype=jnp.int32) % n_t
    cos, sina, sinb = _rope_tables(pos)
    seg, segt = _seg_mats()
    row = lambda a: a.reshape(1, -1)
    args = (x, xb, row(n1w), w_bf, row(qw_row), row(kw_row), cos, sina, sinb, seg, segt,
            poolw_bf, row(pool_scale), state_t)
    f32_qkv = jax.ShapeDtypeStruct((rows, QKV_WIDTH), F32)
    out_shape = [
        jax.ShapeDtypeStruct((rows, POOL_WIDTH), BF16),
        jax.ShapeDtypeStruct((rows, N_BRANCH * D_MODEL), BF16),
        f32_qkv, f32_qkv, f32_qkv,
        jax.ShapeDtypeStruct((rows, POOL_WIDTH), F32),
        jax.ShapeDtypeStruct((QKV_WIDTH, rows), F32),
        jax.ShapeDtypeStruct((QKV_WIDTH, rows), F32),
    ]
    return pl.pallas_call(
        functools.partial(_proj_sample_kernel, n_b, n_t),
        grid=(1,),
        in_specs=[_const_spec(a.shape) for a in args],
        out_specs=[_const_spec(o.shape) for o in out_shape],
        out_shape=out_shape,
        compiler_params=pltpu.CompilerParams(
            dimension_semantics=("arbitrary",), vmem_limit_bytes=VMEM_LIMIT_BYTES),
        name="proj_sample",
    )(*args)


HEADS_PER_STEP = 8


def _attn_roll_kernel(n_t, q_ref, kn_ref, vn_ref, newt_ref, c0_ref, c1_ref, c2_ref,
                      b0_ref, b1_ref, b2_ref, bn_ref, o_ref, o0_ref, o1_ref, o2_ref):
    caches = (c0_ref, c1_ref, c2_ref)
    outs = (o0_ref, o1_ref, o2_ref)
    biases = (b0_ref, b1_ref, b2_ref)
    nt_contract = (((1,), (1,)), ((), ()))
    col = pl.program_id(1) * n_t
    new_tile = pl.multiple_of((col // LANES) * LANES, LANES)
    new_shift = (LANES - n_t) - col % LANES
    for hh in range(HEADS_PER_STEP):
        scores, values = [], []
        for g in range(N_DIL):
            q = q_ref[hh, g].astype(BF16)
            kc = caches[g][0, hh].astype(BF16)
            scores.append(jnp.dot(q, kc, preferred_element_type=F32) + biases[g][...])
            kn = kn_ref[hh, g].astype(BF16)
            scores.append(lax.dot_general(q, kn, nt_contract, preferred_element_type=F32) + bn_ref[g])
            values.append(caches[g][1, hh].astype(BF16))
            values.append(vn_ref[hh, g].astype(BF16))
        m = functools.reduce(jnp.maximum, [jnp.max(s, axis=1, keepdims=True) for s in scores])
        l = jnp.zeros_like(m)
        o = jnp.zeros((n_t, HEAD_DIM), F32)
        for idx, (s, v) in enumerate(zip(scores, values)):
            p = jnp.exp(s - m)
            l = l + jnp.sum(p, axis=1, keepdims=True)
            pb = p.astype(BF16)
            if idx % 2 == 0:
                o = o + lax.dot_general(pb, v, nt_contract, preferred_element_type=F32)
            else:
                o = o + jnp.dot(pb, v, preferred_element_type=F32)
        o_ref[hh] = o / l

        for g in range(N_DIL):
            length = caches[g].shape[-1]
            lane = lax.broadcasted_iota(jnp.int32, (HEAD_DIM, LANES), 1)
            for kv in range(2):
                rolled = pltpu.roll(caches[g][kv, hh], length - n_t, 1)
                new = pltpu.roll(newt_ref[kv, g, hh, :, pl.ds(new_tile, LANES)], new_shift, 1)
                if length > LANES:
                    outs[g][kv, hh, :, 0:length - LANES] = rolled[:, 0:length - LANES]
                outs[g][kv, hh, :, length - LANES:length] = jnp.where(
                    lane >= LANES - n_t, new, rolled[:, length - LANES:length])


def _attn_roll_bias(n_t):
    i = np.arange(n_t)
    bias, bias_new = [], []
    for win, dil in DILATION_PAIRS:
        length = min(win, PAST_LEN)
        delta = length + i[:, None] - np.arange(length)[None, :]
        ok = (delta % dil == 0) & (delta // dil >= 1) & (delta // dil <= win // dil)
        bias.append(jnp.asarray(np.where(ok, 0.0, NEG).astype(np.float32)))
        dn = i[:, None] - i[None, :]
        okn = (dn >= 0) & (dn % dil == 0) & (dn // dil <= win // dil)
        bias_new.append(np.where(okn, 0.0, NEG).astype(np.float32))
    return bias, jnp.asarray(np.stack(bias_new))


def _attn_roll(q_h, kn_h, vn_h, new_t, caches_t):
    n_b, _, _, n_t, _ = q_h.shape
    bias, bias_new = _attn_roll_bias(n_t)
    hs = HEADS_PER_STEP
    qspec = pl.BlockSpec((None, hs, N_DIL, n_t, HEAD_DIM), lambda h, b: (b, h, 0, 0, 0))
    cspec = lambda c: pl.BlockSpec((None, 2, hs, HEAD_DIM, c.shape[-1]), lambda h, b: (b, 0, h, 0, 0))
    return pl.pallas_call(
        functools.partial(_attn_roll_kernel, n_t),
        grid=(N_HEADS // hs, n_b),
        in_specs=[qspec, qspec, qspec,
                  pl.BlockSpec((2, N_DIL, hs, HEAD_DIM, n_t * n_b), lambda h, b: (0, 0, h, 0, 0))]
                 + [cspec(c) for c in caches_t]
                 + [_const_spec(b.shape) for b in bias] + [_const_spec(bias_new.shape)],
        out_specs=[pl.BlockSpec((None, hs, n_t, HEAD_DIM), lambda h, b: (b, h, 0, 0))]
                  + [cspec(c) for c in caches_t],
        out_shape=[jax.ShapeDtypeStruct((n_b, N_HEADS, n_t, HEAD_DIM), F32)]
                  + [jax.ShapeDtypeStruct(c.shape, c.dtype) for c in caches_t],
        compiler_params=pltpu.CompilerParams(
            dimension_semantics=("arbitrary", "arbitrary"), vmem_limit_bytes=VMEM_LIMIT_BYTES),
        name="attn_roll",
    )(q_h, kn_h, vn_h, new_t, *caches_t, *bias, bias_new)


TM = 512
MOE_ROWS = 512
L_E0, L_E1, L_W0, L_W1, L_R0, L_R1 = 0, 1, 2, 3, 4, 5
L_GROUP = N_EXPERTS


def _lane_min_index(mask, lane):
    return jnp.min(jnp.where(mask, lane, float(LANES)), axis=1, keepdims=True)


def _zero_fill_step(n, xs_ref, zbuf, sem):
    slab = zbuf.shape[0]
    n_full, rem = divmod(xs_ref.shape[0], slab)
    full = lambda i: pltpu.make_async_copy(
        zbuf, xs_ref.at[pl.ds(pl.multiple_of(i * slab, MOE_ROWS), slab)], sem)
    tail = pltpu.make_async_copy(zbuf.at[pl.ds(0, max(rem, 1))],
                                 xs_ref.at[pl.ds(n_full * slab, max(rem, 1))], sem)

    @pl.when(n == 0)
    def _():
        zbuf[...] = jnp.zeros(zbuf.shape, F32)
        if rem:
            tail.start()

    @pl.when(n < n_full)
    def _():
        full(n).start()

    @pl.when(n == pl.num_programs(0) - 1)
    def _():
        for _ in range(n_full):
            full(0).wait()
        if rem:
            tail.wait()


def _merge_route_kernel(n_first, xa_ref, xb_ref, yaa_ref, yab_ref, yba_ref, ybb_ref, ga_ref, gb_ref,
                        wbp_ref, wba_ref, wo_ref, n2w_ref, wrh_ref, wrl_ref, br_ref, ltri_ref,
                        x1_ref, h_ref, route_ref, cnt_ref, xs_ref, zbuf, sem):
    n = pl.program_id(0)
    _zero_fill_step(n, xs_ref, zbuf, sem)
    first = n < n_first
    pick = lambda a, b: jnp.where(first, a[...], b[...])
    pa = jnp.dot(pick(yaa_ref, yab_ref), wbp_ref[...], preferred_element_type=F32)
    pb = jnp.dot(pick(yba_ref, ybb_ref), wba_ref[...], preferred_element_type=F32)
    gts = pick(ga_ref, gb_ref)
    merged = gts[:, :D_MODEL].astype(F32) * pa + gts[:, D_MODEL:].astype(F32) * pb
    x1 = pick(xa_ref, xb_ref) + jnp.dot(merged.astype(BF16), wo_ref[...], preferred_element_type=F32)
    x1_ref[...] = x1
    h = _rms(x1, n2w_ref[...])
    h_ref[...] = h

    h_hi = h.astype(BF16)
    h_lo = (h - h_hi.astype(F32)).astype(BF16)
    both = jnp.dot(h_hi, jnp.concatenate([wrh_ref[...], wrl_ref[...]], axis=1), preferred_element_type=F32)
    logits = (both[:, :LANES] + both[:, LANES:]
              + jnp.dot(h_lo, wrh_ref[...], preferred_element_type=F32)) + br_ref[...]
    rows = logits.shape[0]
    lane = lax.broadcasted_iota(jnp.int32, (rows, LANES), 1).astype(F32)
    is_group = (lane >= L_GROUP) & (lane < L_GROUP + N_EXPERT_GROUPS)
    gl = jnp.where(is_group, logits, NEG)
    gmax = jnp.max(gl, axis=1, keepdims=True)
    g_w = 1.0 / jnp.sum(jnp.exp(gl - gmax), axis=1, keepdims=True)
    g_sel = _lane_min_index(gl == gmax, lane) - L_GROUP
    in_group = (lane >= g_sel * EXPERTS_PER_GROUP) & (lane < (g_sel + 1.0) * EXPERTS_PER_GROUP)
    el = jnp.where(in_group, logits, NEG)
    v0 = jnp.max(el, axis=1, keepdims=True)
    e0 = _lane_min_index(el == v0, lane)
    el2 = jnp.where(lane == e0, NEG, el)
    v1 = jnp.max(el2, axis=1, keepdims=True)
    e1 = _lane_min_index(el2 == v1, lane)
    t = jnp.exp(v1 - v0)
    w0 = g_w / (1.0 + t)
    w1 = g_w * t / (1.0 + t)

    hot0 = lane == e0
    hot1 = lane == e1
    onehot = jnp.where(hot0 | hot1, 1.0, 0.0)

    @pl.when(n == 0)
    def _():
        cnt_ref[...] = jnp.zeros(cnt_ref.shape, F32)

    before = jnp.dot(ltri_ref[...], onehot.astype(BF16), preferred_element_type=F32) + cnt_ref[...]
    r0 = jnp.sum(jnp.where(hot0, before, 0.0), axis=1, keepdims=True)
    r1 = jnp.sum(jnp.where(hot1, before, 0.0), axis=1, keepdims=True)
    cnt_ref[...] = cnt_ref[...] + jnp.sum(onehot, axis=0, keepdims=True)

    rec = jnp.zeros((rows, LANES), F32)
    for ln, val in ((L_E0, e0), (L_E1, e1), (L_W0, w0), (L_W1, w1), (L_R0, r0), (L_R1, r1)):
        rec = jnp.where(lane == ln, val, rec)
    route_ref[...] = rec


def _router_weights(w_rg, b_rg, w_re, b_re):
    w = jnp.zeros((D_MODEL, LANES), F32)
    w = w.at[:, :N_EXPERTS].set(w_re).at[:, L_GROUP:L_GROUP + N_EXPERT_GROUPS].set(w_rg)
    b = jnp.zeros((1, LANES), F32)
    b = b.at[0, :N_EXPERTS].set(b_re).at[0, L_GROUP:L_GROUP + N_EXPERT_GROUPS].set(b_rg)
    w_hi = w.astype(BF16)
    w_lo = (w - w_hi.astype(F32)).astype(BF16)
    return w_hi, w_lo, b


def _merge_route(xa, xb, yaa, yab, yba, ybb, ga, gb, wbp, wba, wo, n2w, wr_hi, wr_lo, br):
    n_first = xa.shape[0] // TM
    t_all = xa.shape[0] + xb.shape[0]
    n_rows = _moe_blocks(t_all) * MOE_ROWS
    slab_rows = -(-n_rows // MOE_ROWS // (t_all // TM)) * MOE_ROWS
    assert n_rows // slab_rows <= t_all // TM
    ltri = jnp.asarray(np.tril(np.ones((TM, TM), np.float32), -1), BF16)
    ta = lambda w: pl.BlockSpec((TM, w), lambda n: (jnp.minimum(n, n_first - 1), 0))
    tb = lambda w: pl.BlockSpec((TM, w), lambda n: (jnp.maximum(n - n_first, 0), 0))
    tile = lambda w: pl.BlockSpec((TM, w), lambda n: (n, 0))
    return pl.pallas_call(
        functools.partial(_merge_route_kernel, n_first),
        grid=(t_all // TM,),
        in_specs=[
            ta(D_MODEL), tb(D_MODEL), ta(POOL_WIDTH), tb(POOL_WIDTH), ta(ATTN_WIDTH), tb(ATTN_WIDTH),
            ta(N_BRANCH * D_MODEL), tb(N_BRANCH * D_MODEL),
            _const_spec(wbp.shape), _const_spec(wba.shape), _const_spec(wo.shape),
            _const_spec((1, D_MODEL)),
            _const_spec(wr_hi.shape), _const_spec(wr_lo.shape), _const_spec(br.shape),
            _const_spec(ltri.shape),
        ],
        out_specs=[tile(D_MODEL), tile(D_MODEL), tile(LANES), _const_spec((1, LANES)),
                   pl.BlockSpec(memory_space=pl.ANY)],
        out_shape=[
            jax.ShapeDtypeStruct((t_all, D_MODEL), F32),
            jax.ShapeDtypeStruct((t_all, D_MODEL), F32),
            jax.ShapeDtypeStruct((t_all, LANES), F32),
            jax.ShapeDtypeStruct((1, LANES), F32),
            jax.ShapeDtypeStruct((n_rows, D_MODEL), F32),
        ],
        scratch_shapes=[pltpu.VMEM((slab_rows, D_MODEL), F32), pltpu.SemaphoreType.DMA],
        compiler_params=pltpu.CompilerParams(
            dimension_semantics=("arbitrary",), vmem_limit_bytes=VMEM_LIMIT_BYTES),
        name="merge_route",
    )(xa, xb, yaa, yab, yba, ybb, ga, gb, wbp, wba, wo, n2w.reshape(1, -1), wr_hi, wr_lo, br, ltri)


def _moe_blocks(n_tokens):
    return -(-(2 * n_tokens + N_EXPERTS * (MOE_ROWS - 1)) // MOE_ROWS)


def _plan_kernel(nb_pad, cnt_ref, route_ref, upper_ref, dest_ref, blk_ref):
    n = pl.program_id(0)
    cnt = cnt_ref[...]
    nblk = jnp.floor((cnt + (MOE_ROWS - 1)) * (1.0 / MOE_ROWS))
    nb8 = jnp.broadcast_to(nblk, (8, LANES)).astype(BF16)
    bstart = jnp.dot(nb8, upper_ref[...], preferred_element_type=F32)[0:1]
    bend = bstart + nblk
    pstart = bstart * MOE_ROWS

    rec = route_ref[...]
    rows = rec.shape[0]
    lane = lax.broadcasted_iota(jnp.int32, (rows, LANES), 1).astype(F32)
    col = lambda ln: jnp.sum(jnp.where(lane == ln, rec, 0.0), axis=1, keepdims=True)
    look = lambda e: jnp.sum(jnp.where(lane == e, pstart, 0.0), axis=1, keepdims=True)
    d0 = look(col(L_E0)) + col(L_R0)
    d1 = look(col(L_E1)) + col(L_R1)
    dest_ref[...] = jnp.where(lane == 0, d0, jnp.where(lane == 1, d1, 0.0)).astype(jnp.int32)

    @pl.when(n == 0)
    def _():
        b = lax.broadcasted_iota(jnp.int32, (nb_pad, LANES), 0).astype(F32)
        lane_b = lax.broadcasted_iota(jnp.int32, (nb_pad, LANES), 1)
        done = jnp.where((bend <= b) & (lane_b < N_EXPERTS), 1.0, 0.0)
        e_of_b = jnp.minimum(jnp.sum(done, axis=1, keepdims=True), float(N_EXPERTS - 1))
        total = jnp.max(jnp.where(lane_b < N_EXPERTS, bend, 0.0), axis=1, keepdims=True)
        blk_ref[...] = jnp.where(lane_b == 0, e_of_b, jnp.where(lane_b == 1, total, 0.0)).astype(jnp.int32)


def _plan(cnt, route, n_rows):
    t = route.shape[0]
    nb = n_rows // MOE_ROWS
    nb_pad = -(-nb // 8) * 8
    upper = jnp.asarray(np.triu(np.ones((LANES, LANES), np.float32), 1), BF16)
    return pl.pallas_call(
        functools.partial(_plan_kernel, nb_pad),
        grid=(t // TM,),
        in_specs=[_const_spec((1, LANES)), pl.BlockSpec((TM, LANES), lambda n: (n, 0)),
                  _const_spec((LANES, LANES))],
        out_specs=[pl.BlockSpec((TM, LANES), lambda n: (n, 0)), _const_spec((nb_pad, LANES))],
        out_shape=[jax.ShapeDtypeStruct((t, LANES), jnp.int32),
                   jax.ShapeDtypeStruct((nb_pad, LANES), jnp.int32)],
        compiler_params=pltpu.CompilerParams(
            dimension_semantics=("arbitrary",), vmem_limit_bytes=VMEM_LIMIT_BYTES),
        name="moe_plan",
    )(cnt, route, upper)


def _dispatch_kernel(dest_ref, h_ref, xs_in, xs_out, sem):
    del xs_in
    rows = h_ref.shape[0]

    def issue(i, carry):
        base = pl.multiple_of(i * 8, 8)
        for j in range(8):
            for k in range(2):
                pltpu.make_async_copy(h_ref.at[pl.ds(base + j, 1)],
                                      xs_out.at[pl.ds(dest_ref[k, base + j], 1)], sem).start()
        return carry

    lax.fori_loop(0, rows // 8, issue, 0)
    for k in range(2):
        pltpu.make_async_copy(h_ref, xs_out.at[pl.ds(0, rows)], sem).wait()


def _dispatch(dest_t, h, xs):
    t = h.shape[0]
    return pl.pallas_call(
        _dispatch_kernel,
        grid=(t // TM,),
        in_specs=[pl.BlockSpec((None, 8, TM), lambda n: (n, 0, 0), memory_space=pltpu.SMEM),
                  pl.BlockSpec((TM, D_MODEL), lambda n: (n, 0)),
                  pl.BlockSpec(memory_space=pl.ANY)],
        out_specs=pl.BlockSpec(memory_space=pl.ANY),
        out_shape=jax.ShapeDtypeStruct(xs.shape, xs.dtype),
        input_output_aliases={2: 0},
        scratch_shapes=[pltpu.SemaphoreType.DMA],
        compiler_params=pltpu.CompilerParams(
            dimension_semantics=("arbitrary",), vmem_limit_bytes=VMEM_LIMIT_BYTES),
        name="moe_dispatch",
    )(dest_t, h, xs)


def _expert_kernel(blk_ref, x_ref, wg_ref, wu_ref, wd_ref, y_ref, wg_bf, wu_bf, wd_bf):
    b = pl.program_id(0)
    used = b < blk_ref[1, 0]

    @pl.when(used)
    def _():
        prev = blk_ref[0, jnp.maximum(b - 1, 0)]

        @pl.when((b == 0) | (blk_ref[0, b] != prev))
        def _():
            wg_bf[...] = wg_ref[...].astype(BF16)
            wu_bf[...] = wu_ref[...].astype(BF16)
            wd_bf[...] = wd_ref[...].astype(BF16)

        x = x_ref[...].astype(BF16)
        gate = jnp.dot(x, wg_bf[...], preferred_element_type=F32)
        up = jnp.dot(x, wu_bf[...], preferred_element_type=F32)
        mid = (jax.nn.silu(gate) * up).astype(BF16)
        y_ref[...] = jnp.dot(mid, wd_bf[...], preferred_element_type=F32)

    @pl.when(jnp.logical_not(used))
    def _():
        y_ref[...] = jnp.zeros(y_ref.shape, F32)


def _experts(blk_t, xs, w_g, w_u, w_d):
    n_rows = xs.shape[0]
    nb = n_rows // MOE_ROWS
    last = lambda b, blk: jnp.minimum(b, blk[1, 0] - 1)
    grid_spec = pltpu.PrefetchScalarGridSpec(
        num_scalar_prefetch=1,
        grid=(nb,),
        in_specs=[
            pl.BlockSpec((MOE_ROWS, D_MODEL), lambda b, blk: (last(b, blk), 0)),
            pl.BlockSpec((None, D_MODEL, D_EXPERT), lambda b, blk: (blk[0, last(b, blk)], 0, 0)),
            pl.BlockSpec((None, D_MODEL, D_EXPERT), lambda b, blk: (blk[0, last(b, blk)], 0, 0)),
            pl.BlockSpec((None, D_EXPERT, D_MODEL), lambda b, blk: (blk[0, last(b, blk)], 0, 0)),
        ],
        out_specs=pl.BlockSpec((MOE_ROWS, D_MODEL), lambda b, blk: (b, 0)),
        scratch_shapes=[pltpu.VMEM((D_MODEL, D_EXPERT), BF16), pltpu.VMEM((D_MODEL, D_EXPERT), BF16),
                        pltpu.VMEM((D_EXPERT, D_MODEL), BF16)],
    )
    return pl.pallas_call(
        _expert_kernel,
        grid_spec=grid_spec,
        out_shape=jax.ShapeDtypeStruct((n_rows, D_MODEL), F32),
        compiler_params=pltpu.CompilerParams(
            dimension_semantics=("arbitrary",), vmem_limit_bytes=VMEM_LIMIT_BYTES),
        name="moe_experts",
    )(blk_t, xs, w_g, w_u, w_d)


def _combine_kernel(n_first, dest_ref, dnext_ref, x1_ref, route_ref, ys_ref, oa_ref, ob_ref, gbuf, sem):
    n = pl.program_id(0)
    rows = x1_ref.shape[0]
    slot = n % 2

    def gather(d_ref, s):
        def issue(i, carry):
            base = pl.multiple_of(i * 8, 8)
            for j in range(8):
                for k in range(2):
                    pltpu.make_async_copy(ys_ref.at[pl.ds(d_ref[k, base + j], 1)],
                                          gbuf.at[s, k, pl.ds(base + j, 1)], sem.at[s]).start()
            return carry

        lax.fori_loop(0, rows // 8, issue, 0)

    @pl.when(n == 0)
    def _():
        gather(dest_ref, 0)

    @pl.when(n < pl.num_programs(0) - 1)
    def _():
        gather(dnext_ref, 1 - slot)

    for k in range(2):
        pltpu.make_async_copy(ys_ref.at[pl.ds(0, rows)], gbuf.at[slot, k], sem.at[slot]).wait()
    rec = route_ref[...]
    lane = lax.broadcasted_iota(jnp.int32, rec.shape, 1)
    w0 = jnp.sum(jnp.where(lane == L_W0, rec, 0.0), axis=1, keepdims=True)
    w1 = jnp.sum(jnp.where(lane == L_W1, rec, 0.0), axis=1, keepdims=True)
    res = x1_ref[...] + (gbuf[slot, 0] * w0 + gbuf[slot, 1] * w1)

    @pl.when(n < n_first)
    def _():
        oa_ref[...] = res

    @pl.when(n >= n_first)
    def _():
        ob_ref[...] = res


def _combine(dest_t, x1, route, ys, t_first):
    t = x1.shape[0]
    n_first = t_first // TM
    return pl.pallas_call(
        functools.partial(_combine_kernel, n_first),
        grid=(t // TM,),
        in_specs=[pl.BlockSpec((None, 8, TM), lambda n: (n, 0, 0), memory_space=pltpu.SMEM),
                  pl.BlockSpec((None, 8, TM), lambda n: (jnp.minimum(n + 1, t // TM - 1), 0, 0),
                               memory_space=pltpu.SMEM),
                  pl.BlockSpec((TM, D_MODEL), lambda n: (n, 0)),
                  pl.BlockSpec((TM, LANES), lambda n: (n, 0)),
                  pl.BlockSpec(memory_space=pl.ANY)],
        out_specs=[pl.BlockSpec((TM, D_MODEL), lambda n: (jnp.minimum(n, n_first - 1), 0)),
                   pl.BlockSpec((TM, D_MODEL), lambda n: (jnp.maximum(n - n_first, 0), 0))],
        out_shape=[jax.ShapeDtypeStruct((t_first, D_MODEL), F32),
                   jax.ShapeDtypeStruct((t - t_first, D_MODEL), F32)],
        scratch_shapes=[pltpu.VMEM((2, 2, TM, D_MODEL), F32), pltpu.SemaphoreType.DMA((2,))],
        compiler_params=pltpu.CompilerParams(
            dimension_semantics=("arbitrary",), vmem_limit_bytes=VMEM_LIMIT_BYTES),
        name="moe_combine",
    )(dest_t, dest_t, x1, route, ys)


def kernel(x_prompt, x_sample, cache_kv_w128, cache_kv_w512, cache_kv_w2048, state_pool, norm1_w, w_in, q_norm_w, k_norm_w, pool_w, pool_scale, w_branch_pool, w_branch_attn, w_out, norm2_w, w_router_group, b_router_group, w_router_expert, b_router_expert, w_expert_gate, w_expert_up, w_expert_down):
    assert x_prompt.shape[0] == 1 and norm1_w.shape[0] == 1
    layer = 0
    s_p = x_prompt.shape[1]
    n_b, n_t, _ = x_sample.shape
    t_s = n_b * n_t
    t_all = s_p + t_s
    caches = (cache_kv_w128[layer], cache_kv_w512[layer], cache_kv_w2048[layer])

    w_bf = w_in[layer].astype(BF16)
    qw_row = jnp.tile(q_norm_w[layer][:, None, :], (1, N_HEADS, 1)).reshape(-1) * (HEAD_DIM ** -0.5)
    kw_row = jnp.tile(k_norm_w[layer][:, None, :], (1, N_HEADS, 1)).reshape(-1)
    poolw_bf = pool_w[layer].astype(BF16)
    wbp, wba, wo = (w.astype(BF16) for w in (w_branch_pool[layer], w_branch_attn[layer], w_out[layer]))
    wr_hi, wr_lo, br = _router_weights(w_router_group[layer], b_router_group[layer],
                                       w_router_expert[layer], b_router_expert[layer])

    ya_p, gates_p, q_r, k_r, v_r, kvt, ut = _proj_prompt(
        x_prompt[0], norm1_w[layer], w_bf, qw_row * LOG2_E, kw_row, poolw_bf, pool_scale[layer])
    yb_p = _attn_prompt(q_r, k_r, v_r)

    xs_t = jnp.transpose(x_sample, (1, 0, 2)).reshape(t_s, D_MODEL)
    state_t = jnp.transpose(state_pool[layer], (1, 0, 2))
    ya_s, gates_s, q_s, k_s, v_s, u_s, kt_s, vt_s = _proj_sample(
        xs_t, x_sample.reshape(t_s, D_MODEL), norm1_w[layer], w_bf, qw_row, kw_row, poolw_bf,
        pool_scale[layer], state_t, n_b, n_t)
    split = lambda a: a.reshape(n_b, n_t, N_DIL, N_HEADS, HEAD_DIM)
    per_head = lambda a: jnp.transpose(split(a), (0, 3, 2, 1, 4))
    new_t = jnp.stack([kt_s, vt_s]).reshape(2, N_DIL, N_HEADS, HEAD_DIM, t_s)
    caches_t = [jnp.transpose(c, (0, 2, 3, 4, 1)) for c in caches]
    o_s, *kv_t = _attn_roll(per_head(q_s), per_head(k_s), per_head(v_s), new_t, caches_t)
    yb_s = jnp.transpose(o_s, (2, 0, 1, 3)).reshape(t_s, ATTN_WIDTH).astype(BF16)
    kv_s = [jnp.transpose(c, (0, 4, 1, 2, 3)) for c in kv_t]

    x1, h, route, cnt, xs = _merge_route(x_prompt[0], xs_t, ya_p, ya_s, yb_p, yb_s, gates_p, gates_s,
                                     wbp, wba, wo, norm2_w[layer], wr_hi, wr_lo, br)

    n_rows = _moe_blocks(t_all) * MOE_ROWS
    dest, blk = _plan(cnt, route, n_rows)
    dest_t = jnp.transpose(dest[:, :8].reshape(t_all // TM, TM, 8), (0, 2, 1))
    blk_t = jnp.transpose(blk[:, :2])
    xs = _dispatch(dest_t, h, xs)
    ys = _experts(blk_t, xs, w_expert_gate[layer], w_expert_up[layer], w_expert_down[layer])
    y_p, y_s = _combine(dest_t, x1, route, ys, s_p)

    y_prompt = y_p[None]
    y_sample = jnp.transpose(y_s.reshape(n_t, n_b, D_MODEL), (1, 0, 2))
    pool_prompt = ut[1:][None, None]
    u_new = jnp.transpose(u_s.reshape(n_t, n_b, POOL_WIDTH), (1, 0, 2))
    pool_sample = jnp.concatenate([state_pool[layer][:, n_t:], u_new], axis=1)[None]
    kv_t = kvt.reshape(2, N_DIL, N_HEADS, HEAD_DIM, kvt.shape[1])
    outs = [y_prompt, y_sample, pool_prompt, pool_sample]
    for g, (win, _) in enumerate(DILATION_PAIRS):
        keep = min(win, s_p)
        kv_g = kv_t[:, g, :, :, kv_t.shape[-1] - keep:]
        outs.append(jnp.transpose(kv_g, (3, 0, 1, 2))[None, None])
        outs.append(kv_s[g][None])
    return tuple(outs)
```
